```python
import math
import jax, jax.numpy as jnp
from jax import lax
import numpy as np

D_MODEL = 1024
BATCH = 4
SEQ = 4096
DEPTH = 2
DEC_BATCH = 32
DEC_SEQ = 4
PAST_LEN = 16384
PAGE_SIZE = 128

N_HEADS_A = 8
HEAD_DIM_A = 64
D_A = N_HEADS_A * HEAD_DIM_A
ROT_DIM = HEAD_DIM_A // 4
ROPE_THETA = 500000.0
DILATED_PATTERNS = ((128, 1), (512, 4), (2048, 16))
MAX_WINDOW = 2048
ATTN_BLOCK = 32
N_HEADS_B = 4
DV_B = D_MODEL // 2
DK_B = DV_B // 2
DV_HEAD = DV_B // N_HEADS_B
DK_HEAD = DK_B // N_HEADS_B
GATE_RANK = 16
GATE_TAU = 16.0
GLA_CHUNK = 64
D_MIX = D_A + DV_B
SPLITS = (D_A, 2 * D_A, 3 * D_A, 3 * D_A + DK_B, 3 * D_A + 2 * DK_B,
          3 * D_A + 2 * DK_B + DV_B, 3 * D_A + 2 * DK_B + 2 * DV_B)
IN_COLS = 3 * D_A + 2 * DK_B + 2 * DV_B + GATE_RANK
D_FF = 2816
CONV_W = 3
EPS = 1e-6

kernel_name = "hymba_longnet_gla_convffn_step"


def rms_norm(x, g):
    xf = x.astype(jnp.float32)
    y = xf * lax.rsqrt(jnp.mean(xf * xf, axis=-1, keepdims=True) + EPS)
    return (y * g.astype(jnp.float32)).astype(x.dtype)


def partial_rope(x, pos):
    half = ROT_DIM // 2
    inv_freq = ROPE_THETA ** (-jnp.arange(half, dtype=jnp.float32) / half)
    ang = pos.astype(jnp.float32)[:, None] * inv_freq[None, :]
    cos = jnp.cos(ang)[None, :, None, :]
    sin = jnp.sin(ang)[None, :, None, :]
    x1 = x[..., :half].astype(jnp.float32)
    x2 = x[..., half:ROT_DIM].astype(jnp.float32)
    rot = jnp.concatenate([x1 * cos - x2 * sin, x2 * cos + x1 * sin], axis=-1).astype(x.dtype)
    return jnp.concatenate([rot, x[..., ROT_DIM:]], axis=-1)


def dilated_window_attention(q, k_all, v_all, offset):
    B, T, H, hd = q.shape
    blk = math.gcd(T, ATTN_BLOCK)
    nblk = T // blk
    scale = hd ** -0.5
    q_blocks = q.reshape(B, nblk, blk, H, hd).swapaxes(0, 1)

    def one_block(args):
        q_blk, b = args
        rows = offset + b * blk + jnp.arange(blk, dtype=jnp.int32)
        outs, lses = [], []
        for window, dil in DILATED_PATTERNS:
            idx = rows[:, None] - dil * jnp.arange(window // dil + 1, dtype=jnp.int32)[None, :]
            valid = idx >= 0
            idx = jnp.maximum(idx, 0)
            k_g = k_all[:, idx]
            v_g = v_all[:, idx]
            s = jnp.einsum('bqhd,bqnhd->bhqn', q_blk, k_g,
                           preferred_element_type=jnp.float32) * scale
            s = jnp.where(valid[None, None], s, -jnp.inf)
            lse = jax.nn.logsumexp(s, axis=-1)
            p = jnp.exp(s - lse[..., None])
            outs.append(jnp.einsum('bhqn,bqnhd->bqhd', p, v_g.astype(jnp.float32)))
            lses.append(lse)
        w = jax.nn.softmax(jnp.stack(lses), axis=0)
        w = w.transpose(0, 1, 3, 2)[..., None]
        return jnp.sum(w * jnp.stack(outs), axis=0).astype(q.dtype)

    out = lax.map(one_block, (q_blocks, jnp.arange(nblk, dtype=jnp.int32)))
    return out.swapaxes(0, 1).reshape(B, T, H, hd)


def gla_chunked(q, k, v, log_a, s0):
    B, T, H, dk = q.shape
    dv = v.shape[-1]
    c = math.gcd(T, GLA_CHUNK)
    n = T // c

    def to_chunks(x):
        return x.reshape(B, n, c, H, x.shape[-1]).transpose(1, 0, 3, 2, 4)

    causal = jnp.tril(jnp.ones((c, c), dtype=bool))[None, None, :, :, None]

    def step(S, inp):
        qi, ki, vi, ai = inp
        qf = qi.astype(jnp.float32)
        kf = ki.astype(jnp.float32)
        vf = vi.astype(jnp.float32)
        b = jnp.cumsum(ai.astype(jnp.float32), axis=2)
        diff = b[:, :, :, None, :] - b[:, :, None, :, :]
        decay = jnp.exp(jnp.where(causal, diff, -jnp.inf))
        attn = jnp.einsum('bhtk,bhsk,bhtsk->bhts', qf, kf, decay)
        o = (jnp.einsum('bhts,bhsv->bhtv', attn, vf)
             + jnp.einsum('bhtk,bhkv->bhtv', qf * jnp.exp(b), S))
        b_last = b[:, :, -1:, :]
        S_new = (jnp.exp(b_last[:, :, 0, :])[..., None] * S
                 + jnp.einsum('bhsk,bhsv->bhkv', kf * jnp.exp(b_last - b), vf))
        return S_new, o

    S_fin, o = lax.scan(step, s0.astype(jnp.float32),
                        (to_chunks(q), to_chunks(k), to_chunks(v), to_chunks(log_a)))
    o = o.transpose(1, 0, 3, 2, 4).reshape(B, T, H, dv)
    return o.astype(v.dtype), S_fin.astype(v.dtype)


def token_mixers(h, pos, k_buf, v_buf, s0, w_in, w_gate2, b_gate, g_gla, w_out):
    B, T, _ = h.shape
    proj = h @ w_in
    q_a, k_a, v_a, q_b, k_b, v_b, r_b, gate_lr = jnp.split(proj, SPLITS, axis=-1)
    q_a = partial_rope(q_a.reshape(B, T, N_HEADS_A, HEAD_DIM_A), pos)
    k_a = partial_rope(k_a.reshape(B, T, N_HEADS_A, HEAD_DIM_A), pos)
    v_a = v_a.reshape(B, T, N_HEADS_A, HEAD_DIM_A)
    k_all = jnp.concatenate([k_buf, k_a], axis=1)
    v_all = jnp.concatenate([v_buf, v_a], axis=1)
    o_a = dilated_window_attention(q_a, k_all, v_all, k_buf.shape[1]).reshape(B, T, D_A)
    log_a = jax.nn.log_sigmoid((gate_lr @ w_gate2 + b_gate).astype(jnp.float32)) / GATE_TAU
    q_b = q_b.reshape(B, T, N_HEADS_B, DK_HEAD) * (DK_HEAD ** -0.5)
    k_b = k_b.reshape(B, T, N_HEADS_B, DK_HEAD)
    v_b = v_b.reshape(B, T, N_HEADS_B, DV_HEAD)
    o_b, s_new = gla_chunked(q_b, k_b, v_b, log_a.reshape(B, T, N_HEADS_B, DK_HEAD), s0)
    o_b = rms_norm(o_b, g_gla) * jax.nn.silu(r_b.reshape(B, T, N_HEADS_B, DV_HEAD))
    o = jnp.concatenate([o_a, o_b.reshape(B, T, DV_B)], axis=-1)
    return o @ w_out, k_a, v_a, s_new


def conv_ffn(h, conv_buf, w_up, conv_w, conv_b, w_down):
    T = h.shape[1]
    u = h @ w_up
    ext = jnp.concatenate([conv_buf, u], axis=1)
    c = conv_b + conv_w[0] * ext[:, 0:T]
    for i in range(1, CONV_W):
        c = c + conv_w[i] * ext[:, i:i + T]
    gate, up = jnp.split(c, 2, axis=-1)
    y = jax.nn.gelu(gate, approximate=True) * up
    return y @ w_down, ext[:, T:]


def run_trunk(x, pos, k_bufs, v_bufs, gla_states, conv_bufs, params):
    (g_mix_pre, g_mix_post, g_ffn_pre, g_ffn_post, w_in, w_gate2, b_gate, g_gla, w_out,
     w_up, conv_w, conv_b, w_down) = params
    new_k, new_v, new_s, new_c = [], [], [], []
    for l in range(DEPTH):
        h = rms_norm(x, g_mix_pre[l])
        m, k_rows, v_rows, s_l = token_mixers(h, pos, k_bufs[l], v_bufs[l], gla_states[l],
                                              w_in[l], w_gate2[l], b_gate[l], g_gla[l], w_out[l])
        x = x + rms_norm(m, g_mix_post[l])
        h = rms_norm(x, g_ffn_pre[l])
        f, c_l = conv_ffn(h, conv_bufs[l], w_up[l], conv_w[l], conv_b[l], w_down[l])
        x = x + rms_norm(f, g_ffn_post[l])
        new_k.append(k_rows)
        new_v.append(v_rows)
        new_s.append(s_l)
        new_c.append(c_l)
    return x, jnp.stack(new_k), jnp.stack(new_v), jnp.stack(new_s), jnp.stack(new_c)


def setup_inputs(seed: int = 0) -> dict:
    key = jax.random.key(seed)
    ks = jax.random.split(key, 19)
    win_past = min(MAX_WINDOW, PAST_LEN)

    def nrm(k, shape, scale):
        return jax.random.normal(k, shape, jnp.float32) * scale

    return {
        "x_prompt": nrm(ks[0], (BATCH, SEQ, D_MODEL), 1.0),
        "x_sample": nrm(ks[1], (DEC_BATCH, DEC_SEQ, D_MODEL), 1.0),
        "cache_k_win": nrm(ks[2], (DEPTH, DEC_BATCH, win_past, N_HEADS_A, HEAD_DIM_A), 1.0),
        "cache_v_win": nrm(ks[3], (DEPTH, DEC_BATCH, win_past, N_HEADS_A, HEAD_DIM_A), 1.0),
        "state_gla": nrm(ks[4], (DEPTH, DEC_BATCH, N_HEADS_B, DK_HEAD, DV_HEAD), 1.0),
        "state_ffn_conv": nrm(ks[5], (DEPTH, DEC_BATCH, CONV_W - 1, 2 * D_FF), 1.0),
        "g_mix_pre": 1.0 + nrm(ks[6], (DEPTH, D_MODEL), 0.02),
        "g_mix_post": 1.0 + nrm(ks[7], (DEPTH, D_MODEL), 0.02),
        "g_ffn_pre": 1.0 + nrm(ks[8], (DEPTH, D_MODEL), 0.02),
        "g_ffn_post": 1.0 + nrm(ks[9], (DEPTH, D_MODEL), 0.02),
        "w_in": nrm(ks[10], (DEPTH, D_MODEL, IN_COLS), D_MODEL ** -0.5),
        "w_gate2": nrm(ks[11], (DEPTH, GATE_RANK, DK_B), GATE_RANK ** -0.5),
        "b_gate": nrm(ks[12], (DEPTH, DK_B), 0.01),
        "g_gla": 1.0 + nrm(ks[13], (DEPTH, DV_HEAD), 0.02),
        "w_out": nrm(ks[14], (DEPTH, D_MIX, D_MODEL), D_MIX ** -0.5),
        "w_up": nrm(ks[15], (DEPTH, D_MODEL, 2 * D_FF), D_MODEL ** -0.5),
        "conv_w": nrm(ks[16], (DEPTH, CONV_W, 2 * D_FF), CONV_W ** -0.5),
        "conv_b": nrm(ks[17], (DEPTH, 2 * D_FF), 0.01),
        "w_down": nrm(ks[18], (DEPTH, D_FF, D_MODEL), D_FF ** -0.5),
    }


def reference(x_prompt, x_sample, cache_k_win, cache_v_win, state_gla, state_ffn_conv,
              g_mix_pre, g_mix_post, g_ffn_pre, g_ffn_post, w_in, w_gate2, b_gate, g_gla,
              w_out, w_up, conv_w, conv_b, w_down):
    params = (g_mix_pre, g_mix_post, g_ffn_pre, g_ffn_post, w_in, w_gate2, b_gate, g_gla,
              w_out, w_up, conv_w, conv_b, w_down)
    B, T, _ = x_prompt.shape
    dt = x_prompt.dtype
    pos_p = jnp.arange(T, dtype=jnp.int32)
    z_kv = jnp.zeros((DEPTH, B, 0, N_HEADS_A, HEAD_DIM_A), dt)
    z_s = jnp.zeros((DEPTH, B, N_HEADS_B, DK_HEAD, DV_HEAD), dt)
    z_c = jnp.zeros((DEPTH, B, CONV_W - 1, 2 * D_FF), dt)
    y_prompt, k_p, v_p, s_p, c_p = run_trunk(x_prompt, pos_p, z_kv, z_kv, z_s, z_c, params)
    win_p = min(MAX_WINDOW, T)
    new_k_win_prompt = k_p[:, :, T - win_p:]
    new_v_win_prompt = v_p[:, :, T - win_p:]
    pos_s = PAST_LEN + jnp.arange(x_sample.shape[1], dtype=jnp.int32)
    y_sample, k_s, v_s, s_s, c_s = run_trunk(x_sample, pos_s, cache_k_win, cache_v_win,
                                             state_gla, state_ffn_conv, params)
    return (y_prompt, y_sample, new_k_win_prompt, new_v_win_prompt, s_p, c_p, k_s, v_s, s_s, c_s)
```

```python
import functools
import math

import jax
import jax.numpy as jnp
from jax import lax
from jax.experimental import pallas as pl
from jax.experimental.pallas import tpu as pltpu

F32 = jnp.float32
BF16 = jnp.bfloat16

PAST_LEN = 16384
N_HEADS_A = 8
HEAD_DIM_A = 64
D_A = N_HEADS_A * HEAD_DIM_A
ROT_HALF = HEAD_DIM_A // 8
ROPE_THETA = 500000.0
WIN_DENSE, WIN_MID, WIN_FAR = 128, 512, 2048
N_HEADS_B = 4
DK_HEAD = 64
DV_HEAD = 128
DK_B = N_HEADS_B * DK_HEAD
DV_B = N_HEADS_B * DV_HEAD
GATE_RANK = 16
GATE_TAU = 16.0
GLA_CHUNK = 64
GLA_SUB = 16
EPS = 1e-6
LANES = 128
SUBLANES = 8
NEG_BIG = -1e30
FF_CHUNK = 256
SAMPLE_PAD = 8
VMEM_LIMIT = 56 * 1024 * 1024

_NT = (((1,), (1,)), ((), ()))
_TN = (((0,), (0,)), ((), ()))


def _rms(x, g):
    ms = jnp.mean(x * x, axis=-1, keepdims=True)
    return x * lax.rsqrt(ms + EPS) * g


def _dilation_multiplicity(d):
    nonneg = d >= 0
    m0 = nonneg & (d <= WIN_DENSE)
    m1 = nonneg & (d <= WIN_MID) & ((d & 3) == 0)
    m2 = nonneg & (d <= WIN_FAR) & ((d & 15) == 0)
    return m0.astype(F32) + m1.astype(F32) + m2.astype(F32)


def _inproj_kernel(x_ref, g_ref, wm_ref, wg1_ref, wg2_ref, bg_ref, c_ref, su_ref, sd_ref,
                   qa_ref, ka_ref, kab_ref, va_ref, vab_ref, qb_ref, kb_ref, vb_ref, rb_ref, la_ref):
    h = _rms(x_ref[...], g_ref[...]).astype(BF16)

    def proj(lo, hi):
        return jnp.dot(h, wm_ref[:, lo:hi], preferred_element_type=F32)

    cos = c_ref[...]
    s_up = su_ref[...]
    s_dn = sd_ref[...]

    def rope(y):
        outs = []
        for j in range(D_A // LANES):
            yj = y[:, LANES * j:LANES * (j + 1)]
            outs.append(yj * cos + pltpu.roll(yj, ROT_HALF, 1) * s_up
                        + pltpu.roll(yj, LANES - ROT_HALF, 1) * s_dn)
        return jnp.concatenate(outs, axis=1)

    o = 0
    q = rope(proj(o, o + D_A)); o += D_A
    qa_ref[...] = (q * (HEAD_DIM_A ** -0.5)).astype(BF16)
    k = rope(proj(o, o + D_A)); o += D_A
    ka_ref[...] = k
    kab_ref[...] = k.astype(BF16)
    v = proj(o, o + D_A); o += D_A
    va_ref[...] = v
    vab_ref[...] = v.astype(BF16)
    qb_ref[...] = proj(o, o + DK_B) * (DK_HEAD ** -0.5); o += DK_B
    kb_ref[...] = proj(o, o + DK_B); o += DK_B
    vb_ref[...] = proj(o, o + DV_B); o += DV_B
    rb_ref[...] = proj(o, o + DV_B)
    gate_lr = jnp.dot(h, wg1_ref[...], preferred_element_type=F32)
    z = jnp.dot(gate_lr.astype(BF16), wg2_ref[...], preferred_element_type=F32) + bg_ref[...]
    log_sig = jnp.minimum(z, 0.0) - jnp.log1p(jnp.exp(-jnp.abs(z)))
    la_ref[...] = log_sig * (1.0 / GATE_TAU)


def _inproj(x, g, wm, wg1, wg2, bg, tabs, tm):
    n, d = x.shape
    t_rows = tabs[0].shape[0]
    n_tab = t_rows // tm
    row = lambda i: (i, 0)
    const = lambda i: (0, 0)
    tab = lambda i: (i % n_tab, 0)
    f32_out = lambda w: jax.ShapeDtypeStruct((n, w), F32)
    bf_out = lambda w: jax.ShapeDtypeStruct((n, w), BF16)
    widths = (D_A, D_A, D_A, D_A, D_A, DK_B, DK_B, DV_B, DV_B, DK_B)
    out_shape = (bf_out(D_A), f32_out(D_A), bf_out(D_A), f32_out(D_A), bf_out(D_A),
                 f32_out(DK_B), f32_out(DK_B), f32_out(DV_B), f32_out(DV_B), f32_out(DK_B))
    return pl.pallas_call(
        _inproj_kernel,
        grid=(n // tm,),
        in_specs=[pl.BlockSpec((tm, d), row), pl.BlockSpec((1, d), const),
                  pl.BlockSpec(wm.shape, const), pl.BlockSpec(wg1.shape, const),
                  pl.BlockSpec(wg2.shape, const), pl.BlockSpec(bg.shape, const),
                  pl.BlockSpec((tm, LANES), tab), pl.BlockSpec((tm, LANES), tab),
                  pl.BlockSpec((tm, LANES), tab)],
        out_specs=tuple(pl.BlockSpec((tm, w), row) for w in widths),
        out_shape=out_shape,
        compiler_params=pltpu.CompilerParams(dimension_semantics=("arbitrary",),
                                             vmem_limit_bytes=VMEM_LIMIT),
        name="inproj",
    )(x, g, wm, wg1, wg2, bg, *tabs)


def _attn_prompt_kernel(q_ref, k_ref, v_ref, o_ref, *, blk):
    qi = pl.program_id(2)
    q0 = qi * blk
    q = q_ref[...]
    lane = lax.broadcasted_iota(jnp.int32, (1, LANES), 1)
    head_sel = (lane < HEAD_DIM_A, lane >= HEAD_DIM_A)
    qh = [jnp.where(m, q, jnp.zeros_like(q)) for m in head_sel]
    rel = (lax.broadcasted_iota(jnp.int32, (blk, blk), 0)
           - lax.broadcasted_iota(jnp.int32, (blk, blk), 1))
    kb_lo = jnp.maximum(q0 - WIN_FAR, 0) // blk

    def body(kb, carry):
        k0 = pl.multiple_of(kb * blk, blk)
        kblk = k_ref[pl.ds(k0, blk), :]
        vblk = v_ref[pl.ds(k0, blk), :]
        mult = _dilation_multiplicity(rel + (q0 - k0))
        live = mult > 0.0
        new = []
        for h in range(2):
            m, l, acc = carry[h]
            s = lax.dot_general(qh[h], kblk, _NT, preferred_element_type=F32)
            s = jnp.where(live, s, NEG_BIG)
            m_new = jnp.maximum(m, jnp.max(s, axis=1, keepdims=True))
            alpha = jnp.exp(m - m_new)
            p = jnp.exp(s - m_new) * mult
            l = alpha * l + jnp.sum(p, axis=1, keepdims=True)
            acc = alpha * acc + jnp.dot(p.astype(BF16), vblk, preferred_element_type=F32)
            new.append((m_new, l, acc))
        return tuple(new)

    init = tuple((jnp.full((blk, 1), NEG_BIG, F32), jnp.zeros((blk, 1), F32),
                  jnp.zeros((blk, LANES), F32)) for _ in range(2))
    (_, l0, a0), (_, l1, a1) = lax.fori_loop(kb_lo, qi + 1, body, init)
    o_ref[...] = jnp.where(head_sel[0], a0 / l0, a1 / l1)


def _attn_prompt(q, k, v, blk):
    b, t, _ = q.shape
    n_pair = D_A // LANES
    return pl.pallas_call(
        functools.partial(_attn_prompt_kernel, blk=blk),
        grid=(b, n_pair, t // blk),
        in_specs=[pl.BlockSpec((None, blk, LANES), lambda bi, p, i: (bi, i, p)),
                  pl.BlockSpec((None, t, LANES), lambda bi, p, i: (bi, 0, p)),
                  pl.BlockSpec((None, t, LANES), lambda bi, p, i: (bi, 0, p))],
        out_specs=pl.BlockSpec((None, blk, LANES), lambda bi, p, i: (bi, i, p)),
        out_shape=jax.ShapeDtypeStruct((b, t, D_A), F32),
        compiler_params=pltpu.CompilerParams(
            dimension_semantics=("arbitrary", "arbitrary", "arbitrary"),
            vmem_limit_bytes=VMEM_LIMIT),
        name="attn_prompt",
    )(q, k, v)


def _attn_sample_kernel(q_ref, kc_ref, vc_ref, kn_ref, vn_ref, o_ref, kall, vall, *, win):
    kall[0:win, :] = kc_ref[...].astype(BF16)
    vall[0:win, :] = vc_ref[...].astype(BF16)
    pad = jnp.zeros((LANES - SAMPLE_PAD, D_A), F32)
    kall[win:win + LANES, :] = jnp.concatenate([kn_ref[...], pad], axis=0).astype(BF16)
    vall[win:win + LANES, :] = jnp.concatenate([vn_ref[...], pad], axis=0).astype(BF16)

    n_rows = SAMPLE_PAD * N_HEADS_A
    q = q_ref[...]
    qx = jnp.broadcast_to(q[:, None, :], (SAMPLE_PAD, N_HEADS_A, D_A)).reshape(n_rows, D_A)
    row = lax.broadcasted_iota(jnp.int32, (n_rows, D_A), 0)
    lane = lax.broadcasted_iota(jnp.int32, (n_rows, D_A), 1)
    own = (row % N_HEADS_A) == (lane // HEAD_DIM_A)
    qx = jnp.where(own, qx, jnp.zeros_like(qx))
    n_keys = win + LANES
    s = lax.dot_general(qx, kall[...], _NT, preferred_element_type=F32)
    tok = lax.broadcasted_iota(jnp.int32, (n_rows, n_keys), 0) // N_HEADS_A
    key = lax.broadcasted_iota(jnp.int32, (n_rows, n_keys), 1)
    mult = _dilation_multiplicity(win + tok - key)
    s = jnp.where(mult > 0.0, s, NEG_BIG)
    m = jnp.max(s, axis=1, keepdims=True)
    p = jnp.exp(s - m) * mult
    l = jnp.sum(p, axis=1, keepdims=True)
    ox = jnp.dot(p.astype(BF16), vall[...], preferred_element_type=F32) / l
    ox = jnp.where(own, ox, 0.0)
    o_ref[...] = jnp.sum(ox.reshape(SAMPLE_PAD, N_HEADS_A, D_A), axis=1)


def _attn_sample(q, kc, vc, kn, vn):
    bs, win, _ = kc.shape
    seq = lambda i: (i, 0, 0)
    return pl.pallas_call(
        functools.partial(_attn_sample_kernel, win=win),
        grid=(bs,),
        in_specs=[pl.BlockSpec((None, SAMPLE_PAD, D_A), seq),
                  pl.BlockSpec((None, win, D_A), seq), pl.BlockSpec((None, win, D_A), seq),
                  pl.BlockSpec((None, SAMPLE_PAD, D_A), seq), pl.BlockSpec((None, SAMPLE_PAD, D_A), seq)],
        out_specs=pl.BlockSpec((None, SAMPLE_PAD, D_A), seq),
        out_shape=jax.ShapeDtypeStruct((bs, SAMPLE_PAD, D_A), F32),
        scratch_shapes=[pltpu.VMEM((win + LANES, D_A), BF16), pltpu.VMEM((win + LANES, D_A), BF16)],
        compiler_params=pltpu.CompilerParams(dimension_semantics=("arbitrary",),
                                             vmem_limit_bytes=VMEM_LIMIT),
        name="attn_sample",
    )(q, kc, vc, kn, vn)


def _gla_kernel(q_ref, k_ref, v_ref, la_ref, r_ref, g_ref, s0_ref, o_ref, sf_ref, s_scr,
                *, chunk, sub, n_chunks, t_valid):
    @pl.when(pl.program_id(1) == 0)
    def _():
        s_scr[...] = s0_ref[...]

    n_sub = chunk // sub
    row = lax.broadcasted_iota(jnp.int32, (chunk, 1), 0)
    tri = (lax.broadcasted_iota(jnp.int32, (chunk, chunk), 1)
           <= lax.broadcasted_iota(jnp.int32, (chunk, chunk), 0)).astype(F32)
    ones = jnp.ones((chunk, LANES), F32)
    lane = lax.broadcasted_iota(jnp.int32, (1, LANES), 1)
    head_sel = (lane < DK_HEAD, lane >= DK_HEAD)
    causal = (lax.broadcasted_iota(jnp.int32, (chunk, chunk), 1)
              <= lax.broadcasted_iota(jnp.int32, (chunk, chunk), 0))
    g = g_ref[...]
    keep = row < t_valid

    def one_chunk(c, carry):
        rows = pl.ds(pl.multiple_of(c * chunk, chunk), chunk)
        la = la_ref[rows, :]
        if t_valid < chunk:
            la = jnp.where(keep, la, 0.0)
        b_all = jnp.dot(tri, la, precision=lax.Precision.HIGHEST, preferred_element_type=F32)
        for p in range(N_HEADS_B // 2):
            ls = slice(LANES * p, LANES * (p + 1))
            b = b_all[:, ls]
            qp = q_ref[rows, ls]
            kp = k_ref[rows, ls]
            if t_valid < chunk:
                kp = jnp.where(keep, kp, 0.0)
            b_last = b[chunk - 1:chunk, :]
            q_dec = qp * jnp.exp(b)
            k_hat = (kp * jnp.exp(b_last - b)).astype(BF16)
            b_last_col = lax.dot_general(la[:, ls], ones, _TN, precision=lax.Precision.HIGHEST,
                                         preferred_element_type=F32)
            s_old = s_scr[p]
            s_old_bf = s_old.astype(BF16)
            q_sub, k_sub = [], []
            for i in range(n_sub):
                r0, r1 = sub * i, sub * (i + 1)
                beta = b[r0:r0 + 1, :]
                q_sub.append(qp[r0:r1] * jnp.exp(b[r0:r1] - beta))
                k_sub.append((kp * jnp.exp(jnp.where(row < r1, beta - b, 0.0))).astype(BF16))
            for hh in range(2):
                h = 2 * p + hh
                vs = slice(DV_HEAD * h, DV_HEAD * (h + 1))
                vh = v_ref[rows, vs].astype(BF16)
                sel = head_sel[hh]
                o = jnp.dot(jnp.where(sel, q_dec, 0.0).astype(BF16), s_old_bf,
                            preferred_element_type=F32)
                parts = [lax.dot_general(jnp.where(sel, q_sub[i], 0.0).astype(BF16), k_sub[i], _NT,
                                         preferred_element_type=F32) for i in range(n_sub)]
                attn = parts[0] if n_sub == 1 else jnp.concatenate(parts, axis=0)
                attn = jnp.where(causal, attn, 0.0)
                o = o + jnp.dot(attn.astype(BF16), vh, preferred_element_type=F32)
                ds_ = lax.dot_general(k_hat, vh, _TN, preferred_element_type=F32)
                ks = slice(DK_HEAD * hh, DK_HEAD * (hh + 1))
                s_scr[p, ks, :] = jnp.exp(b_last_col[ks]) * s_old[ks] + ds_[ks]
                r = r_ref[rows, vs]
                o_ref[rows, vs] = _rms(o, g) * (r * (1.0 / (1.0 + jnp.exp(-r))))
        return carry

    lax.fori_loop(0, n_chunks, one_chunk, 0)
    sf_ref[...] = s_scr[...]


def _gla(q, k, v, la, r, g, s0, *, n_seq, rows_per_step, chunk, sub, t_valid):
    n = q.shape[0]
    t = n // n_seq
    steps = t // rows_per_step
    row = lambda b, i: (b * steps + i, 0)
    st = lambda b, i: (b, 0, 0, 0)
    n_pair = N_HEADS_B // 2
    kern = functools.partial(_gla_kernel, chunk=chunk, sub=sub, n_chunks=rows_per_step // chunk,
                             t_valid=t_valid)
    return pl.pallas_call(
        kern,
        grid=(n_seq, steps),
        in_specs=[pl.BlockSpec((rows_per_step, DK_B), row), pl.BlockSpec((rows_per_step, DK_B), row),
                  pl.BlockSpec((rows_per_step, DV_B), row), pl.BlockSpec((rows_per_step, DK_B), row),
                  pl.BlockSpec((rows_per_step, DV_B), row), pl.BlockSpec((1, DV_HEAD), lambda b, i: (0, 0)),
                  pl.BlockSpec((None, n_pair, LANES, DV_HEAD), st)],
        out_specs=(pl.BlockSpec((rows_per_step, DV_B), row),
                   pl.BlockSpec((None, n_pair, LANES, DV_HEAD), st)),
        out_shape=(jax.ShapeDtypeStruct((n, DV_B), F32),
                   jax.ShapeDtypeStruct((n_seq, n_pair, LANES, DV_HEAD), F32)),
        scratch_shapes=[pltpu.VMEM((n_pair, LANES, DV_HEAD), F32)],
        compiler_params=pltpu.CompilerParams(dimension_semantics=("arbitrary", "arbitrary"),
                                             vmem_limit_bytes=VMEM_LIMIT),
        name="gla",
    )(q, k, v, la, r, g, s0)


def _outproj_kernel(oa_ref, ob_ref, x_ref, w_ref, g_ref, y_ref):
    m = jnp.dot(oa_ref[...].astype(BF16), w_ref[0:D_A, :], preferred_element_type=F32)
    m = m + jnp.dot(ob_ref[...].astype(BF16), w_ref[D_A:D_A + DV_B, :], preferred_element_type=F32)
    y_ref[...] = x_ref[...] + _rms(m, g_ref[...])


def _outproj(oa, ob, x, w, g, tm):
    n, d = x.shape
    row = lambda i: (i, 0)
    const = lambda i: (0, 0)
    return pl.pallas_call(
        _outproj_kernel,
        grid=(n // tm,),
        in_specs=[pl.BlockSpec((tm, D_A), row), pl.BlockSpec((tm, DV_B), row), pl.BlockSpec((tm, d), row),
                  pl.BlockSpec(w.shape, const), pl.BlockSpec((1, d), const)],
        out_specs=pl.BlockSpec((tm, d), row),
        out_shape=jax.ShapeDtypeStruct((n, d), F32),
        compiler_params=pltpu.CompilerParams(dimension_semantics=("arbitrary",),
                                             vmem_limit_bytes=VMEM_LIMIT),
        name="outproj",
    )(oa, ob, x, w, g)


def _gelu_tanh(x):
    c = math.sqrt(2.0 / math.pi)
    return 0.5 * x * (1.0 + jnp.tanh(c * (x + 0.044715 * (x * x * x))))


def _conv_geglu(u, prev, cw, cb):
    rows, width = u.shape
    tiles = rows // SUBLANES
    u3 = u.reshape(tiles, SUBLANES, width)
    p3 = prev.reshape(tiles, SUBLANES, width)
    r = lax.broadcasted_iota(jnp.int32, (tiles, SUBLANES, width), 1)
    back1 = jnp.where(r >= 1, pltpu.roll(u3, 1, 1), pltpu.roll(p3, 1, 1)).reshape(rows, width)
    back2 = jnp.where(r >= 2, pltpu.roll(u3, 2, 1), pltpu.roll(p3, 2, 1)).reshape(rows, width)
    c = cb + cw[0:1, :] * back2 + cw[1:2, :] * back1 + cw[2:3, :] * u
    return (_gelu_tanh(c[:, :FF_CHUNK]) * c[:, FF_CHUNK:]).astype(BF16)


def _ffn_prompt_kernel(x_ref, g1_ref, wup_ref, cw_ref, cb_ref, wdn_ref, g2_ref, y_ref, tail_ref,
                       ext, carry, acc, *, n_ff, tiles_per_seq):
    tm = x_ref.shape[0]

    @pl.when(pl.program_id(0) % tiles_per_seq == 0)
    def _():
        carry[...] = jnp.zeros_like(carry)

    x = x_ref[...]
    h = _rms(x, g1_ref[...]).astype(BF16)
    acc[...] = jnp.zeros_like(acc)

    def step(j, c):
        u = jnp.dot(h, wup_ref[j], preferred_element_type=F32)
        ext[0:SUBLANES, :] = carry[j]
        ext[SUBLANES:SUBLANES + tm, :] = u
        y = _conv_geglu(u, ext[0:tm, :], cw_ref[j], cb_ref[j])
        acc[...] += jnp.dot(y, wdn_ref[j], preferred_element_type=F32)
        tail = ext[tm:tm + SUBLANES, :]
        carry[j] = tail
        tail_ref[j] = tail
        return c

    lax.fori_loop(0, n_ff, step, 0)
    y_ref[...] = x + _rms(acc[...], g2_ref[...])


def _ffn_prompt(x, g1, wup, cw, cb, wdn, g2, *, n_seq, tm):
    n, d = x.shape
    n_ff, _, width = wup.shape
    tiles_per_seq = n // n_seq // tm
    row = lambda i: (i, 0)
    c2 = lambda i: (0, 0)
    c3 = lambda i: (0, 0, 0)
    single = pl.Buffered(1)
    kern = functools.partial(_ffn_prompt_kernel, n_ff=n_ff, tiles_per_seq=tiles_per_seq)
    return pl.pallas_call(
        kern,
        grid=(n // tm,),
        in_specs=[pl.BlockSpec((tm, d), row), pl.BlockSpec((1, d), c2),
                  pl.BlockSpec(wup.shape, c3, pipeline_mode=single), pl.BlockSpec(cw.shape, c3),
                  pl.BlockSpec(cb.shape, c3), pl.BlockSpec(wdn.shape, c3, pipeline_mode=single),
                  pl.BlockSpec((1, d), c2)],
        out_specs=(pl.BlockSpec((tm, d), row),
                   pl.BlockSpec((None, n_ff, SUBLANES, width), lambda i: (i // tiles_per_seq, 0, 0, 0))),
        out_shape=(jax.ShapeDtypeStruct((n, d), F32),
                   jax.ShapeDtypeStruct((n_seq, n_ff, SUBLANES, width), F32)),
        scratch_shapes=[pltpu.VMEM((tm + SUBLANES, width), F32), pltpu.VMEM((n_ff, SUBLANES, width), F32),
                        pltpu.VMEM((tm, d), F32)],
        compiler_params=pltpu.CompilerParams(dimension_semantics=("arbitrary",),
                                             vmem_limit_bytes=VMEM_LIMIT),
        name="ffn_prompt",
    )(x, g1, wup, cw, cb, wdn, g2)


def _ffn_sample_kernel(x_ref, g1_ref, wup_ref, cw_ref, cb_ref, wdn_ref, g2_ref, prev_ref, y_ref, u_ref,
                       acc, *, n_ff):
    x = x_ref[...]
    h = _rms(x, g1_ref[...]).astype(BF16)
    acc[...] = jnp.zeros_like(acc)

    def step(j, c):
        u = jnp.dot(h, wup_ref[j], preferred_element_type=F32)
        u_ref[j] = u
        y = _conv_geglu(u, prev_ref[j], cw_ref[j], cb_ref[j])
        acc[...] += jnp.dot(y, wdn_ref[j], preferred_element_type=F32)
        return c

    lax.fori_loop(0, n_ff, step, 0)
    y_ref[...] = x + _rms(acc[...], g2_ref[...])


def _ffn_sample(x, g1, wup, cw, cb, wdn, g2, prev):
    n, d = x.shape
    n_ff, _, width = wup.shape
    full = lambda a: pl.BlockSpec(a.shape, lambda i: (0,) * a.ndim)
    return pl.pallas_call(
        functools.partial(_ffn_sample_kernel, n_ff=n_ff),
        grid=(1,),
        in_specs=[full(x), full(g1), full(wup), full(cw), full(cb), full(wdn), full(g2), full(prev)],
        out_specs=(pl.BlockSpec((n, d), lambda i: (0, 0)),
                   pl.BlockSpec((n_ff, n, width), lambda i: (0, 0, 0))),
        out_shape=(jax.ShapeDtypeStruct((n, d), F32), jax.ShapeDtypeStruct((n_ff, n, width), F32)),
        scratch_shapes=[pltpu.VMEM((n, d), F32)],
        compiler_params=pltpu.CompilerParams(dimension_semantics=("arbitrary",),
                                             vmem_limit_bytes=VMEM_LIMIT),
        name="ffn_sample",
    )(x, g1, wup, cw, cb, wdn, g2, prev)


def _rope_tables(pos):
    inv_freq = ROPE_THETA ** (-jnp.arange(ROT_HALF, dtype=F32) / ROT_HALF)
    ang = pos.astype(F32)[:, None] * inv_freq[None, :]
    cos, sin = jnp.cos(ang), jnp.sin(ang)
    n = pos.shape[0]
    rest = HEAD_DIM_A - 2 * ROT_HALF
    reps = LANES // HEAD_DIM_A
    c = jnp.tile(jnp.concatenate([cos, cos, jnp.ones((n, rest), F32)], axis=1), (1, reps))
    s_up = jnp.tile(jnp.concatenate([jnp.zeros((n, ROT_HALF), F32), sin, jnp.zeros((n, rest), F32)], axis=1),
                    (1, reps))
    s_dn = jnp.tile(jnp.concatenate([-sin, jnp.zeros((n, ROT_HALF + rest), F32)], axis=1), (1, reps))
    return c, s_up, s_dn


def _ff_chunks(a, d_ff):
    n_ff = d_ff // FF_CHUNK
    gate = a[..., :d_ff].reshape(a.shape[:-1] + (n_ff, FF_CHUNK))
    up = a[..., d_ff:].reshape(a.shape[:-1] + (n_ff, FF_CHUNK))
    return jnp.moveaxis(jnp.concatenate([gate, up], axis=-1), -2, 0)


def _ff_unchunks(a, d_ff):
    a = jnp.moveaxis(a, 0, -2)
    gate = a[..., :FF_CHUNK].reshape(a.shape[:-2] + (d_ff,))
    up = a[..., FF_CHUNK:].reshape(a.shape[:-2] + (d_ff,))
    return jnp.concatenate([gate, up], axis=-1)


def kernel(x_prompt, x_sample, cache_k_win, cache_v_win, state_gla, state_ffn_conv, g_mix_pre, g_mix_post,
           g_ffn_pre, g_ffn_post, w_in, w_gate2, b_gate, g_gla, w_out, w_up, conv_w, conv_b, w_down):
    depth = w_in.shape[0]
    b, t, d = x_prompt.shape
    bs, ts, _ = x_sample.shape
    win = cache_k_win.shape[2]
    d_ff = w_down.shape[1]
    n_ff = d_ff // FF_CHUNK
    conv_taps = conv_w.shape[1]
    n_main = 3 * D_A + 2 * DK_B + 2 * DV_B
    n_pair = N_HEADS_B // 2

    tabs_p = _rope_tables(jnp.arange(t, dtype=jnp.int32))
    pos_s = PAST_LEN + jnp.arange(SAMPLE_PAD, dtype=jnp.int32)
    tabs_s = tuple(jnp.tile(a, (bs, 1)) for a in _rope_tables(pos_s))

    xp = x_prompt.reshape(b * t, d)
    xs = jnp.pad(x_sample, ((0, 0), (0, SAMPLE_PAD - ts), (0, 0))).reshape(bs * SAMPLE_PAD, d)

    outs_p = {"k": [], "v": [], "s": [], "c": []}
    outs_s = {"k": [], "v": [], "s": [], "c": []}
    for l in range(depth):
        wm = w_in[l][:, :n_main].astype(BF16)
        wg1 = jnp.pad(w_in[l][:, n_main:], ((0, 0), (0, LANES - GATE_RANK))).astype(BF16)
        wg2 = jnp.pad(w_gate2[l], ((0, LANES - GATE_RANK), (0, 0))).astype(BF16)
        bg = b_gate[l][None, :]
        wo = w_out[l].astype(BF16)
        wup = _ff_chunks(w_up[l], d_ff).astype(BF16)
        cw = jnp.pad(_ff_chunks(conv_w[l], d_ff), ((0, 0), (0, SUBLANES - conv_taps), (0, 0)))
        cb = _ff_chunks(conv_b[l][None, :], d_ff)
        wdn = w_down[l].reshape(n_ff, FF_CHUNK, d).astype(BF16)
        g_pre, g_post = g_mix_pre[l][None, :], g_mix_post[l][None, :]
        g_f1, g_f2 = g_ffn_pre[l][None, :], g_ffn_post[l][None, :]
        gg = g_gla[l][None, :]

        qa, ka, kab, va, vab, qb, kb, vb, rb, la = _inproj(xp, g_pre, wm, wg1, wg2, bg, tabs_p, tm=256)
        oa = _attn_prompt(qa.reshape(b, t, D_A), kab.reshape(b, t, D_A), vab.reshape(b, t, D_A), blk=256)
        s0 = jnp.zeros((b, n_pair, LANES, DV_HEAD), F32)
        ob, s_fin = _gla(qb, kb, vb, la, rb, gg, s0, n_seq=b, rows_per_step=512, chunk=GLA_CHUNK,
                         sub=GLA_SUB, t_valid=GLA_CHUNK)
        x1 = _outproj(oa.reshape(b * t, D_A), ob, xp, wo, g_post, tm=512)
        xp, tail = _ffn_prompt(x1, g_f1, wup, cw, cb, wdn, g_f2, n_seq=b, tm=512)
        win_p = min(win, t)
        outs_p["k"].append(ka.reshape(b, t, N_HEADS_A, HEAD_DIM_A)[:, t - win_p:])
        outs_p["v"].append(va.reshape(b, t, N_HEADS_A, HEAD_DIM_A)[:, t - win_p:])
        outs_p["s"].append(s_fin.reshape(b, N_HEADS_B, DK_HEAD, DV_HEAD))
        outs_p["c"].append(_ff_unchunks(jnp.moveaxis(tail, 1, 0), d_ff)[:, SUBLANES - (conv_taps - 1):])

        qa, ka, kab, va, vab, qb, kb, vb, rb, la = _inproj(xs, g_pre, wm, wg1, wg2, bg, tabs_s,
                                                           tm=bs * SAMPLE_PAD)
        ka3 = ka.reshape(bs, SAMPLE_PAD, D_A)
        va3 = va.reshape(bs, SAMPLE_PAD, D_A)
        oa = _attn_sample(qa.reshape(bs, SAMPLE_PAD, D_A), cache_k_win[l].reshape(bs, win, D_A),
                          cache_v_win[l].reshape(bs, win, D_A), ka3, va3)
        s0 = state_gla[l].reshape(bs, n_pair, LANES, DV_HEAD)
        ob, s_fin = _gla(qb, kb, vb, la, rb, gg, s0, n_seq=bs, rows_per_step=SAMPLE_PAD, chunk=SAMPLE_PAD,
                         sub=SAMPLE_PAD, t_valid=ts)
        x1 = _outproj(oa.reshape(bs * SAMPLE_PAD, D_A), ob, xs, wo, g_post, tm=bs * SAMPLE_PAD)
        prev = jnp.pad(state_ffn_conv[l], ((0, 0), (SUBLANES - (conv_taps - 1), 0), (0, 0)))
        prev = _ff_chunks(prev.reshape(bs * SUBLANES, 2 * d_ff), d_ff)
        xs, u_all = _ffn_sample(x1, g_f1, wup, cw, cb, wdn, g_f2, prev)
        u_all = _ff_unchunks(u_all, d_ff).reshape(bs, SAMPLE_PAD, 2 * d_ff)
        outs_s["k"].append(ka3[:, :ts].reshape(bs, ts, N_HEADS_A, HEAD_DIM_A))
        outs_s["v"].append(va3[:, :ts].reshape(bs, ts, N_HEADS_A, HEAD_DIM_A))
        outs_s["s"].append(s_fin.reshape(bs, N_HEADS_B, DK_HEAD, DV_HEAD))
        outs_s["c"].append(u_all[:, ts - (conv_taps - 1):ts])

    y_prompt = xp.reshape(b, t, d)
    y_sample = xs.reshape(bs, SAMPLE_PAD, d)[:, :ts]
    st = lambda xs_: jnp.stack(xs_)
    return (y_prompt, y_sample, st(outs_p["k"]), st(outs_p["v"]), st(outs_p["s"]), st(outs_p["c"]),
            st(outs_s["k"]), st(outs_s["v"]), st(outs_s["s"]), st(outs_s["c"]))
```

```python
import functools
import math

import jax
import jax.numpy as jnp
from jax import lax
from jax.experimental import pallas as pl
from jax.experimental.pallas import tpu as pltpu

F32 = jnp.float32
BF16 = jnp.bfloat16

PAST_LEN = 16384
N_HEADS_A = 8
HEAD_DIM_A = 64
D_A = N_HEADS_A * HEAD_DIM_A
ROT_HALF = HEAD_DIM_A // 8
ROPE_THETA = 500000.0
WIN_DENSE, WIN_MID, WIN_FAR = 128, 512, 2048
DILATIONS = (1, 4, 16)
BAND = 128
MERGE_ROWS = 256
ATT_UNROLL = 8
N_HEADS_B = 4
DK_HEAD = 64
DV_HEAD = 128
DK_B = N_HEADS_B * DK_HEAD
DV_B = N_HEADS_B * DV_HEAD
GATE_RANK = 16
GATE_TAU = 16.0
GLA_CHUNK = 64
GLA_SUB = 16
EPS = 1e-6
LANES = 128
SUBLANES = 8
NEG_BIG = -1e30
FF_CHUNK = 256
SAMPLE_PAD = 8
VMEM_LIMIT = 56 * 1024 * 1024

_NT = (((1,), (1,)), ((), ()))
_TN = (((0,), (0,)), ((), ()))


def _rms(x, g):
    ms = jnp.mean(x * x, axis=-1, keepdims=True)
    return x * lax.rsqrt(ms + EPS) * g


def _dilation_multiplicity(d):
    nonneg = d >= 0
    m0 = nonneg & (d <= WIN_DENSE)
    m1 = nonneg & (d <= WIN_MID) & ((d & 3) == 0)
    m2 = nonneg & (d <= WIN_FAR) & ((d & 15) == 0)
    return m0.astype(F32) + m1.astype(F32) + m2.astype(F32)


def _inproj_kernel(x_ref, g_ref, wm_ref, wg1_ref, wg2_ref, bg_ref, c_ref, su_ref, sd_ref,
                   qa_ref, ka_ref, va_ref, qb_ref, kb_ref, vb_ref, rb_ref, la_ref):
    h = _rms(x_ref[...], g_ref[...]).astype(BF16)

    def proj(lo, hi):
        return jnp.dot(h, wm_ref[:, lo:hi], preferred_element_type=F32)

    cos = c_ref[...]
    s_up = su_ref[...]
    s_dn = sd_ref[...]

    def rope(y):
        outs = []
        for j in range(D_A // LANES):
            yj = y[:, LANES * j:LANES * (j + 1)]
            outs.append(yj * cos + pltpu.roll(yj, ROT_HALF, 1) * s_up
                        + pltpu.roll(yj, LANES - ROT_HALF, 1) * s_dn)
        return jnp.concatenate(outs, axis=1)

    o = 0
    q = rope(proj(o, o + D_A)); o += D_A
    qa_ref[...] = q * (HEAD_DIM_A ** -0.5)
    ka_ref[...] = rope(proj(o, o + D_A)); o += D_A
    va_ref[...] = proj(o, o + D_A); o += D_A
    qb_ref[...] = proj(o, o + DK_B) * (DK_HEAD ** -0.5); o += DK_B
    kb_ref[...] = proj(o, o + DK_B); o += DK_B
    vb_ref[...] = proj(o, o + DV_B); o += DV_B
    rb_ref[...] = proj(o, o + DV_B)
    gate_lr = jnp.dot(h, wg1_ref[...], preferred_element_type=F32)
    z = jnp.dot(gate_lr.astype(BF16), wg2_ref[...], preferred_element_type=F32) + bg_ref[...]
    log_sig = jnp.minimum(z, 0.0) - jnp.log1p(jnp.exp(-jnp.abs(z)))
    la_ref[...] = log_sig * (1.0 / GATE_TAU)


def _inproj(x, g, wm, wg1, wg2, bg, tabs, tm):
    n, d = x.shape
    t_rows = tabs[0].shape[0]
    n_tab = t_rows // tm
    row = lambda i: (i, 0)
    const = lambda i: (0, 0)
    tab = lambda i: (i % n_tab, 0)
    widths = (D_A, D_A, D_A, DK_B, DK_B, DV_B, DV_B, DK_B)
    out_shape = tuple(jax.ShapeDtypeStruct((n, w), F32) for w in widths)
    return pl.pallas_call(
        _inproj_kernel,
        grid=(n // tm,),
        in_specs=[pl.BlockSpec((tm, d), row), pl.BlockSpec((1, d), const),
                  pl.BlockSpec(wm.shape, const), pl.BlockSpec(wg1.shape, const),
                  pl.BlockSpec(wg2.shape, const), pl.BlockSpec(bg.shape, const),
                  pl.BlockSpec((tm, LANES), tab), pl.BlockSpec((tm, LANES), tab),
                  pl.BlockSpec((tm, LANES), tab)],
        out_specs=tuple(pl.BlockSpec((tm, w), row) for w in widths),
        out_shape=out_shape,
        compiler_params=pltpu.CompilerParams(dimension_semantics=("arbitrary",),
                                             vmem_limit_bytes=VMEM_LIMIT),
        name="inproj",
    )(x, g, wm, wg1, wg2, bg, *tabs)


def _attn_prompt_kernel(q_ref, k_ref, v_ref, bias_ref, o_ref, qs, ks, vs, on, ln, *, t):
    blk = BAND
    lane = lax.broadcasted_iota(jnp.int32, (1, LANES), 1)
    head0 = lane < HEAD_DIM_A
    head_sel = (head0, jnp.logical_not(head0))

    for pi, s in enumerate(DILATIONS):
        n_cls = t // s
        for src, dst in ((q_ref, qs), (k_ref, ks), (v_ref, vs)):
            def sort_class(r, c, src=src, dst=dst, pi=pi, s=s, n_cls=n_cls):
                x = src[...] if s == 1 else src[pl.ds(r, n_cls, stride=s), :]
                dst[pi, pl.ds(pl.multiple_of(r * n_cls, n_cls), n_cls), :] = x.astype(BF16)
                return c
            lax.fori_loop(0, s, sort_class, 0)

    def band_block(pi, s, idx):
        n_blk = t // s // blk
        r = idx // n_blk
        j = idx % n_blk
        q0 = pl.multiple_of(idx * blk, blk)
        w0 = pl.multiple_of(q0 - jnp.minimum(j, 1) * blk, blk)
        qb = qs[pi, pl.ds(q0, blk), :]
        kw = ks[pi, pl.ds(w0, 2 * blk), :]
        vw = vs[pi, pl.ds(w0, 2 * blk), :]
        bias = bias_ref[jnp.minimum(j, 1)]
        outs, lses = [], []
        for sel in head_sel:
            qh = jnp.where(sel, qb, jnp.zeros_like(qb))
            sc = lax.dot_general(qh, kw, _NT, preferred_element_type=F32) + bias
            m = jnp.max(sc, axis=1, keepdims=True)
            p = jnp.exp(sc - m)
            l = jnp.sum(p, axis=1, keepdims=True)
            o = jnp.dot(p.astype(BF16), vw, preferred_element_type=F32)
            outs.append(o / l)
            lses.append(m + jnp.log(l))
        rows = pl.ds(q0, blk) if s == 1 else pl.ds(r + s * (j * blk), blk, stride=s)
        on.at[pi][rows, :] = jnp.where(head0, outs[0], outs[1])
        ln.at[pi][rows, :] = jnp.where(head0, lses[0], lses[1])

    for pi, s in enumerate(DILATIONS):
        def one_block(idx, c, pi=pi, s=s):
            band_block(pi, s, idx)
            return c

        lax.fori_loop(0, t // blk, one_block, 0, unroll=ATT_UNROLL)

    def merge(i, c):
        rows = pl.ds(pl.multiple_of(i * MERGE_ROWS, MERGE_ROWS), MERGE_ROWS)
        lse = [ln[pi, rows, :] for pi in range(len(DILATIONS))]
        top = jnp.maximum(jnp.maximum(lse[0], lse[1]), lse[2])
        w = [jnp.exp(x - top) for x in lse]
        num = w[0] * on[0, rows, :] + w[1] * on[1, rows, :] + w[2] * on[2, rows, :]
        o_ref[rows, :] = num / (w[0] + w[1] + w[2])
        return c

    lax.fori_loop(0, t // MERGE_ROWS, merge, 0)


def _band_bias():
    iq = jnp.arange(BAND, dtype=jnp.int32)[:, None]
    c = jnp.arange(2 * BAND, dtype=jnp.int32)[None, :]
    d = jnp.stack([iq - c, iq + BAND - c])
    return jnp.where((d >= 0) & (d <= BAND), 0.0, NEG_BIG).astype(F32)


def _attn_prompt(q, k, v):
    b, t, _ = q.shape
    n_pair = D_A // LANES
    n_pat = len(DILATIONS)
    pair = lambda bi, p: (bi, 0, p)
    bias = _band_bias()
    return pl.pallas_call(
        functools.partial(_attn_prompt_kernel, t=t),
        grid=(b, n_pair),
        in_specs=[pl.BlockSpec((None, t, LANES), pair), pl.BlockSpec((None, t, LANES), pair),
                  pl.BlockSpec((None, t, LANES), pair), pl.BlockSpec(bias.shape, lambda bi, p: (0, 0, 0))],
        out_specs=pl.BlockSpec((None, t, LANES), pair),
        out_shape=jax.ShapeDtypeStruct((b, t, D_A), F32),
        scratch_shapes=[pltpu.VMEM((n_pat, t, LANES), BF16)] * 3 + [pltpu.VMEM((n_pat, t, LANES), F32)] * 2,
        compiler_params=pltpu.CompilerParams(dimension_semantics=("arbitrary", "arbitrary"),
                                             vmem_limit_bytes=VMEM_LIMIT),
        name="attn_prompt",
    )(q, k, v, bias)


def _attn_sample_kernel(q_ref, kc_ref, vc_ref, kn_ref, vn_ref, o_ref, kall, vall, *, win):
    kall[0:win, :] = kc_ref[...].astype(BF16)
    vall[0:win, :] = vc_ref[...].astype(BF16)
    pad = jnp.zeros((LANES - SAMPLE_PAD, D_A), F32)
    kall[win:win + LANES, :] = jnp.concatenate([kn_ref[...], pad], axis=0).astype(BF16)
    vall[win:win + LANES, :] = jnp.concatenate([vn_ref[...], pad], axis=0).astype(BF16)

    n_rows = SAMPLE_PAD * N_HEADS_A
    q = q_ref[...].astype(BF16)
    qx = jnp.broadcast_to(q[:, None, :], (SAMPLE_PAD, N_HEADS_A, D_A)).reshape(n_rows, D_A)
    row = lax.broadcasted_iota(jnp.int32, (n_rows, D_A), 0)
    lane = lax.broadcasted_iota(jnp.int32, (n_rows, D_A), 1)
    own = (row % N_HEADS_A) == (lane // HEAD_DIM_A)
    qx = jnp.where(own, qx, jnp.zeros_like(qx))
    n_keys = win + LANES
    s = lax.dot_general(qx, kall[...], _NT, preferred_element_type=F32)
    tok = lax.broadcasted_iota(jnp.int32, (n_rows, n_keys), 0) // N_HEADS_A
    key = lax.broadcasted_iota(jnp.int32, (n_rows, n_keys), 1)
    mult = _dilation_multiplicity(win + tok - key)
    s = jnp.where(mult > 0.0, s, NEG_BIG)
    m = jnp.max(s, axis=1, keepdims=True)
    p = jnp.exp(s - m) * mult
    l = jnp.sum(p, axis=1, keepdims=True)
    ox = jnp.dot(p.astype(BF16), vall[...], preferred_element_type=F32) / l
    ox = jnp.where(own, ox, 0.0)
    o_ref[...] = jnp.sum(ox.reshape(SAMPLE_PAD, N_HEADS_A, D_A), axis=1)


def _attn_sample(q, kc, vc, kn, vn):
    bs, win, _ = kc.shape
    seq = lambda i: (i, 0, 0)
    return pl.pallas_call(
        functools.partial(_attn_sample_kernel, win=win),
        grid=(bs,),
        in_specs=[pl.BlockSpec((None, SAMPLE_PAD, D_A), seq),
                  pl.BlockSpec((None, win, D_A), seq), pl.BlockSpec((None, win, D_A), seq),
                  pl.BlockSpec((None, SAMPLE_PAD, D_A), seq), pl.BlockSpec((None, SAMPLE_PAD, D_A), seq)],
        out_specs=pl.BlockSpec((None, SAMPLE_PAD, D_A), seq),
        out_shape=jax.ShapeDtypeStruct((bs, SAMPLE_PAD, D_A), F32),
        scratch_shapes=[pltpu.VMEM((win + LANES, D_A), BF16), pltpu.VMEM((win + LANES, D_A), BF16)],
        compiler_params=pltpu.CompilerParams(dimension_semantics=("arbitrary",),
                                             vmem_limit_bytes=VMEM_LIMIT),
        name="attn_sample",
    )(q, kc, vc, kn, vn)


def _gla_kernel(q_ref, k_ref, v_ref, la_ref, r_ref, g_ref, s0_ref, o_ref, sf_ref, s_scr,
                *, chunk, sub, n_chunks, t_valid):
    @pl.when(pl.program_id(1) == 0)
    def _():
        s_scr[...] = s0_ref[...]

    n_sub = chunk // sub
    row = lax.broadcasted_iota(jnp.int32, (chunk, 1), 0)
    tri = (lax.broadcasted_iota(jnp.int32, (chunk, chunk), 1)
           <= lax.broadcasted_iota(jnp.int32, (chunk, chunk), 0)).astype(F32)
    ones = jnp.ones((chunk, LANES), F32)
    lane = lax.broadcasted_iota(jnp.int32, (1, LANES), 1)
    head_sel = (lane < DK_HEAD, lane >= DK_HEAD)
    causal = (lax.broadcasted_iota(jnp.int32, (chunk, chunk), 1)
              <= lax.broadcasted_iota(jnp.int32, (chunk, chunk), 0))
    g = g_ref[...]
    keep = row < t_valid

    def one_chunk(c, carry):
        rows = pl.ds(pl.multiple_of(c * chunk, chunk), chunk)
        la = la_ref[rows, :]
        if t_valid < chunk:
            la = jnp.where(keep, la, 0.0)
        b_all = jnp.dot(tri, la, precision=lax.Precision.HIGHEST, preferred_element_type=F32)
        for p in range(N_HEADS_B // 2):
            ls = slice(LANES * p, LANES * (p + 1))
            b = b_all[:, ls]
            qp = q_ref[rows, ls]
            kp = k_ref[rows, ls]
            if t_valid < chunk:
                kp = jnp.where(keep, kp, 0.0)
            b_last = b[chunk - 1:chunk, :]
            q_dec = qp * jnp.exp(b)
            k_hat = (kp * jnp.exp(b_last - b)).astype(BF16)
            b_last_col = lax.dot_general(la[:, ls], ones, _TN, precision=lax.Precision.HIGHEST,
                                         preferred_element_type=F32)
            s_old = s_scr[p]
            s_old_bf = s_old.astype(BF16)
            q_sub, k_sub = [], []
            for i in range(n_sub):
                r0, r1 = sub * i, sub * (i + 1)
                beta = b[r0:r0 + 1, :]
                q_sub.append(qp[r0:r1] * jnp.exp(b[r0:r1] - beta))
                k_sub.append((kp * jnp.exp(jnp.where(row < r1, beta - b, 0.0))).astype(BF16))
            for hh in range(2):
                h = 2 * p + hh
                vs = slice(DV_HEAD * h, DV_HEAD * (h + 1))
                vh = v_ref[rows, vs].astype(BF16)
                sel = head_sel[hh]
                o = jnp.dot(jnp.where(sel, q_dec, 0.0).astype(BF16), s_old_bf,
                            preferred_element_type=F32)
                parts = [lax.dot_general(jnp.where(sel, q_sub[i], 0.0).astype(BF16), k_sub[i], _NT,
                                         preferred_element_type=F32) for i in range(n_sub)]
                attn = parts[0] if n_sub == 1 else jnp.concatenate(parts, axis=0)
                attn = jnp.where(causal, attn, 0.0)
                o = o + jnp.dot(attn.astype(BF16), vh, preferred_element_type=F32)
                ds_ = lax.dot_general(k_hat, vh, _TN, preferred_element_type=F32)
                ks = slice(DK_HEAD * hh, DK_HEAD * (hh + 1))
                s_scr[p, ks, :] = jnp.exp(b_last_col[ks]) * s_old[ks] + ds_[ks]
                r = r_ref[rows, vs]
                o_ref[rows, vs] = _rms(o, g) * (r * (1.0 / (1.0 + jnp.exp(-r))))
        return carry

    lax.fori_loop(0, n_chunks, one_chunk, 0)
    sf_ref[...] = s_scr[...]


def _gla(q, k, v, la, r, g, s0, *, n_seq, rows_per_step, chunk, sub, t_valid):
    n = q.shape[0]
    t = n // n_seq
    steps = t // rows_per_step
    row = lambda b, i: (b * steps + i, 0)
    st = lambda b, i: (b, 0, 0, 0)
    n_pair = N_HEADS_B // 2
    kern = functools.partial(_gla_kernel, chunk=chunk, sub=sub, n_chunks=rows_per_step // chunk,
                             t_valid=t_valid)
    return pl.pallas_call(
        kern,
        grid=(n_seq, steps),
        in_specs=[pl.BlockSpec((rows_per_step, DK_B), row), pl.BlockSpec((rows_per_step, DK_B), row),
                  pl.BlockSpec((rows_per_step, DV_B), row), pl.BlockSpec((rows_per_step, DK_B), row),
                  pl.BlockSpec((rows_per_step, DV_B), row), pl.BlockSpec((1, DV_HEAD), lambda b, i: (0, 0)),
                  pl.BlockSpec((None, n_pair, LANES, DV_HEAD), st)],
        out_specs=(pl.BlockSpec((rows_per_step, DV_B), row),
                   pl.BlockSpec((None, n_pair, LANES, DV_HEAD), st)),
        out_shape=(jax.ShapeDtypeStruct((n, DV_B), F32),
                   jax.ShapeDtypeStruct((n_seq, n_pair, LANES, DV_HEAD), F32)),
        scratch_shapes=[pltpu.VMEM((n_pair, LANES, DV_HEAD), F32)],
        compiler_params=pltpu.CompilerParams(dimension_semantics=("arbitrary", "arbitrary"),
                                             vmem_limit_bytes=VMEM_LIMIT),
        name="gla",
    )(q, k, v, la, r, g, s0)


def _outproj_kernel(oa_ref, ob_ref, x_ref, w_ref, g_ref, y_ref):
    m = jnp.dot(oa_ref[...].astype(BF16), w_ref[0:D_A, :], preferred_element_type=F32)
    m = m + jnp.dot(ob_ref[...].astype(BF16), w_ref[D_A:D_A + DV_B, :], preferred_element_type=F32)
    y_ref[...] = x_ref[...] + _rms(m, g_ref[...])


def _outproj(oa, ob, x, w, g, tm):
    n, d = x.shape
    row = lambda i: (i, 0)
    const = lambda i: (0, 0)
    return pl.pallas_call(
        _outproj_kernel,
        grid=(n // tm,),
        in_specs=[pl.BlockSpec((tm, D_A), row), pl.BlockSpec((tm, DV_B), row), pl.BlockSpec((tm, d), row),
                  pl.BlockSpec(w.shape, const), pl.BlockSpec((1, d), const)],
        out_specs=pl.BlockSpec((tm, d), row),
        out_shape=jax.ShapeDtypeStruct((n, d), F32),
        compiler_params=pltpu.CompilerParams(dimension_semantics=("arbitrary",),
                                             vmem_limit_bytes=VMEM_LIMIT),
        name="outproj",
    )(oa, ob, x, w, g)


def _gelu_tanh(x):
    c = math.sqrt(2.0 / math.pi)
    return 0.5 * x * (1.0 + jnp.tanh(c * (x + 0.044715 * (x * x * x))))


def _conv_geglu(u, prev, cw, cb):
    rows, width = u.shape
    tiles = rows // SUBLANES
    u3 = u.reshape(tiles, SUBLANES, width)
    p3 = prev.reshape(tiles, SUBLANES, width)
    r = lax.broadcasted_iota(jnp.int32, (tiles, SUBLANES, width), 1)
    back1 = jnp.where(r >= 1, pltpu.roll(u3, 1, 1), pltpu.roll(p3, 1, 1)).reshape(rows, width)
    back2 = jnp.where(r >= 2, pltpu.roll(u3, 2, 1), pltpu.roll(p3, 2, 1)).reshape(rows, width)
    c = cb + cw[0:1, :] * back2 + cw[1:2, :] * back1 + cw[2:3, :] * u
    return (_gelu_tanh(c[:, :FF_CHUNK]) * c[:, FF_CHUNK:]).astype(BF16)


def _ffn_prompt_kernel(x_ref, g1_ref, wup_ref, cw_ref, cb_ref, wdn_ref, g2_ref, y_ref, tail_ref,
                       ext, carry, acc, *, n_ff, tiles_per_seq):
    tm = x_ref.shape[0]

    @pl.when(pl.program_id(0) % tiles_per_seq == 0)
    def _():
        carry[...] = jnp.zeros_like(carry)

    x = x_ref[...]
    h = _rms(x, g1_ref[...]).astype(BF16)
    acc[...] = jnp.zeros_like(acc)

    def step(j, c):
        u = jnp.dot(h, wup_ref[j], preferred_element_type=F32)
        ext[0:SUBLANES, :] = carry[j]
        ext[SUBLANES:SUBLANES + tm, :] = u
        y = _conv_geglu(u, ext[0:tm, :], cw_ref[j], cb_ref[j])
        acc[...] += jnp.dot(y, wdn_ref[j], preferred_element_type=F32)
        tail = ext[tm:tm + SUBLANES, :]
        carry[j] = tail
        tail_ref[j] = tail
        return c

    lax.fori_loop(0, n_ff, step, 0)
    y_ref[...] = x + _rms(acc[...], g2_ref[...])


def _ffn_prompt(x, g1, wup, cw, cb, wdn, g2, *, n_seq, tm):
    n, d = x.shape
    n_ff, _, width = wup.shape
    tiles_per_seq = n // n_seq // tm
    row = lambda i: (i, 0)
    c2 = lambda i: (0, 0)
    c3 = lambda i: (0, 0, 0)
    single = pl.Buffered(1)
    kern = functools.partial(_ffn_prompt_kernel, n_ff=n_ff, tiles_per_seq=tiles_per_seq)
    return pl.pallas_call(
        kern,
        grid=(n // tm,),
        in_specs=[pl.BlockSpec((tm, d), row), pl.BlockSpec((1, d), c2),
                  pl.BlockSpec(wup.shape, c3, pipeline_mode=single), pl.BlockSpec(cw.shape, c3),
                  pl.BlockSpec(cb.shape, c3), pl.BlockSpec(wdn.shape, c3, pipeline_mode=single),
                  pl.BlockSpec((1, d), c2)],
        out_specs=(pl.BlockSpec((tm, d), row),
                   pl.BlockSpec((None, n_ff, SUBLANES, width), lambda i: (i // tiles_per_seq, 0, 0, 0))),
        out_shape=(jax.ShapeDtypeStruct((n, d), F32),
                   jax.ShapeDtypeStruct((n_seq, n_ff, SUBLANES, width), F32)),
        scratch_shapes=[pltpu.VMEM((tm + SUBLANES, width), F32), pltpu.VMEM((n_ff, SUBLANES, width), F32),
                        pltpu.VMEM((tm, d), F32)],
        compiler_params=pltpu.CompilerParams(dimension_semantics=("arbitrary",),
                                             vmem_limit_bytes=VMEM_LIMIT),
        name="ffn_prompt",
    )(x, g1, wup, cw, cb, wdn, g2)


def _ffn_sample_kernel(x_ref, g1_ref, wup_ref, cw_ref, cb_ref, wdn_ref, g2_ref, prev_ref, y_ref, u_ref,
                       acc, *, n_ff):
    x = x_ref[...]
    h = _rms(x, g1_ref[...]).astype(BF16)
    acc[...] = jnp.zeros_like(acc)

    def step(j, c):
        u = jnp.dot(h, wup_ref[j], preferred_element_type=F32)
        u_ref[j] = u
        y = _conv_geglu(u, prev_ref[j], cw_ref[j], cb_ref[j])
        acc[...] += jnp.dot(y, wdn_ref[j], preferred_element_type=F32)
        return c

    lax.fori_loop(0, n_ff, step, 0)
    y_ref[...] = x + _rms(acc[...], g2_ref[...])


def _ffn_sample(x, g1, wup, cw, cb, wdn, g2, prev):
    n, d = x.shape
    n_ff, _, width = wup.shape
    full = lambda a: pl.BlockSpec(a.shape, lambda i: (0,) * a.ndim)
    return pl.pallas_call(
        functools.partial(_ffn_sample_kernel, n_ff=n_ff),
        grid=(1,),
        in_specs=[full(x), full(g1), full(wup), full(cw), full(cb), full(wdn), full(g2), full(prev)],
        out_specs=(pl.BlockSpec((n, d), lambda i: (0, 0)),
                   pl.BlockSpec((n_ff, n, width), lambda i: (0, 0, 0))),
        out_shape=(jax.ShapeDtypeStruct((n, d), F32), jax.ShapeDtypeStruct((n_ff, n, width), F32)),
        scratch_shapes=[pltpu.VMEM((n, d), F32)],
        compiler_params=pltpu.CompilerParams(dimension_semantics=("arbitrary",),
                                             vmem_limit_bytes=VMEM_LIMIT),
        name="ffn_sample",
    )(x, g1, wup, cw, cb, wdn, g2, prev)


def _rope_tables(pos):
    inv_freq = ROPE_THETA ** (-jnp.arange(ROT_HALF, dtype=F32) / ROT_HALF)
    ang = pos.astype(F32)[:, None] * inv_freq[None, :]
    cos, sin = jnp.cos(ang), jnp.sin(ang)
    n = pos.shape[0]
    rest = HEAD_DIM_A - 2 * ROT_HALF
    reps = LANES // HEAD_DIM_A
    c = jnp.tile(jnp.concatenate([cos, cos, jnp.ones((n, rest), F32)], axis=1), (1, reps))
    s_up = jnp.tile(jnp.concatenate([jnp.zeros((n, ROT_HALF), F32), sin, jnp.zeros((n, rest), F32)], axis=1),
                    (1, reps))
    s_dn = jnp.tile(jnp.concatenate([-sin, jnp.zeros((n, ROT_HALF + rest), F32)], axis=1), (1, reps))
    return c, s_up, s_dn


def _ff_chunks(a, d_ff):
    n_ff = d_ff // FF_CHUNK
    gate = a[..., :d_ff].reshape(a.shape[:-1] + (n_ff, FF_CHUNK))
    up = a[..., d_ff:].reshape(a.shape[:-1] + (n_ff, FF_CHUNK))
    return jnp.moveaxis(jnp.concatenate([gate, up], axis=-1), -2, 0)


def _ff_unchunks(a, d_ff):
    a = jnp.moveaxis(a, 0, -2)
    gate = a[..., :FF_CHUNK].reshape(a.shape[:-2] + (d_ff,))
    up = a[..., FF_CHUNK:].reshape(a.shape[:-2] + (d_ff,))
    return jnp.concatenate([gate, up], axis=-1)


def kernel(x_prompt, x_sample, cache_k_win, cache_v_win, state_gla, state_ffn_conv, g_mix_pre, g_mix_post,
           g_ffn_pre, g_ffn_post, w_in, w_gate2, b_gate, g_gla, w_out, w_up, conv_w, conv_b, w_down):
    depth = w_in.shape[0]
    b, t, d = x_prompt.shape
    bs, ts, _ = x_sample.shape
    win = cache_k_win.shape[2]
    d_ff = w_down.shape[1]
    n_ff = d_ff // FF_CHUNK
    conv_taps = conv_w.shape[1]
    n_main = 3 * D_A + 2 * DK_B + 2 * DV_B
    n_pair = N_HEADS_B // 2

    tabs_p = _rope_tables(jnp.arange(t, dtype=jnp.int32))
    pos_s = PAST_LEN + jnp.arange(SAMPLE_PAD, dtype=jnp.int32)
    tabs_s = tuple(jnp.tile(a, (bs, 1)) for a in _rope_tables(pos_s))

    xp = x_prompt.reshape(b * t, d)
    xs = jnp.pad(x_sample, ((0, 0), (0, SAMPLE_PAD - ts), (0, 0))).reshape(bs * SAMPLE_PAD, d)

    outs_p = {"k": [], "v": [], "s": [], "c": []}
    outs_s = {"k": [], "v": [], "s": [], "c": []}
    for l in range(depth):
        wm = w_in[l][:, :n_main].astype(BF16)
        wg1 = jnp.pad(w_in[l][:, n_main:], ((0, 0), (0, LANES - GATE_RANK))).astype(BF16)
        wg2 = jnp.pad(w_gate2[l], ((0, LANES - GATE_RANK), (0, 0))).astype(BF16)
        bg = b_gate[l][None, :]
        wo = w_out[l].astype(BF16)
        wup = _ff_chunks(w_up[l], d_ff).astype(BF16)
        cw = jnp.pad(_ff_chunks(conv_w[l], d_ff), ((0, 0), (0, SUBLANES - conv_taps), (0, 0)))
        cb = _ff_chunks(conv_b[l][None, :], d_ff)
        wdn = w_down[l].reshape(n_ff, FF_CHUNK, d).astype(BF16)
        g_pre, g_post = g_mix_pre[l][None, :], g_mix_post[l][None, :]
        g_f1, g_f2 = g_ffn_pre[l][None, :], g_ffn_post[l][None, :]
        gg = g_gla[l][None, :]

        qa, ka, va, qb, kb, vb, rb, la = _inproj(xp, g_pre, wm, wg1, wg2, bg, tabs_p, tm=256)
        oa = _attn_prompt(qa.reshape(b, t, D_A), ka.reshape(b, t, D_A), va.reshape(b, t, D_A))
        s0 = jnp.zeros((b, n_pair, LANES, DV_HEAD), F32)
        ob, s_fin = _gla(qb, kb, vb, la, rb, gg, s0, n_seq=b, rows_per_step=512, chunk=GLA_CHUNK,
                         sub=GLA_SUB, t_valid=GLA_CHUNK)
        x1 = _outproj(oa.reshape(b * t, D_A), ob, xp, wo, g_post, tm=512)
        xp, tail = _ffn_prompt(x1, g_f1, wup, cw, cb, wdn, g_f2, n_seq=b, tm=512)
        win_p = min(win, t)
        outs_p["k"].append(ka.reshape(b, t, N_HEADS_A, HEAD_DIM_A)[:, t - win_p:])
        outs_p["v"].append(va.reshape(b, t, N_HEADS_A, HEAD_DIM_A)[:, t - win_p:])
        outs_p["s"].append(s_fin.reshape(b, N_HEADS_B, DK_HEAD, DV_HEAD))
        outs_p["c"].append(_ff_unchunks(jnp.moveaxis(tail, 1, 0), d_ff)[:, SUBLANES - (conv_taps - 1):])

        qa, ka, va, qb, kb, vb, rb, la = _inproj(xs, g_pre, wm, wg1, wg2, bg, tabs_s, tm=bs * SAMPLE_PAD)
        ka3 = ka.reshape(bs, SAMPLE_PAD, D_A)
        va3 = va.reshape(bs, SAMPLE_PAD, D_A)
        oa = _attn_sample(qa.reshape(bs, SAMPLE_PAD, D_A), cache_k_win[l].reshape(bs, win, D_A),
                          cache_v_win[l].reshape(bs, win, D_A), ka3, va3)
        s0 = state_gla[l].reshape(bs, n_pair, LANES, DV_HEAD)
        ob, s_fin = _gla(qb, kb, vb, la, rb, gg, s0, n_seq=bs, rows_per_step=SAMPLE_PAD, chunk=SAMPLE_PAD,
                         sub=SAMPLE_PAD, t_valid=ts)
        x1 = _outproj(oa.reshape(bs * SAMPLE_PAD, D_A), ob, xs, wo, g_post, tm=bs * SAMPLE_PAD)
        prev = jnp.pad(state_ffn_conv[l], ((0, 0), (SUBLANES - (conv_taps - 1), 0), (0, 0)))
        prev = _ff_chunks(prev.reshape(bs * SUBLANES, 2 * d_ff), d_ff)
        xs, u_all = _ffn_sample(x1, g_f1, wup, cw, cb, wdn, g_f2, prev)
        u_all = _ff_unchunks(u_all, d_ff).reshape(bs, SAMPLE_PAD, 2 * d_ff)
        outs_s["k"].append(ka3[:, :ts].reshape(bs, ts, N_HEADS_A, HEAD_DIM_A))
        outs_s["v"].append(va3[:, :ts].reshape(bs, ts, N_HEADS_A, HEAD_DIM_A))
        outs_s["s"].append(s_fin.reshape(bs, N_HEADS_B, DK_HEAD, DV_HEAD))
        outs_s["c"].append(u_all[:, ts - (conv_taps - 1):ts])

    y_prompt = xp.reshape(b, t, d)
    y_sample = xs.reshape(bs, SAMPLE_PAD, d)[:, :ts]
    st = lambda xs_: jnp.stack(xs_)
    return (y_prompt, y_sample, st(outs_p["k"]), st(outs_p["v"]), st(outs_p["s"]), st(outs_p["c"]),
            st(outs_s["k"]), st(outs_s["v"]), st(outs_s["s"]), st(outs_s["c"]))
```

```python
import functools
import math

import jax
import jax.numpy as jnp
from jax import lax
from jax.experimental import pallas as pl
from jax.experimental.pallas import tpu as pltpu

F32 = jnp.float32
BF16 = jnp.bfloat16

PAST_LEN = 16384
N_HEADS_A = 8
HEAD_DIM_A = 64
D_A = N_HEADS_A * HEAD_DIM_A
ROT_HALF = HEAD_DIM_A // 8
ROPE_THETA = 500000.0
WIN_DENSE, WIN_MID, WIN_FAR = 128, 512, 2048
DILATIONS = (1, 4, 16)
BAND = 128
MERGE_ROWS = 256
ATT_UNROLL = 8
N_HEADS_B = 4
DK_HEAD = 64
DV_HEAD = 128
DK_B = N_HEADS_B * DK_HEAD
DV_B = N_HEADS_B * DV_HEAD
GATE_RANK = 16
GATE_TAU = 16.0
GLA_CHUNK = 64
GLA_SUB = 16
EPS = 1e-6
LANES = 128
SUBLANES = 8
NEG_BIG = -1e30
FF_CHUNK = 256
SAMPLE_PAD = 8
VMEM_LIMIT = 56 * 1024 * 1024

_NT = (((1,), (1,)), ((), ()))
_TN = (((0,), (0,)), ((), ()))


def _rms(x, g):
    ms = jnp.mean(x * x, axis=-1, keepdims=True)
    return x * lax.rsqrt(ms + EPS) * g


def _dilation_multiplicity(d):
    nonneg = d >= 0
    m0 = nonneg & (d <= WIN_DENSE)
    m1 = nonneg & (d <= WIN_MID) & ((d & 3) == 0)
    m2 = nonneg & (d <= WIN_FAR) & ((d & 15) == 0)
    return m0.astype(F32) + m1.astype(F32) + m2.astype(F32)


def _inproj_kernel(x_ref, g_ref, wm_ref, wg1_ref, wg2_ref, bg_ref, c_ref, su_ref, sd_ref,
                   qa_ref, ka_ref, va_ref, qb_ref, kb_ref, vb_ref, rb_ref, la_ref, *win_refs,
                   tiles_per_seq, first_win_tile):
    h = _rms(x_ref[...], g_ref[...]).astype(BF16)

    def proj(lo, hi):
        return jnp.dot(h, wm_ref[:, lo:hi], preferred_element_type=F32)

    cos = c_ref[...]
    s_up = su_ref[...]
    s_dn = sd_ref[...]

    def rope(y):
        outs = []
        for j in range(D_A // LANES):
            yj = y[:, LANES * j:LANES * (j + 1)]
            outs.append(yj * cos + pltpu.roll(yj, ROT_HALF, 1) * s_up
                        + pltpu.roll(yj, LANES - ROT_HALF, 1) * s_dn)
        return jnp.concatenate(outs, axis=1)

    o = 0
    q = rope(proj(o, o + D_A)); o += D_A
    qa_ref[...] = q * (HEAD_DIM_A ** -0.5)
    k = rope(proj(o, o + D_A)); o += D_A
    ka_ref[...] = k
    v = proj(o, o + D_A); o += D_A
    va_ref[...] = v
    if win_refs:
        kt_ref, vt_ref = win_refs

        @pl.when(pl.program_id(0) % tiles_per_seq >= first_win_tile)
        def _():
            kt_ref[...] = k.T
            vt_ref[...] = v.T
    qb_ref[...] = proj(o, o + DK_B) * (DK_HEAD ** -0.5); o += DK_B
    kb_ref[...] = proj(o, o + DK_B); o += DK_B
    vb_ref[...] = proj(o, o + DV_B); o += DV_B
    rb_ref[...] = proj(o, o + DV_B)
    gate_lr = jnp.dot(h, wg1_ref[...], preferred_element_type=F32)
    z = jnp.dot(gate_lr.astype(BF16), wg2_ref[...], preferred_element_type=F32) + bg_ref[...]
    log_sig = jnp.minimum(z, 0.0) - jnp.log1p(jnp.exp(-jnp.abs(z)))
    la_ref[...] = log_sig * (1.0 / GATE_TAU)


def _inproj(x, g, wm, wg1, wg2, bg, tabs, tm, n_seq=1, win_rows=0):
    n, d = x.shape
    tiles_per_seq = n // n_seq // tm
    first_win_tile = tiles_per_seq - win_rows // tm
    assert win_rows % tm == 0
    t_rows = tabs[0].shape[0]
    n_tab = t_rows // tm
    row = lambda i: (i, 0)
    const = lambda i: (0, 0)
    tab = lambda i: (i % n_tab, 0)
    widths = (D_A, D_A, D_A, DK_B, DK_B, DV_B, DV_B, DK_B)
    out_shape = tuple(jax.ShapeDtypeStruct((n, w), F32) for w in widths)
    out_specs = tuple(pl.BlockSpec((tm, w), row) for w in widths)
    if win_rows:
        win_tile = lambda i: (i // tiles_per_seq, 0, jnp.maximum(i % tiles_per_seq - first_win_tile, 0))
        out_shape += (jax.ShapeDtypeStruct((n_seq, D_A, win_rows), F32),) * 2
        out_specs += (pl.BlockSpec((None, D_A, tm), win_tile),) * 2
    return pl.pallas_call(
        functools.partial(_inproj_kernel, tiles_per_seq=tiles_per_seq, first_win_tile=first_win_tile),
        grid=(n // tm,),
        in_specs=[pl.BlockSpec((tm, d), row), pl.BlockSpec((1, d), const),
                  pl.BlockSpec(wm.shape, const), pl.BlockSpec(wg1.shape, const),
                  pl.BlockSpec(wg2.shape, const), pl.BlockSpec(bg.shape, const),
                  pl.BlockSpec((tm, LANES), tab), pl.BlockSpec((tm, LANES), tab),
                  pl.BlockSpec((tm, LANES), tab)],
        out_specs=out_specs,
        out_shape=out_shape,
        compiler_params=pltpu.CompilerParams(dimension_semantics=("arbitrary",),
                                             vmem_limit_bytes=VMEM_LIMIT),
        name="inproj",
    )(x, g, wm, wg1, wg2, bg, *tabs)


def _attn_prompt_kernel(q_ref, k_ref, v_ref, bias_ref, o_ref, qs, ks, vs, on, ln, *, t):
    blk = BAND
    lane = lax.broadcasted_iota(jnp.int32, (1, LANES), 1)
    head0 = lane < HEAD_DIM_A
    head_sel = (head0, jnp.logical_not(head0))

    for pi, s in enumerate(DILATIONS):
        n_cls = t // s
        for src, dst in ((q_ref, qs), (k_ref, ks), (v_ref, vs)):
            def sort_class(r, c, src=src, dst=dst, pi=pi, s=s, n_cls=n_cls):
                x = src[...] if s == 1 else src[pl.ds(r, n_cls, stride=s), :]
                dst[pi, pl.ds(pl.multiple_of(r * n_cls, n_cls), n_cls), :] = x.astype(BF16)
                return c
            lax.fori_loop(0, s, sort_class, 0)

    def band_block(pi, s, idx):
        n_blk = t // s // blk
        r = idx // n_blk
        j = idx % n_blk
        q0 = pl.multiple_of(idx * blk, blk)
        w0 = pl.multiple_of(q0 - jnp.minimum(j, 1) * blk, blk)
        qb = qs[pi, pl.ds(q0, blk), :]
        kw = ks[pi, pl.ds(w0, 2 * blk), :]
        vw = vs[pi, pl.ds(w0, 2 * blk), :]
        bias = bias_ref[jnp.minimum(j, 1)]
        outs, lses = [], []
        for sel in head_sel:
            qh = jnp.where(sel, qb, jnp.zeros_like(qb))
            sc = lax.dot_general(qh, kw, _NT, preferred_element_type=F32) + bias
            m = jnp.max(sc, axis=1, keepdims=True)
            p = jnp.exp(sc - m)
            l = jnp.sum(p, axis=1, keepdims=True)
            o = jnp.dot(p.astype(BF16), vw, preferred_element_type=F32)
            outs.append(o / l)
            lses.append(m + jnp.log(l))
        rows = pl.ds(q0, blk) if s == 1 else pl.ds(r + s * (j * blk), blk, stride=s)
        on.at[pi][rows, :] = jnp.where(head0, outs[0], outs[1])
        ln.at[pi][rows, :] = jnp.where(head0, lses[0], lses[1])

    for pi, s in enumerate(DILATIONS):
        def one_block(idx, c, pi=pi, s=s):
            band_block(pi, s, idx)
            return c

        lax.fori_loop(0, t // blk, one_block, 0, unroll=ATT_UNROLL)

    def merge(i, c):
        rows = pl.ds(pl.multiple_of(i * MERGE_ROWS, MERGE_ROWS), MERGE_ROWS)
        lse = [ln[pi, rows, :] for pi in range(len(DILATIONS))]
        top = jnp.maximum(jnp.maximum(lse[0], lse[1]), lse[2])
        w = [jnp.exp(x - top) for x in lse]
        num = w[0] * on[0, rows, :] + w[1] * on[1, rows, :] + w[2] * on[2, rows, :]
        o_ref[rows, :] = num / (w[0] + w[1] + w[2])
        return c

    lax.fori_loop(0, t // MERGE_ROWS, merge, 0)


def _band_bias():
    iq = jnp.arange(BAND, dtype=jnp.int32)[:, None]
    c = jnp.arange(2 * BAND, dtype=jnp.int32)[None, :]
    d = jnp.stack([iq - c, iq + BAND - c])
    return jnp.where((d >= 0) & (d <= BAND), 0.0, NEG_BIG).astype(F32)


def _attn_prompt(q, k, v):
    b, t, _ = q.shape
    n_pair = D_A // LANES
    n_pat = len(DILATIONS)
    pair = lambda bi, p: (bi, 0, p)
    bias = _band_bias()
    return pl.pallas_call(
        functools.partial(_attn_prompt_kernel, t=t),
        grid=(b, n_pair),
        in_specs=[pl.BlockSpec((None, t, LANES), pair), pl.BlockSpec((None, t, LANES), pair),
                  pl.BlockSpec((None, t, LANES), pair), pl.BlockSpec(bias.shape, lambda bi, p: (0, 0, 0))],
        out_specs=pl.BlockSpec((None, t, LANES), pair),
        out_shape=jax.ShapeDtypeStruct((b, t, D_A), F32),
        scratch_shapes=[pltpu.VMEM((n_pat, t, LANES), BF16)] * 3 + [pltpu.VMEM((n_pat, t, LANES), F32)] * 2,
        compiler_params=pltpu.CompilerParams(dimension_semantics=("arbitrary", "arbitrary"),
                                             vmem_limit_bytes=VMEM_LIMIT),
        name="attn_prompt",
    )(q, k, v, bias)


def _attn_sample_kernel(q_ref, kt_ref, vt_ref, kn_ref, vn_ref, o_ref, *, win):
    q = q_ref[...].astype(BF16)
    pad = jnp.zeros((LANES - SAMPLE_PAD, D_A), F32)
    kn_t = jnp.concatenate([kn_ref[...], pad], axis=0).T.astype(BF16)
    vn_t = jnp.concatenate([vn_ref[...], pad], axis=0).T.astype(BF16)
    n_keys = win + LANES
    tok = lax.broadcasted_iota(jnp.int32, (SAMPLE_PAD, n_keys), 0)
    key = lax.broadcasted_iota(jnp.int32, (SAMPLE_PAD, n_keys), 1)
    mult = _dilation_multiplicity(win + tok - key)
    live = mult > 0.0
    head_cols = []
    for h in range(N_HEADS_A):
        hs = slice(HEAD_DIM_A * h, HEAD_DIM_A * (h + 1))
        kt = jnp.concatenate([kt_ref[h].astype(BF16), kn_t[hs, :]], axis=1)
        vt = jnp.concatenate([vt_ref[h].astype(BF16), vn_t[hs, :]], axis=1)
        s = jnp.where(live, jnp.dot(q[:, hs], kt, preferred_element_type=F32), NEG_BIG)
        m = jnp.max(s, axis=1, keepdims=True)
        p = jnp.exp(s - m) * mult
        p = p / jnp.sum(p, axis=1, keepdims=True)
        head_cols.append(lax.dot_general(vt, p.astype(BF16), _NT, preferred_element_type=F32))
    o_ref[...] = jnp.concatenate(head_cols, axis=0).T


def _attn_sample(q, kt, vt, kn, vn, *, layer):
    bs = q.shape[0]
    win = kt.shape[-1]
    seq = lambda i: (i, 0, 0)
    slab = lambda i: (layer, i, 0, 0, 0)
    return pl.pallas_call(
        functools.partial(_attn_sample_kernel, win=win),
        grid=(bs,),
        in_specs=[pl.BlockSpec((None, SAMPLE_PAD, D_A), seq),
                  pl.BlockSpec((None, None, N_HEADS_A, HEAD_DIM_A, win), slab),
                  pl.BlockSpec((None, None, N_HEADS_A, HEAD_DIM_A, win), slab),
                  pl.BlockSpec((None, SAMPLE_PAD, D_A), seq), pl.BlockSpec((None, SAMPLE_PAD, D_A), seq)],
        out_specs=pl.BlockSpec((None, SAMPLE_PAD, D_A), seq),
        out_shape=jax.ShapeDtypeStruct((bs, SAMPLE_PAD, D_A), F32),
        compiler_params=pltpu.CompilerParams(dimension_semantics=("arbitrary",),
                                             vmem_limit_bytes=VMEM_LIMIT),
        name="attn_sample",
    )(q, kt, vt, kn, vn)


def _gla_kernel(q_ref, k_ref, v_ref, la_ref, r_ref, g_ref, s0_ref, o_ref, sf_ref, s_scr,
                *, chunk, sub, n_chunks, t_valid):
    @pl.when(pl.program_id(1) == 0)
    def _():
        s_scr[...] = s0_ref[...]

    n_sub = chunk // sub
    row = lax.broadcasted_iota(jnp.int32, (chunk, 1), 0)
    tri = (lax.broadcasted_iota(jnp.int32, (chunk, chunk), 1)
           <= lax.broadcasted_iota(jnp.int32, (chunk, chunk), 0)).astype(F32)
    ones = jnp.ones((chunk, LANES), F32)
    lane = lax.broadcasted_iota(jnp.int32, (1, LANES), 1)
    head_sel = (lane < DK_HEAD, lane >= DK_HEAD)
    causal = (lax.broadcasted_iota(jnp.int32, (chunk, chunk), 1)
              <= lax.broadcasted_iota(jnp.int32, (chunk, chunk), 0))
    g = g_ref[...]
    keep = row < t_valid

    def one_chunk(c, carry):
        rows = pl.ds(pl.multiple_of(c * chunk, chunk), chunk)
        la = la_ref[rows, :]
        if t_valid < chunk:
            la = jnp.where(keep, la, 0.0)
        b_all = jnp.dot(tri, la, precision=lax.Precision.HIGHEST, preferred_element_type=F32)
        for p in range(N_HEADS_B // 2):
            ls = slice(LANES * p, LANES * (p + 1))
            b = b_all[:, ls]
            qp = q_ref[rows, ls]
            kp = k_ref[rows, ls]
            if t_valid < chunk:
                kp = jnp.where(keep, kp, 0.0)
            b_last = b[chunk - 1:chunk, :]
            q_dec = qp * jnp.exp(b)
            k_hat = (kp * jnp.exp(b_last - b)).astype(BF16)
            b_last_col = lax.dot_general(la[:, ls], ones, _TN, precision=lax.Precision.HIGHEST,
                                         preferred_element_type=F32)
            s_old = s_scr[p]
            s_old_bf = s_old.astype(BF16)
            q_sub, k_sub = [], []
            for i in range(n_sub):
                r0, r1 = sub * i, sub * (i + 1)
                beta = b[r0:r0 + 1, :]
                q_sub.append(qp[r0:r1] * jnp.exp(b[r0:r1] - beta))
                k_sub.append((kp * jnp.exp(jnp.where(row < r1, beta - b, 0.0))).astype(BF16))
            for hh in range(2):
                h = 2 * p + hh
                vs = slice(DV_HEAD * h, DV_HEAD * (h + 1))
                vh = v_ref[rows, vs].astype(BF16)
                sel = head_sel[hh]
                o = jnp.dot(jnp.where(sel, q_dec, 0.0).astype(BF16), s_old_bf,
                            preferred_element_type=F32)
                parts = [lax.dot_general(jnp.where(sel, q_sub[i], 0.0).astype(BF16), k_sub[i], _NT,
                                         preferred_element_type=F32) for i in range(n_sub)]
                attn = parts[0] if n_sub == 1 else jnp.concatenate(parts, axis=0)
                attn = jnp.where(causal, attn, 0.0)
                o = o + jnp.dot(attn.astype(BF16), vh, preferred_element_type=F32)
                ds_ = lax.dot_general(k_hat, vh, _TN, preferred_element_type=F32)
                ks = slice(DK_HEAD * hh, DK_HEAD * (hh + 1))
                s_scr[p, ks, :] = jnp.exp(b_last_col[ks]) * s_old[ks] + ds_[ks]
                r = r_ref[rows, vs]
                o_ref[rows, vs] = _rms(o, g) * (r * (1.0 / (1.0 + jnp.exp(-r))))
        return carry

    lax.fori_loop(0, n_chunks, one_chunk, 0)
    sf_ref[...] = s_scr[...]


def _gla(q, k, v, la, r, g, s0, *, n_seq, rows_per_step, chunk, sub, t_valid):
    n = q.shape[0]
    t = n // n_seq
    steps = t // rows_per_step
    row = lambda b, i: (b * steps + i, 0)
    st = lambda b, i: (b, 0, 0, 0)
    n_pair = N_HEADS_B // 2
    kern = functools.partial(_gla_kernel, chunk=chunk, sub=sub, n_chunks=rows_per_step // chunk,
                             t_valid=t_valid)
    return pl.pallas_call(
        kern,
        grid=(n_seq, steps),
        in_specs=[pl.BlockSpec((rows_per_step, DK_B), row), pl.BlockSpec((rows_per_step, DK_B), row),
                  pl.BlockSpec((rows_per_step, DV_B), row), pl.BlockSpec((rows_per_step, DK_B), row),
                  pl.BlockSpec((rows_per_step, DV_B), row), pl.BlockSpec((1, DV_HEAD), lambda b, i: (0, 0)),
                  pl.BlockSpec((None, n_pair, LANES, DV_HEAD), st)],
        out_specs=(pl.BlockSpec((rows_per_step, DV_B), row),
                   pl.BlockSpec((None, n_pair, LANES, DV_HEAD), st)),
        out_shape=(jax.ShapeDtypeStruct((n, DV_B), F32),
                   jax.ShapeDtypeStruct((n_seq, n_pair, LANES, DV_HEAD), F32)),
        scratch_shapes=[pltpu.VMEM((n_pair, LANES, DV_HEAD), F32)],
        compiler_params=pltpu.CompilerParams(dimension_semantics=("arbitrary", "arbitrary"),
                                             vmem_limit_bytes=VMEM_LIMIT),
        name="gla",
    )(q, k, v, la, r, g, s0)


def _outproj_kernel(oa_ref, ob_ref, x_ref, w_ref, g_ref, y_ref):
    m = jnp.dot(oa_ref[...].astype(BF16), w_ref[0:D_A, :], preferred_element_type=F32)
    m = m + jnp.dot(ob_ref[...].astype(BF16), w_ref[D_A:D_A + DV_B, :], preferred_element_type=F32)
    y_ref[...] = x_ref[...] + _rms(m, g_ref[...])


def _outproj(oa, ob, x, w, g, tm):
    n, d = x.shape
    row = lambda i: (i, 0)
    const = lambda i: (0, 0)
    return pl.pallas_call(
        _outproj_kernel,
        grid=(n // tm,),
        in_specs=[pl.BlockSpec((tm, D_A), row), pl.BlockSpec((tm, DV_B), row), pl.BlockSpec((tm, d), row),
                  pl.BlockSpec(w.shape, const), pl.BlockSpec((1, d), const)],
        out_specs=pl.BlockSpec((tm, d), row),
        out_shape=jax.ShapeDtypeStruct((n, d), F32),
        compiler_params=pltpu.CompilerParams(dimension_semantics=("arbitrary",),
                                             vmem_limit_bytes=VMEM_LIMIT),
        name="outproj",
    )(oa, ob, x, w, g)


def _gelu_tanh(x):
    c = math.sqrt(2.0 / math.pi)
    return 0.5 * x * (1.0 + jnp.tanh(c * (x + 0.044715 * (x * x * x))))


def _conv_geglu(u, prev, cw, cb):
    rows, width = u.shape
    tiles = rows // SUBLANES
    u3 = u.reshape(tiles, SUBLANES, width)
    p3 = prev.reshape(tiles, SUBLANES, width)
    r = lax.broadcasted_iota(jnp.int32, (tiles, SUBLANES, width), 1)
    back1 = jnp.where(r >= 1, pltpu.roll(u3, 1, 1), pltpu.roll(p3, 1, 1)).reshape(rows, width)
    back2 = jnp.where(r >= 2, pltpu.roll(u3, 2, 1), pltpu.roll(p3, 2, 1)).reshape(rows, width)
    c = cb + cw[0:1, :] * back2 + cw[1:2, :] * back1 + cw[2:3, :] * u
    return (_gelu_tanh(c[:, :FF_CHUNK]) * c[:, FF_CHUNK:]).astype(BF16)


def _ffn_prompt_kernel(x_ref, g1_ref, wup_ref, cw_ref, cb_ref, wdn_ref, g2_ref, y_ref, tail_ref,
                       carry, *, n_ff, tiles_per_seq):
    tm, d = x_ref.shape

    @pl.when(pl.program_id(0) % tiles_per_seq == 0)
    def _():
        carry[...] = jnp.zeros_like(carry)

    x = x_ref[...]
    h = _rms(x, g1_ref[...]).astype(BF16)
    ys = []
    for j in range(n_ff):
        u = jnp.dot(h, wup_ref[j], preferred_element_type=F32)
        prev = jnp.concatenate([carry[j], u[:tm - SUBLANES]], axis=0)
        ys.append(_conv_geglu(u, prev, cw_ref[j], cb_ref[j]))
        tail = u[tm - SUBLANES:]
        carry[j] = tail
        tail_ref[j] = tail
    f = jnp.dot(jnp.concatenate(ys, axis=1), wdn_ref[...].reshape(n_ff * FF_CHUNK, d),
                preferred_element_type=F32)
    y_ref[...] = x + _rms(f, g2_ref[...])


def _ffn_prompt(x, g1, wup, cw, cb, wdn, g2, *, n_seq, tm):
    n, d = x.shape
    n_ff, _, width = wup.shape
    tiles_per_seq = n // n_seq // tm
    row = lambda i: (i, 0)
    c2 = lambda i: (0, 0)
    c3 = lambda i: (0, 0, 0)
    single = pl.Buffered(1)
    kern = functools.partial(_ffn_prompt_kernel, n_ff=n_ff, tiles_per_seq=tiles_per_seq)
    return pl.pallas_call(
        kern,
        grid=(n // tm,),
        in_specs=[pl.BlockSpec((tm, d), row), pl.BlockSpec((1, d), c2),
                  pl.BlockSpec(wup.shape, c3, pipeline_mode=single), pl.BlockSpec(cw.shape, c3),
                  pl.BlockSpec(cb.shape, c3), pl.BlockSpec(wdn.shape, c3, pipeline_mode=single),
                  pl.BlockSpec((1, d), c2)],
        out_specs=(pl.BlockSpec((tm, d), row),
                   pl.BlockSpec((None, n_ff, SUBLANES, width), lambda i: (i // tiles_per_seq, 0, 0, 0))),
        out_shape=(jax.ShapeDtypeStruct((n, d), F32),
                   jax.ShapeDtypeStruct((n_seq, n_ff, SUBLANES, width), F32)),
        scratch_shapes=[pltpu.VMEM((n_ff, SUBLANES, width), F32)],
        compiler_params=pltpu.CompilerParams(dimension_semantics=("arbitrary",),
                                             vmem_limit_bytes=VMEM_LIMIT),
        name="ffn_prompt",
    )(x, g1, wup, cw, cb, wdn, g2)


def _ffn_sample_kernel(x_ref, g1_ref, wup_ref, cw_ref, cb_ref, wdn_ref, g2_ref, prev_ref, y_ref, u_ref,
                       acc, *, n_ff):
    x = x_ref[...]
    h = _rms(x, g1_ref[...]).astype(BF16)
    acc[...] = jnp.zeros_like(acc)

    def step(j, c):
        u = jnp.dot(h, wup_ref[j], preferred_element_type=F32)
        u_ref[j] = u
        y = _conv_geglu(u, prev_ref[j], cw_ref[j], cb_ref[j])
        acc[...] += jnp.dot(y, wdn_ref[j], preferred_element_type=F32)
        return c

    lax.fori_loop(0, n_ff, step, 0)
    y_ref[...] = x + _rms(acc[...], g2_ref[...])


def _ffn_sample(x, g1, wup, cw, cb, wdn, g2, prev):
    n, d = x.shape
    n_ff, _, width = wup.shape
    full = lambda a: pl.BlockSpec(a.shape, lambda i: (0,) * a.ndim)
    return pl.pallas_call(
        functools.partial(_ffn_sample_kernel, n_ff=n_ff),
        grid=(1,),
        in_specs=[full(x), full(g1), full(wup), full(cw), full(cb), full(wdn), full(g2), full(prev)],
        out_specs=(pl.BlockSpec((n, d), lambda i: (0, 0)),
                   pl.BlockSpec((n_ff, n, width), lambda i: (0, 0, 0))),
        out_shape=(jax.ShapeDtypeStruct((n, d), F32), jax.ShapeDtypeStruct((n_ff, n, width), F32)),
        scratch_shapes=[pltpu.VMEM((n, d), F32)],
        compiler_params=pltpu.CompilerParams(dimension_semantics=("arbitrary",),
                                             vmem_limit_bytes=VMEM_LIMIT),
        name="ffn_sample",
    )(x, g1, wup, cw, cb, wdn, g2, prev)


def _rope_tables(pos):
    inv_freq = ROPE_THETA ** (-jnp.arange(ROT_HALF, dtype=F32) / ROT_HALF)
    ang = pos.astype(F32)[:, None] * inv_freq[None, :]
    cos, sin = jnp.cos(ang), jnp.sin(ang)
    n = pos.shape[0]
    rest = HEAD_DIM_A - 2 * ROT_HALF
    reps = LANES // HEAD_DIM_A
    c = jnp.tile(jnp.concatenate([cos, cos, jnp.ones((n, rest), F32)], axis=1), (1, reps))
    s_up = jnp.tile(jnp.concatenate([jnp.zeros((n, ROT_HALF), F32), sin, jnp.zeros((n, rest), F32)], axis=1),
                    (1, reps))
    s_dn = jnp.tile(jnp.concatenate([-sin, jnp.zeros((n, ROT_HALF + rest), F32)], axis=1), (1, reps))
    return c, s_up, s_dn


def _ff_chunks(a, d_ff):
    n_ff = d_ff // FF_CHUNK
    gate = a[..., :d_ff].reshape(a.shape[:-1] + (n_ff, FF_CHUNK))
    up = a[..., d_ff:].reshape(a.shape[:-1] + (n_ff, FF_CHUNK))
    return jnp.moveaxis(jnp.concatenate([gate, up], axis=-1), -2, 0)


def _ff_unchunks(a, d_ff):
    a = jnp.moveaxis(a, 0, -2)
    gate = a[..., :FF_CHUNK].reshape(a.shape[:-2] + (d_ff,))
    up = a[..., FF_CHUNK:].reshape(a.shape[:-2] + (d_ff,))
    return jnp.concatenate([gate, up], axis=-1)


def kernel(x_prompt, x_sample, cache_k_win, cache_v_win, state_gla, state_ffn_conv, g_mix_pre, g_mix_post,
           g_ffn_pre, g_ffn_post, w_in, w_gate2, b_gate, g_gla, w_out, w_up, conv_w, conv_b, w_down):
    depth = w_in.shape[0]
    b, t, d = x_prompt.shape
    bs, ts, _ = x_sample.shape
    win = cache_k_win.shape[2]
    d_ff = w_down.shape[1]
    n_ff = d_ff // FF_CHUNK
    conv_taps = conv_w.shape[1]
    n_main = 3 * D_A + 2 * DK_B + 2 * DV_B
    n_pair = N_HEADS_B // 2

    tabs_p = _rope_tables(jnp.arange(t, dtype=jnp.int32))
    pos_s = PAST_LEN + jnp.arange(SAMPLE_PAD, dtype=jnp.int32)
    tabs_s = tuple(jnp.tile(a, (bs, 1)) for a in _rope_tables(pos_s))

    xp = x_prompt.reshape(b * t, d)
    xs = jnp.pad(x_sample, ((0, 0), (0, SAMPLE_PAD - ts), (0, 0))).reshape(bs * SAMPLE_PAD, d)

    cache_kt = jnp.transpose(cache_k_win, (0, 1, 3, 4, 2))
    cache_vt = jnp.transpose(cache_v_win, (0, 1, 3, 4, 2))
    win_p = min(win, t)

    outs_p = {"k": [], "v": [], "s": [], "c": []}
    outs_s = {"k": [], "v": [], "s": [], "c": []}
    for l in range(depth):
        wm = w_in[l][:, :n_main].astype(BF16)
        wg1 = jnp.pad(w_in[l][:, n_main:], ((0, 0), (0, LANES - GATE_RANK))).astype(BF16)
        wg2 = jnp.pad(w_gate2[l], ((0, LANES - GATE_RANK), (0, 0))).astype(BF16)
        bg = b_gate[l][None, :]
        wo = w_out[l].astype(BF16)
        wup = _ff_chunks(w_up[l], d_ff).astype(BF16)
        cw = jnp.pad(_ff_chunks(conv_w[l], d_ff), ((0, 0), (0, SUBLANES - conv_taps), (0, 0)))
        cb = _ff_chunks(conv_b[l][None, :], d_ff)
        wdn = w_down[l].reshape(n_ff, FF_CHUNK, d).astype(BF16)
        g_pre, g_post = g_mix_pre[l][None, :], g_mix_post[l][None, :]
        g_f1, g_f2 = g_ffn_pre[l][None, :], g_ffn_post[l][None, :]
        gg = g_gla[l][None, :]

        qa, ka, va, qb, kb, vb, rb, la, kt_win, vt_win = _inproj(xp, g_pre, wm, wg1, wg2, bg, tabs_p, tm=256,
                                                                 n_seq=b, win_rows=win_p)
        oa = _attn_prompt(qa.reshape(b, t, D_A), ka.reshape(b, t, D_A), va.reshape(b, t, D_A))
        s0 = jnp.zeros((b, n_pair, LANES, DV_HEAD), F32)
        ob, s_fin = _gla(qb, kb, vb, la, rb, gg, s0, n_seq=b, rows_per_step=512, chunk=GLA_CHUNK,
                         sub=GLA_SUB, t_valid=GLA_CHUNK)
        x1 = _outproj(oa.reshape(b * t, D_A), ob, xp, wo, g_post, tm=512)
        xp, tail = _ffn_prompt(x1, g_f1, wup, cw, cb, wdn, g_f2, n_seq=b, tm=512)
        to_rows = lambda a: jnp.transpose(a.reshape(b, N_HEADS_A, HEAD_DIM_A, win_p), (0, 3, 1, 2))
        outs_p["k"].append(to_rows(kt_win))
        outs_p["v"].append(to_rows(vt_win))
        outs_p["s"].append(s_fin.reshape(b, N_HEADS_B, DK_HEAD, DV_HEAD))
        outs_p["c"].append(_ff_unchunks(jnp.moveaxis(tail, 1, 0), d_ff)[:, SUBLANES - (conv_taps - 1):])

        qa, ka, va, qb, kb, vb, rb, la = _inproj(xs, g_pre, wm, wg1, wg2, bg, tabs_s, tm=bs * SAMPLE_PAD)
        ka3 = ka.reshape(bs, SAMPLE_PAD, D_A)
        va3 = va.reshape(bs, SAMPLE_PAD, D_A)
        oa = _attn_sample(qa.reshape(bs, SAMPLE_PAD, D_A), cache_kt, cache_vt, ka3, va3, layer=l)
        s0 = state_gla[l].reshape(bs, n_pair, LANES, DV_HEAD)
        ob, s_fin = _gla(qb, kb, vb, la, rb, gg, s0, n_seq=bs, rows_per_step=SAMPLE_PAD, chunk=SAMPLE_PAD,
                         sub=SAMPLE_PAD, t_valid=ts)
        x1 = _outproj(oa.reshape(bs * SAMPLE_PAD, D_A), ob, xs, wo, g_post, tm=bs * SAMPLE_PAD)
        prev = jnp.pad(state_ffn_conv[l], ((0, 0), (SUBLANES - (conv_taps - 1), 0), (0, 0)))
        prev = _ff_chunks(prev.reshape(bs * SUBLANES, 2 * d_ff), d_ff)
        xs, u_all = _ffn_sample(x1, g_f1, wup, cw, cb, wdn, g_f2, prev)
        u_all = _ff_unchunks(u_all, d_ff).reshape(bs, SAMPLE_PAD, 2 * d_ff)
        outs_s["k"].append(ka3[:, :ts].reshape(bs, ts, N_HEADS_A, HEAD_DIM_A))
        outs_s["v"].append(va3[:, :ts].reshape(bs, ts, N_HEADS_A, HEAD_DIM_A))
        outs_s["s"].append(s_fin.reshape(bs, N_HEADS_B, DK_HEAD, DV_HEAD))
        outs_s["c"].append(u_all[:, ts - (conv_taps - 1):ts])

    y_prompt = xp.reshape(b, t, d)
    y_sample = xs.reshape(bs, SAMPLE_PAD, d)[:, :ts]
    st = lambda xs_: jnp.stack(xs_)
    return (y_prompt, y_sample, st(outs_p["k"]), st(outs_p["v"]), st(outs_p["s"]), st(outs_p["c"]),
            st(outs_s["k"]), st(outs_s["v"]), st(outs_s["s"]), st(outs_s["c"]))
```

```python
import functools
import math

import jax
import jax.numpy as jnp
from jax import lax
from jax.experimental import pallas as pl
from jax.experimental.pallas import tpu as pltpu

F32 = jnp.float32
BF16 = jnp.bfloat16

PAST_LEN = 16384
N_HEADS_A = 8
HEAD_DIM_A = 64
D_A = N_HEADS_A * HEAD_DIM_A
ROT_HALF = HEAD_DIM_A // 8
ROPE_THETA = 500000.0
WIN_DENSE, WIN_MID, WIN_FAR = 128, 512, 2048
DILATIONS = (1, 4, 16)
BAND = 128
MERGE_ROWS = 256
ATT_UNROLL = 8
N_HEADS_B = 4
DK_HEAD = 64
DV_HEAD = 128
DK_B = N_HEADS_B * DK_HEAD
DV_B = N_HEADS_B * DV_HEAD
GATE_RANK = 16
GATE_TAU = 16.0
GLA_CHUNK = 64
GLA_SUB = 16
EPS = 1e-6
LANES = 128
SUBLANES = 8
NEG_BIG = -1e30
FF_CHUNK = 256
SAMPLE_PAD = 8
VMEM_LIMIT = 56 * 1024 * 1024

_NT = (((1,), (1,)), ((), ()))
_TN = (((0,), (0,)), ((), ()))


def _rms(x, g):
    ms = jnp.mean(x * x, axis=-1, keepdims=True)
    return x * lax.rsqrt(ms + EPS) * g


def _split3(x):
    hi = x.astype(BF16)
    r1 = x - hi.astype(F32)
    mid = r1.astype(BF16)
    lo = (r1 - mid.astype(F32)).astype(BF16)
    return hi, mid, lo


def _dilation_multiplicity(d):
    nonneg = d >= 0
    m0 = nonneg & (d <= WIN_DENSE)
    m1 = nonneg & (d <= WIN_MID) & ((d & 3) == 0)
    m2 = nonneg & (d <= WIN_FAR) & ((d & 15) == 0)
    return m0.astype(F32) + m1.astype(F32) + m2.astype(F32)


def _inproj_kernel(x_ref, g_ref, wm_ref, wg1_ref, wg2_ref, bg_ref, c_ref, su_ref, sd_ref,
                   qa_ref, ka_ref, va_ref, qb_ref, kb_ref, vb_ref, rb_ref, la_ref, *win_refs,
                   tiles_per_seq, first_win_tile, gla_chunk, gla_valid):
    h = _rms(x_ref[...], g_ref[...]).astype(BF16)

    def proj(lo, hi):
        return jnp.dot(h, wm_ref[:, lo:hi], preferred_element_type=F32)

    cos = c_ref[...]
    s_up = su_ref[...]
    s_dn = sd_ref[...]

    def rope(y):
        outs = []
        for j in range(D_A // LANES):
            yj = y[:, LANES * j:LANES * (j + 1)]
            outs.append(yj * cos + pltpu.roll(yj, ROT_HALF, 1) * s_up
                        + pltpu.roll(yj, LANES - ROT_HALF, 1) * s_dn)
        return jnp.concatenate(outs, axis=1)

    o = 0
    q = rope(proj(o, o + D_A)); o += D_A
    qa_ref[...] = q * (HEAD_DIM_A ** -0.5)
    ka_ref[...] = rope(proj(o, o + D_A)); o += D_A
    va_ref[...] = proj(o, o + D_A); o += D_A
    qb_ref[...] = proj(o, o + DK_B) * (DK_HEAD ** -0.5); o += DK_B
    kb_ref[...] = proj(o, o + DK_B); o += DK_B
    vb_ref[...] = proj(o, o + DV_B); o += DV_B
    rb_ref[...] = proj(o, o + DV_B)
    gate_lr = jnp.dot(h, wg1_ref[...], preferred_element_type=F32)
    z = jnp.dot(gate_lr.astype(BF16), wg2_ref[...], preferred_element_type=F32) + bg_ref[...]
    log_sig = jnp.minimum(z, 0.0) - jnp.log1p(jnp.exp(-jnp.abs(z)))
    log_a = log_sig * (1.0 / GATE_TAU)
    tm = log_a.shape[0]
    r = lax.broadcasted_iota(jnp.int32, (tm, tm), 0)
    c = lax.broadcasted_iota(jnp.int32, (tm, tm), 1)
    tri = ((c <= r) & (c // gla_chunk == r // gla_chunk) & (c % gla_chunk < gla_valid)).astype(BF16)
    la_ref[...] = sum(jnp.dot(tri, piece, preferred_element_type=F32) for piece in _split3(log_a))
    if win_refs:
        kt_ref, vt_ref = win_refs

        @pl.when(pl.program_id(0) % tiles_per_seq >= first_win_tile)
        def _():
            kt_ref[...] = ka_ref[...].T
            vt_ref[...] = va_ref[...].T


def _inproj(x, g, wm, wg1, wg2, bg, tabs, tm, gla_chunk, gla_valid, n_seq=1, win_rows=0):
    n, d = x.shape
    tiles_per_seq = n // n_seq // tm
    first_win_tile = tiles_per_seq - win_rows // tm
    assert win_rows % tm == 0
    t_rows = tabs[0].shape[0]
    n_tab = t_rows // tm
    row = lambda i: (i, 0)
    const = lambda i: (0, 0)
    tab = lambda i: (i % n_tab, 0)
    widths = (D_A, D_A, D_A, DK_B, DK_B, DV_B, DV_B, DK_B)
    out_shape = tuple(jax.ShapeDtypeStruct((n, w), F32) for w in widths)
    out_specs = tuple(pl.BlockSpec((tm, w), row) for w in widths)
    if win_rows:
        win_tile = lambda i: (i // tiles_per_seq, 0, jnp.maximum(i % tiles_per_seq - first_win_tile, 0))
        out_shape += (jax.ShapeDtypeStruct((n_seq, D_A, win_rows), F32),) * 2
        out_specs += (pl.BlockSpec((None, D_A, tm), win_tile),) * 2
    return pl.pallas_call(
        functools.partial(_inproj_kernel, tiles_per_seq=tiles_per_seq, first_win_tile=first_win_tile,
                          gla_chunk=gla_chunk, gla_valid=gla_valid),
        grid=(n // tm,),
        in_specs=[pl.BlockSpec((tm, d), row), pl.BlockSpec((1, d), const),
                  pl.BlockSpec(wm.shape, const), pl.BlockSpec(wg1.shape, const),
                  pl.BlockSpec(wg2.shape, const), pl.BlockSpec(bg.shape, const),
                  pl.BlockSpec((tm, LANES), tab), pl.BlockSpec((tm, LANES), tab),
                  pl.BlockSpec((tm, LANES), tab)],
        out_specs=out_specs,
        out_shape=out_shape,
        compiler_params=pltpu.CompilerParams(dimension_semantics=("arbitrary",),
                                             vmem_limit_bytes=VMEM_LIMIT),
        name="inproj",
    )(x, g, wm, wg1, wg2, bg, *tabs)


def _attn_prompt_kernel(q_ref, k_ref, v_ref, bias_ref, o_ref, qs, ks, vs, on, ln, *, t):
    blk = BAND
    lane = lax.broadcasted_iota(jnp.int32, (1, LANES), 1)
    head0 = lane < HEAD_DIM_A
    head_sel = (head0, jnp.logical_not(head0))

    for pi, s in enumerate(DILATIONS):
        n_cls = t // s
        for src, dst in ((q_ref, qs), (k_ref, ks), (v_ref, vs)):
            def sort_class(r, c, src=src, dst=dst, pi=pi, s=s, n_cls=n_cls):
                x = src[...] if s == 1 else src[pl.ds(r, n_cls, stride=s), :]
                dst[pi, pl.ds(pl.multiple_of(r * n_cls, n_cls), n_cls), :] = x.astype(BF16)
                return c
            lax.fori_loop(0, s, sort_class, 0)

    def band_blocks(pi, s, first_idx):
        n_blk = t // s // blk
        blocks = []
        for u in range(ATT_UNROLL):
            idx = first_idx + u
            r = idx // n_blk
            j = idx % n_blk
            q0 = pl.multiple_of(idx * blk, blk)
            w0 = pl.multiple_of(q0 - jnp.minimum(j, 1) * blk, blk)
            rows = pl.ds(q0, blk) if s == 1 else pl.ds(r + s * (j * blk), blk, stride=s)
            blocks.append((qs[pi, pl.ds(q0, blk), :], ks[pi, pl.ds(w0, 2 * blk), :],
                           vs[pi, pl.ds(w0, 2 * blk), :], bias_ref[jnp.minimum(j, 1)], rows))
        heads = [(u, sel) for u in range(ATT_UNROLL) for sel in head_sel]
        sc = [lax.dot_general(jnp.where(sel, blocks[u][0], jnp.zeros_like(blocks[u][0])), blocks[u][1], _NT,
                              preferred_element_type=F32) + blocks[u][3] for u, sel in heads]
        m = [jnp.max(x, axis=1, keepdims=True) for x in sc]
        p = [jnp.exp(x - mx) for x, mx in zip(sc, m)]
        l = [jnp.sum(x, axis=1, keepdims=True) for x in p]
        o = [jnp.dot(x.astype(BF16), blocks[u][2], preferred_element_type=F32) for x, (u, _) in zip(p, heads)]
        for u in range(ATT_UNROLL):
            rows = blocks[u][4]
            o0, o1 = o[2 * u] / l[2 * u], o[2 * u + 1] / l[2 * u + 1]
            lse0, lse1 = m[2 * u] + jnp.log(l[2 * u]), m[2 * u + 1] + jnp.log(l[2 * u + 1])
            on.at[pi][rows, :] = jnp.where(head0, o0, o1)
            ln.at[pi][rows, :] = jnp.where(head0, lse0, lse1)

    for pi, s in enumerate(DILATIONS):
        def one_group(gi, c, pi=pi, s=s):
            band_blocks(pi, s, gi * ATT_UNROLL)
            return c

        lax.fori_loop(0, t // blk // ATT_UNROLL, one_group, 0)

    def merge(i, c):
        rows = pl.ds(pl.multiple_of(i * MERGE_ROWS, MERGE_ROWS), MERGE_ROWS)
        lse = [ln[pi, rows, :] for pi in range(len(DILATIONS))]
        top = jnp.maximum(jnp.maximum(lse[0], lse[1]), lse[2])
        w = [jnp.exp(x - top) for x in lse]
        num = w[0] * on[0, rows, :] + w[1] * on[1, rows, :] + w[2] * on[2, rows, :]
        o_ref[rows, :] = num / (w[0] + w[1] + w[2])
        return c

    lax.fori_loop(0, t // MERGE_ROWS, merge, 0)


def _band_bias():
    iq = jnp.arange(BAND, dtype=jnp.int32)[:, None]
    c = jnp.arange(2 * BAND, dtype=jnp.int32)[None, :]
    d = jnp.stack([iq - c, iq + BAND - c])
    return jnp.where((d >= 0) & (d <= BAND), 0.0, NEG_BIG).astype(F32)


def _attn_prompt(q, k, v):
    b, t, _ = q.shape
    n_pair = D_A // LANES
    n_pat = len(DILATIONS)
    pair = lambda bi, p: (bi, 0, p)
    bias = _band_bias()
    return pl.pallas_call(
        functools.partial(_attn_prompt_kernel, t=t),
        grid=(b, n_pair),
        in_specs=[pl.BlockSpec((None, t, LANES), pair), pl.BlockSpec((None, t, LANES), pair),
                  pl.BlockSpec((None, t, LANES), pair), pl.BlockSpec(bias.shape, lambda bi, p: (0, 0, 0))],
        out_specs=pl.BlockSpec((None, t, LANES), pair),
        out_shape=jax.ShapeDtypeStruct((b, t, D_A), F32),
        scratch_shapes=[pltpu.VMEM((n_pat, t, LANES), BF16)] * 3 + [pltpu.VMEM((n_pat, t, LANES), F32)] * 2,
        compiler_params=pltpu.CompilerParams(dimension_semantics=("arbitrary", "arbitrary"),
                                             vmem_limit_bytes=VMEM_LIMIT),
        name="attn_prompt",
    )(q, k, v, bias)


def _attn_sample_kernel(q_ref, kt_ref, vt_ref, kn_ref, vn_ref, o_ref, *, win):
    q = q_ref[...].astype(BF16)
    pad = jnp.zeros((LANES - SAMPLE_PAD, D_A), F32)
    kn_t = jnp.concatenate([kn_ref[...], pad], axis=0).T.astype(BF16)
    vn_t = jnp.concatenate([vn_ref[...], pad], axis=0).T.astype(BF16)
    n_keys = win + LANES
    tok = lax.broadcasted_iota(jnp.int32, (SAMPLE_PAD, n_keys), 0)
    key = lax.broadcasted_iota(jnp.int32, (SAMPLE_PAD, n_keys), 1)
    mult = _dilation_multiplicity(win + tok - key)
    live = mult > 0.0
    hs = [slice(HEAD_DIM_A * h, HEAD_DIM_A * (h + 1)) for h in range(N_HEADS_A)]
    kts = [jnp.concatenate([kt_ref[h].astype(BF16), kn_t[hs[h], :]], axis=1) for h in range(N_HEADS_A)]
    ss = [jnp.where(live, jnp.dot(q[:, hs[h]], kts[h], preferred_element_type=F32), NEG_BIG)
          for h in range(N_HEADS_A)]
    ps = []
    for s in ss:
        m = jnp.max(s, axis=1, keepdims=True)
        p = jnp.exp(s - m) * mult
        ps.append((p / jnp.sum(p, axis=1, keepdims=True)).astype(BF16))
    vts = [jnp.concatenate([vt_ref[h].astype(BF16), vn_t[hs[h], :]], axis=1) for h in range(N_HEADS_A)]
    head_cols = [lax.dot_general(vts[h], ps[h], _NT, preferred_element_type=F32) for h in range(N_HEADS_A)]
    o_ref[...] = jnp.concatenate(head_cols, axis=0).T


def _attn_sample(q, kt, vt, kn, vn, *, layer):
    bs = q.shape[0]
    win = kt.shape[-1]
    seq = lambda i: (i, 0, 0)
    slab = lambda i: (layer, i, 0, 0, 0)
    return pl.pallas_call(
        functools.partial(_attn_sample_kernel, win=win),
        grid=(bs,),
        in_specs=[pl.BlockSpec((None, SAMPLE_PAD, D_A), seq),
                  pl.BlockSpec((None, None, N_HEADS_A, HEAD_DIM_A, win), slab),
                  pl.BlockSpec((None, None, N_HEADS_A, HEAD_DIM_A, win), slab),
                  pl.BlockSpec((None, SAMPLE_PAD, D_A), seq), pl.BlockSpec((None, SAMPLE_PAD, D_A), seq)],
        out_specs=pl.BlockSpec((None, SAMPLE_PAD, D_A), seq),
        out_shape=jax.ShapeDtypeStruct((bs, SAMPLE_PAD, D_A), F32),
        compiler_params=pltpu.CompilerParams(dimension_semantics=("arbitrary",),
                                             vmem_limit_bytes=VMEM_LIMIT),
        name="attn_sample",
    )(q, kt, vt, kn, vn)


def _gla_kernel(q_ref, k_ref, v_ref, b_ref, r_ref, g_ref, s0_ref, o_ref, sf_ref, s_scr,
                *, chunk, sub, n_chunks, t_valid):
    n_pair = N_HEADS_B // 2
    pair_v = 2 * DV_HEAD

    @pl.when(pl.program_id(1) == 0)
    def _():
        zero = jnp.zeros((DK_HEAD, DV_HEAD), F32)
        for p in range(n_pair):
            s0 = s0_ref[p]
            s_scr[p] = jnp.concatenate([jnp.concatenate([s0[:DK_HEAD], zero], axis=1),
                                        jnp.concatenate([zero, s0[DK_HEAD:]], axis=1)], axis=0)

    n_sub = chunk // sub
    row = lax.broadcasted_iota(jnp.int32, (chunk, 1), 0)
    lane = lax.broadcasted_iota(jnp.int32, (1, LANES), 1)
    head0 = lane < DK_HEAD
    causal = (lax.broadcasted_iota(jnp.int32, (chunk, 2 * chunk), 1) % chunk
              <= lax.broadcasted_iota(jnp.int32, (chunk, 2 * chunk), 0))
    own_block = (lax.broadcasted_iota(jnp.int32, (LANES, pair_v), 0) // DK_HEAD
                 == lax.broadcasted_iota(jnp.int32, (LANES, pair_v), 1) // DV_HEAD)
    g = g_ref[...]
    keep = row < t_valid
    zero_v = jnp.zeros((chunk, DV_HEAD), BF16)

    items = [(c, p) for c in range(n_chunks) for p in range(n_pair)]
    rows_of = lambda c: slice(c * chunk, (c + 1) * chunk)
    b_, q_, k_, v_ = {}, {}, {}, {}
    for c, p in items:
        ls = slice(LANES * p, LANES * (p + 1))
        b_[c, p] = b_ref[rows_of(c), ls]
        q_[c, p] = q_ref[rows_of(c), ls]
        kp = k_ref[rows_of(c), ls]
        k_[c, p] = jnp.where(keep, kp, 0.0) if t_valid < chunk else kp
        v_[c, p] = v_ref[rows_of(c), pair_v * p:pair_v * (p + 1)].astype(BF16)
    b_last = {it: b_[it][chunk - 1:chunk, :] for it in items}
    q_dec = {it: (q_[it] * jnp.exp(b_[it])).astype(BF16) for it in items}
    k_hat = {it: (k_[it] * jnp.exp(b_last[it] - b_[it])).astype(BF16) for it in items}
    q_sub, k_sub = {}, {}
    for it in items:
        b = b_[it]
        for i in range(n_sub):
            r0, r1 = sub * i, sub * (i + 1)
            beta = b[r0:r0 + 1, :]
            q_sub[it, i] = (q_[it][r0:r1] * jnp.exp(b[r0:r1] - beta)).astype(BF16)
            k_i = (k_[it] * jnp.exp(jnp.where(row < r1, beta - b, 0.0))).astype(BF16)
            k_sub[it, i] = jnp.concatenate([jnp.where(head0, k_i, jnp.zeros_like(k_i)),
                                            jnp.where(head0, jnp.zeros_like(k_i), k_i)], axis=0)
    parts = {key: lax.dot_general(q_sub[key], k_sub[key], _NT, preferred_element_type=F32) for key in q_sub}
    attn = {}
    for it in items:
        a = parts[it, 0] if n_sub == 1 else jnp.concatenate([parts[it, i] for i in range(n_sub)], axis=0)
        attn[it] = jnp.where(causal, a, 0.0).astype(BF16)
    v_bd = {it: jnp.concatenate([jnp.concatenate([v_[it][:, :DV_HEAD], zero_v], axis=1),
                                 jnp.concatenate([zero_v, v_[it][:, DV_HEAD:]], axis=1)], axis=0) for it in items}
    o_intra = {it: jnp.dot(attn[it], v_bd[it], preferred_element_type=F32) for it in items}
    ds_ = {it: lax.dot_general(k_hat[it], v_[it], _TN, preferred_element_type=F32) for it in items}
    decay_col = {it: jnp.exp(jnp.broadcast_to(b_last[it], (SUBLANES, LANES)).T[:, 0:1]) for it in items}
    state = [s_scr[p] for p in range(n_pair)]
    for c, p in items:
        o_pair = o_intra[c, p] + jnp.dot(q_dec[c, p], state[p].astype(BF16), preferred_element_type=F32)
        state[p] = jnp.where(own_block, decay_col[c, p] * state[p] + ds_[c, p], 0.0)
        for hh in range(2):
            vs = slice(DV_HEAD * (2 * p + hh), DV_HEAD * (2 * p + hh + 1))
            r = r_ref[rows_of(c), vs]
            o = o_pair[:, DV_HEAD * hh:DV_HEAD * (hh + 1)]
            o_ref[rows_of(c), vs] = _rms(o, g) * (r * (1.0 / (1.0 + jnp.exp(-r))))
    for p in range(n_pair):
        s_scr[p] = state[p]
    for p in range(n_pair):
        s = s_scr[p]
        sf_ref[p] = jnp.concatenate([s[:DK_HEAD, :DV_HEAD], s[DK_HEAD:, DV_HEAD:]], axis=0)


def _gla(q, k, v, la, r, g, s0, *, n_seq, rows_per_step, chunk, sub, t_valid):
    n = q.shape[0]
    t = n // n_seq
    steps = t // rows_per_step
    row = lambda b, i: (b * steps + i, 0)
    st = lambda b, i: (b, 0, 0, 0)
    n_pair = N_HEADS_B // 2
    kern = functools.partial(_gla_kernel, chunk=chunk, sub=sub, n_chunks=rows_per_step // chunk,
                             t_valid=t_valid)
    return pl.pallas_call(
        kern,
        grid=(n_seq, steps),
        in_specs=[pl.BlockSpec((rows_per_step, DK_B), row), pl.BlockSpec((rows_per_step, DK_B), row),
                  pl.BlockSpec((rows_per_step, DV_B), row), pl.BlockSpec((rows_per_step, DK_B), row),
                  pl.BlockSpec((rows_per_step, DV_B), row), pl.BlockSpec((1, DV_HEAD), lambda b, i: (0, 0)),
                  pl.BlockSpec((None, n_pair, LANES, DV_HEAD), st)],
        out_specs=(pl.BlockSpec((rows_per_step, DV_B), row),
                   pl.BlockSpec((None, n_pair, LANES, DV_HEAD), st)),
        out_shape=(jax.ShapeDtypeStruct((n, DV_B), F32),
                   jax.ShapeDtypeStruct((n_seq, n_pair, LANES, DV_HEAD), F32)),
        scratch_shapes=[pltpu.VMEM((n_pair, LANES, 2 * DV_HEAD), F32)],
        compiler_params=pltpu.CompilerParams(dimension_semantics=("arbitrary", "arbitrary"),
                                             vmem_limit_bytes=VMEM_LIMIT),
        name="gla",
    )(q, k, v, la, r, g, s0)


def _outproj_kernel(oa_ref, ob_ref, x_ref, w_ref, g_ref, y_ref):
    m = jnp.dot(oa_ref[...].astype(BF16), w_ref[0:D_A, :], preferred_element_type=F32)
    m = m + jnp.dot(ob_ref[...].astype(BF16), w_ref[D_A:D_A + DV_B, :], preferred_element_type=F32)
    y_ref[...] = x_ref[...] + _rms(m, g_ref[...])


def _outproj(oa, ob, x, w, g, tm):
    n, d = x.shape
    row = lambda i: (i, 0)
    const = lambda i: (0, 0)
    return pl.pallas_call(
        _outproj_kernel,
        grid=(n // tm,),
        in_specs=[pl.BlockSpec((tm, D_A), row), pl.BlockSpec((tm, DV_B), row), pl.BlockSpec((tm, d), row),
                  pl.BlockSpec(w.shape, const), pl.BlockSpec((1, d), const)],
        out_specs=pl.BlockSpec((tm, d), row),
        out_shape=jax.ShapeDtypeStruct((n, d), F32),
        compiler_params=pltpu.CompilerParams(dimension_semantics=("arbitrary",),
                                             vmem_limit_bytes=VMEM_LIMIT),
        name="outproj",
    )(oa, ob, x, w, g)


def _gelu_tanh(x):
    c = math.sqrt(2.0 / math.pi)
    return 0.5 * x * (1.0 + jnp.tanh(c * (x + 0.044715 * (x * x * x))))


def _conv_geglu(u, prev, cw, cb):
    rows, width = u.shape
    tiles = rows // SUBLANES
    u3 = u.reshape(tiles, SUBLANES, width)
    p3 = prev.reshape(tiles, SUBLANES, width)
    r = lax.broadcasted_iota(jnp.int32, (tiles, SUBLANES, width), 1)
    back1 = pltpu.roll(jnp.where(r >= SUBLANES - 1, p3, u3), 1, 1).reshape(rows, width)
    back2 = pltpu.roll(jnp.where(r >= SUBLANES - 2, p3, u3), 2, 1).reshape(rows, width)
    c = cb + cw[0:1, :] * back2 + cw[1:2, :] * back1 + cw[2:3, :] * u
    return (_gelu_tanh(c[:, :FF_CHUNK]) * c[:, FF_CHUNK:]).astype(BF16)


def _ffn_prompt_kernel(x_ref, g1_ref, wup_ref, cw_ref, cb_ref, wdn_ref, g2_ref, y_ref, tail_ref,
                       carry, *, n_ff, tiles_per_seq):
    tm, d = x_ref.shape

    @pl.when(pl.program_id(0) % tiles_per_seq == 0)
    def _():
        carry[...] = jnp.zeros_like(carry)

    x = x_ref[...]
    h = _rms(x, g1_ref[...]).astype(BF16)
    ys = []
    for j in range(n_ff):
        u = jnp.dot(h, wup_ref[j], preferred_element_type=F32)
        prev = jnp.concatenate([carry[j], u[:tm - SUBLANES]], axis=0)
        ys.append(_conv_geglu(u, prev, cw_ref[j], cb_ref[j]))
        tail = u[tm - SUBLANES:]
        carry[j] = tail
        tail_ref[j] = tail
    f = jnp.dot(jnp.concatenate(ys, axis=1), wdn_ref[...].reshape(n_ff * FF_CHUNK, d),
                preferred_element_type=F32)
    y_ref[...] = x + _rms(f, g2_ref[...])


def _ffn_prompt(x, g1, wup, cw, cb, wdn, g2, *, n_seq, tm):
    n, d = x.shape
    n_ff, _, width = wup.shape
    tiles_per_seq = n // n_seq // tm
    row = lambda i: (i, 0)
    c2 = lambda i: (0, 0)
    c3 = lambda i: (0, 0, 0)
    single = pl.Buffered(1)
    kern = functools.partial(_ffn_prompt_kernel, n_ff=n_ff, tiles_per_seq=tiles_per_seq)
    return pl.pallas_call(
        kern,
        grid=(n // tm,),
        in_specs=[pl.BlockSpec((tm, d), row), pl.BlockSpec((1, d), c2),
                  pl.BlockSpec(wup.shape, c3, pipeline_mode=single), pl.BlockSpec(cw.shape, c3),
                  pl.BlockSpec(cb.shape, c3), pl.BlockSpec(wdn.shape, c3, pipeline_mode=single),
                  pl.BlockSpec((1, d), c2)],
        out_specs=(pl.BlockSpec((tm, d), row),
                   pl.BlockSpec((None, n_ff, SUBLANES, width), lambda i: (i // tiles_per_seq, 0, 0, 0))),
        out_shape=(jax.ShapeDtypeStruct((n, d), F32),
                   jax.ShapeDtypeStruct((n_seq, n_ff, SUBLANES, width), F32)),
        scratch_shapes=[pltpu.VMEM((n_ff, SUBLANES, width), F32)],
        compiler_params=pltpu.CompilerParams(dimension_semantics=("arbitrary",),
                                             vmem_limit_bytes=VMEM_LIMIT),
        name="ffn_prompt",
    )(x, g1, wup, cw, cb, wdn, g2)


def _ffn_sample_kernel(x_ref, g1_ref, wup_ref, cw_ref, cb_ref, wdn_ref, g2_ref, prev_ref, y_ref, u_ref,
                       acc, *, n_ff):
    x = x_ref[...]
    h = _rms(x, g1_ref[...]).astype(BF16)
    acc[...] = jnp.zeros_like(acc)

    def step(j, c):
        u = jnp.dot(h, wup_ref[j], preferred_element_type=F32)
        u_ref[j] = u
        y = _conv_geglu(u, prev_ref[j], cw_ref[j], cb_ref[j])
        acc[...] += jnp.dot(y, wdn_ref[j], preferred_element_type=F32)
        return c

    lax.fori_loop(0, n_ff, step, 0)
    y_ref[...] = x + _rms(acc[...], g2_ref[...])


def _ffn_sample(x, g1, wup, cw, cb, wdn, g2, prev):
    n, d = x.shape
    n_ff, _, width = wup.shape
    full = lambda a: pl.BlockSpec(a.shape, lambda i: (0,) * a.ndim)
    return pl.pallas_call(
        functools.partial(_ffn_sample_kernel, n_ff=n_ff),
        grid=(1,),
        in_specs=[full(x), full(g1), full(wup), full(cw), full(cb), full(wdn), full(g2), full(prev)],
        out_specs=(pl.BlockSpec((n, d), lambda i: (0, 0)),
                   pl.BlockSpec((n_ff, n, width), lambda i: (0, 0, 0))),
        out_shape=(jax.ShapeDtypeStruct((n, d), F32), jax.ShapeDtypeStruct((n_ff, n, width), F32)),
        scratch_shapes=[pltpu.VMEM((n, d), F32)],
        compiler_params=pltpu.CompilerParams(dimension_semantics=("arbitrary",),
                                             vmem_limit_bytes=VMEM_LIMIT),
        name="ffn_sample",
    )(x, g1, wup, cw, cb, wdn, g2, prev)


def _rope_tables(pos):
    inv_freq = ROPE_THETA ** (-jnp.arange(ROT_HALF, dtype=F32) / ROT_HALF)
    ang = pos.astype(F32)[:, None] * inv_freq[None, :]
    cos, sin = jnp.cos(ang), jnp.sin(ang)
    n = pos.shape[0]
    rest = HEAD_DIM_A - 2 * ROT_HALF
    reps = LANES // HEAD_DIM_A
    c = jnp.tile(jnp.concatenate([cos, cos, jnp.ones((n, rest), F32)], axis=1), (1, reps))
    s_up = jnp.tile(jnp.concatenate([jnp.zeros((n, ROT_HALF), F32), sin, jnp.zeros((n, rest), F32)], axis=1),
                    (1, reps))
    s_dn = jnp.tile(jnp.concatenate([-sin, jnp.zeros((n, ROT_HALF + rest), F32)], axis=1), (1, reps))
    return c, s_up, s_dn


def _ff_chunks(a, d_ff):
    n_ff = d_ff // FF_CHUNK
    gate = a[..., :d_ff].reshape(a.shape[:-1] + (n_ff, FF_CHUNK))
    up = a[..., d_ff:].reshape(a.shape[:-1] + (n_ff, FF_CHUNK))
    return jnp.moveaxis(jnp.concatenate([gate, up], axis=-1), -2, 0)


def _ff_unchunks(a, d_ff):
    a = jnp.moveaxis(a, 0, -2)
    gate = a[..., :FF_CHUNK].reshape(a.shape[:-2] + (d_ff,))
    up = a[..., FF_CHUNK:].reshape(a.shape[:-2] + (d_ff,))
    return jnp.concatenate([gate, up], axis=-1)


def kernel(x_prompt, x_sample, cache_k_win, cache_v_win, state_gla, state_ffn_conv, g_mix_pre, g_mix_post,
           g_ffn_pre, g_ffn_post, w_in, w_gate2, b_gate, g_gla, w_out, w_up, conv_w, conv_b, w_down):
    depth = w_in.shape[0]
    b, t, d = x_prompt.shape
    bs, ts, _ = x_sample.shape
    win = cache_k_win.shape[2]
    d_ff = w_down.shape[1]
    n_ff = d_ff // FF_CHUNK
    conv_taps = conv_w.shape[1]
    n_main = 3 * D_A + 2 * DK_B + 2 * DV_B
    n_pair = N_HEADS_B // 2

    tabs_p = _rope_tables(jnp.arange(t, dtype=jnp.int32))
    pos_s = PAST_LEN + jnp.arange(SAMPLE_PAD, dtype=jnp.int32)
    tabs_s = tuple(jnp.tile(a, (bs, 1)) for a in _rope_tables(pos_s))

    xp = x_prompt.reshape(b * t, d)
    xs = jnp.pad(x_sample, ((0, 0), (0, SAMPLE_PAD - ts), (0, 0))).reshape(bs * SAMPLE_PAD, d)

    cache_kt = jnp.transpose(cache_k_win, (0, 1, 3, 4, 2))
    cache_vt = jnp.transpose(cache_v_win, (0, 1, 3, 4, 2))
    win_p = min(win, t)

    outs_p = {"k": [], "v": [], "s": [], "c": []}
    outs_s = {"k": [], "v": [], "s": [], "c": []}
    for l in range(depth):
        wm = w_in[l][:, :n_main].astype(BF16)
        wg1 = jnp.pad(w_in[l][:, n_main:], ((0, 0), (0, LANES - GATE_RANK))).astype(BF16)
        wg2 = jnp.pad(w_gate2[l], ((0, LANES - GATE_RANK), (0, 0))).astype(BF16)
        bg = b_gate[l][None, :]
        wo = w_out[l].astype(BF16)
        wup = _ff_chunks(w_up[l], d_ff).astype(BF16)
        cw = jnp.pad(_ff_chunks(conv_w[l], d_ff), ((0, 0), (0, SUBLANES - conv_taps), (0, 0)))
        cb = _ff_chunks(conv_b[l][None, :], d_ff)
        wdn = w_down[l].reshape(n_ff, FF_CHUNK, d).astype(BF16)
        g_pre, g_post = g_mix_pre[l][None, :], g_mix_post[l][None, :]
        g_f1, g_f2 = g_ffn_pre[l][None, :], g_ffn_post[l][None, :]
        gg = g_gla[l][None, :]

        qa, ka, va, qb, kb, vb, rb, la, kt_win, vt_win = _inproj(
            xp, g_pre, wm, wg1, wg2, bg, tabs_p, tm=256, gla_chunk=GLA_CHUNK, gla_valid=GLA_CHUNK, n_seq=b,
            win_rows=win_p)
        oa = _attn_prompt(qa.reshape(b, t, D_A), ka.reshape(b, t, D_A), va.reshape(b, t, D_A))
        s0 = jnp.zeros((b, n_pair, LANES, DV_HEAD), F32)
        ob, s_fin = _gla(qb, kb, vb, la, rb, gg, s0, n_seq=b, rows_per_step=512, chunk=GLA_CHUNK,
                         sub=GLA_SUB, t_valid=GLA_CHUNK)
        x1 = _outproj(oa.reshape(b * t, D_A), ob, xp, wo, g_post, tm=512)
        xp, tail = _ffn_prompt(x1, g_f1, wup, cw, cb, wdn, g_f2, n_seq=b, tm=512)
        to_rows = lambda a: jnp.transpose(a.reshape(b, N_HEADS_A, HEAD_DIM_A, win_p), (0, 3, 1, 2))
        outs_p["k"].append(to_rows(kt_win))
        outs_p["v"].append(to_rows(vt_win))
        outs_p["s"].append(s_fin.reshape(b, N_HEADS_B, DK_HEAD, DV_HEAD))
        outs_p["c"].append(_ff_unchunks(jnp.moveaxis(tail, 1, 0), d_ff)[:, SUBLANES - (conv_taps - 1):])

        qa, ka, va, qb, kb, vb, rb, la = _inproj(xs, g_pre, wm, wg1, wg2, bg, tabs_s, tm=bs * SAMPLE_PAD,
                                                 gla_chunk=SAMPLE_PAD, gla_valid=ts)
        ka3 = ka.reshape(bs, SAMPLE_PAD, D_A)
        va3 = va.reshape(bs, SAMPLE_PAD, D_A)
        oa = _attn_sample(qa.reshape(bs, SAMPLE_PAD, D_A), cache_kt, cache_vt, ka3, va3, layer=l)
        s0 = state_gla[l].reshape(bs, n_pair, LANES, DV_HEAD)
        ob, s_fin = _gla(qb, kb, vb, la, rb, gg, s0, n_seq=bs, rows_per_step=SAMPLE_PAD, chunk=SAMPLE_PAD,
                         sub=SAMPLE_PAD, t_valid=ts)
        x1 = _outproj(oa.reshape(bs * SAMPLE_PAD, D_A), ob, xs, wo, g_post, tm=bs * SAMPLE_PAD)
        prev = jnp.pad(state_ffn_conv[l], ((0, 0), (SUBLANES - (conv_taps - 1), 0), (0, 0)))
        prev = _ff_chunks(prev.reshape(bs * SUBLANES, 2 * d_ff), d_ff)
        xs, u_all = _ffn_sample(x1, g_f1, wup, cw, cb, wdn, g_f2, prev)
        u_all = _ff_unchunks(u_all, d_ff).reshape(bs, SAMPLE_PAD, 2 * d_ff)
        outs_s["k"].append(ka3[:, :ts].reshape(bs, ts, N_HEADS_A, HEAD_DIM_A))
        outs_s["v"].append(va3[:, :ts].reshape(bs, ts, N_HEADS_A, HEAD_DIM_A))
        outs_s["s"].append(s_fin.reshape(bs, N_HEADS_B, DK_HEAD, DV_HEAD))
        outs_s["c"].append(u_all[:, ts - (conv_taps - 1):ts])

    y_prompt = xp.reshape(b, t, d)
    y_sample = xs.reshape(bs, SAMPLE_PAD, d)[:, :ts]
    st = lambda xs_: jnp.stack(xs_)
    return (y_prompt, y_sample, st(outs_p["k"]), st(outs_p["v"]), st(outs_p["s"]), st(outs_p["c"]),
            st(outs_s["k"]), st(outs_s["v"]), st(outs_s["s"]), st(outs_s["c"]))
```

```python
import functools
import math

import jax
import jax.numpy as jnp
from jax import lax
from jax.experimental import pallas as pl
from jax.experimental.pallas import tpu as pltpu

F32 = jnp.float32
BF16 = jnp.bfloat16

PAST_LEN = 16384
N_HEADS_A = 8
HEAD_DIM_A = 64
D_A = N_HEADS_A * HEAD_DIM_A
ROT_HALF = HEAD_DIM_A // 8
ROPE_THETA = 500000.0
WIN_DENSE, WIN_MID, WIN_FAR = 128, 512, 2048
DILATIONS = (1, 4, 16)
BAND = 128
N_HEADS_B = 4
DK_HEAD = 64
DV_HEAD = 128
DK_B = N_HEADS_B * DK_HEAD
DV_B = N_HEADS_B * DV_HEAD
N_MAIN = 3 * D_A + 2 * DK_B + 2 * DV_B
GATE_RANK = 16
GATE_TAU = 16.0
GLA_CHUNK = 64
GLA_SUB = 16
EPS = 1e-6
NEG_BIG = -1e30

LANES = 128
SUBLANES = 8
VMEM_LIMIT = 56 * 1024 * 1024
SAMPLE_PAD = SUBLANES
FF_CHUNK = 2 * LANES
INPROJ_ROWS = 256
GLA_ROWS = 512
FFN_ROWS = 512
MERGE_ROWS = 256
ATT_UNROLL = 8

_NT = (((1,), (1,)), ((), ()))
_TN = (((0,), (0,)), ((), ()))


def _rms(x, g):
    ms = jnp.mean(x * x, axis=-1, keepdims=True)
    return x * lax.rsqrt(ms + EPS) * g


def _split3(x):
    hi = x.astype(BF16)
    r1 = x - hi.astype(F32)
    mid = r1.astype(BF16)
    lo = (r1 - mid.astype(F32)).astype(BF16)
    return hi, mid, lo


def _dilation_multiplicity(d):
    nonneg = d >= 0
    m0 = nonneg & (d <= WIN_DENSE)
    m1 = nonneg & (d <= WIN_MID) & ((d & 3) == 0)
    m2 = nonneg & (d <= WIN_FAR) & ((d & 15) == 0)
    return m0.astype(F32) + m1.astype(F32) + m2.astype(F32)


def _inproj_kernel(x_ref, g_ref, w_ref, wg2_ref, bg_ref, c_ref, su_ref, sd_ref, *refs,
                   prompt, tiles_per_seq, first_win_tile, gla_chunk, gla_valid):
    h = _rms(x_ref[...], g_ref[...]).astype(BF16)
    tm = h.shape[0]

    def proj(lo, hi):
        return jnp.dot(h, w_ref[:, lo:hi], preferred_element_type=F32)

    cos = c_ref[...]
    s_up = su_ref[...]
    s_dn = sd_ref[...]

    def rope(y):
        outs = []
        for j in range(D_A // LANES):
            yj = y[:, LANES * j:LANES * (j + 1)]
            outs.append(yj * cos + pltpu.roll(yj, ROT_HALF, 1) * s_up
                        + pltpu.roll(yj, LANES - ROT_HALF, 1) * s_dn)
        return jnp.concatenate(outs, axis=1)

    qkv = (rope(proj(0, D_A)) * (HEAD_DIM_A ** -0.5), rope(proj(D_A, 2 * D_A)), proj(2 * D_A, 3 * D_A))
    if prompt:
        sorted_refs, rest = refs[:9], refs[9:]
        qb_ref, kb_ref, vb_ref, rb_ref, la_ref, kt_ref, vt_ref, scr = rest
        for a, val in enumerate(qkv):
            sorted_refs[3 * a][0] = val.astype(BF16)
            for c in range(D_A // LANES):
                cl = slice(LANES * c, LANES * (c + 1))
                scr[a, c] = val[:, cl]
                for o_ref, s in zip(sorted_refs[3 * a + 1:3 * a + 3], DILATIONS[1:]):
                    for r in range(s):
                        o_ref[r, :, cl] = scr.at[a, c][pl.ds(r, tm // s, stride=s), :].astype(BF16)
    else:
        qa_ref, ka_ref, va_ref, qb_ref, kb_ref, vb_ref, rb_ref, la_ref = refs
        qa_ref[...], ka_ref[...], va_ref[...] = qkv
    o = 3 * D_A
    qb_ref[...] = proj(o, o + DK_B) * (DK_HEAD ** -0.5); o += DK_B
    kb_ref[...] = proj(o, o + DK_B); o += DK_B
    vb_ref[...] = proj(o, o + DV_B); o += DV_B
    rb_ref[...] = proj(o, o + DV_B); o += DV_B
    gate_lr = proj(o, o + LANES)
    z = jnp.dot(gate_lr.astype(BF16), wg2_ref[...], preferred_element_type=F32) + bg_ref[...]
    log_sig = jnp.minimum(z, 0.0) - jnp.log1p(jnp.exp(-jnp.abs(z)))
    log_a = log_sig * (1.0 / GATE_TAU)
    r = lax.broadcasted_iota(jnp.int32, (tm, tm), 0)
    c = lax.broadcasted_iota(jnp.int32, (tm, tm), 1)
    tri = ((c <= r) & (c // gla_chunk == r // gla_chunk) & (c % gla_chunk < gla_valid)).astype(BF16)
    la_ref[...] = sum(jnp.dot(tri, piece, preferred_element_type=F32) for piece in _split3(log_a))
    if prompt:
        @pl.when(pl.program_id(0) % tiles_per_seq >= first_win_tile)
        def _():
            for c in range(D_A // LANES):
                kt_ref[LANES * c:LANES * (c + 1), :] = scr[1, c].T
                vt_ref[LANES * c:LANES * (c + 1), :] = scr[2, c].T


def _inproj(x, g, w, wg2, bg, tabs, *, layer, tm, gla_chunk, gla_valid, n_seq=1, win_rows=0):
    n, d = x.shape
    prompt = win_rows > 0
    t = n // n_seq
    tiles_per_seq = t // tm
    first_win_tile = tiles_per_seq - win_rows // tm
    assert win_rows % tm == 0 and tm % DILATIONS[-1] == 0
    n_tab = tabs[0].shape[0] // tm
    row = lambda i: (i, 0)
    tab = lambda i: (i % n_tab, 0)
    lay3 = lambda i: (layer, 0, 0)
    f32_rows = lambda w_: (jax.ShapeDtypeStruct((n, w_), F32), pl.BlockSpec((tm, w_), row))
    outs = []
    if prompt:
        tile4 = lambda i: (i // tiles_per_seq, 0, i % tiles_per_seq, 0)
        for _ in range(3):
            for s in DILATIONS:
                outs.append((jax.ShapeDtypeStruct((n_seq, s, t // s, D_A), BF16),
                             pl.BlockSpec((None, s, tm // s, D_A), tile4)))
    else:
        outs += [f32_rows(D_A)] * 3
    outs += [f32_rows(DK_B), f32_rows(DK_B), f32_rows(DV_B), f32_rows(DV_B), f32_rows(DK_B)]
    scratch = []
    if prompt:
        win_tile = lambda i: (i // tiles_per_seq, 0, jnp.maximum(i % tiles_per_seq - first_win_tile, 0))
        outs += [(jax.ShapeDtypeStruct((n_seq, D_A, win_rows), F32), pl.BlockSpec((None, D_A, tm), win_tile))] * 2
        scratch = [pltpu.VMEM((3, D_A // LANES, tm, LANES), F32)]
    kern = functools.partial(_inproj_kernel, prompt=prompt, tiles_per_seq=tiles_per_seq,
                             first_win_tile=first_win_tile, gla_chunk=gla_chunk, gla_valid=gla_valid)
    return pl.pallas_call(
        kern,
        grid=(n // tm,),
        in_specs=[pl.BlockSpec((tm, d), row), pl.BlockSpec((None, 1, d), lay3),
                  pl.BlockSpec((None,) + w.shape[1:], lay3), pl.BlockSpec((None,) + wg2.shape[1:], lay3),
                  pl.BlockSpec((None, 1, DK_B), lay3),
                  pl.BlockSpec((tm, LANES), tab), pl.BlockSpec((tm, LANES), tab), pl.BlockSpec((tm, LANES), tab)],
        out_specs=tuple(spec for _, spec in outs),
        out_shape=tuple(shape for shape, _ in outs),
        scratch_shapes=scratch,
        compiler_params=pltpu.CompilerParams(dimension_semantics=("arbitrary",),
                                             vmem_limit_bytes=VMEM_LIMIT),
        name="inproj",
    )(x, g, w, wg2, bg, *tabs)


def _attn_prompt_kernel(*refs, t):
    sorted_refs, (bias_ref, o_ref, on, ln) = refs[:9], refs[9:]
    blk = BAND
    lane = lax.broadcasted_iota(jnp.int32, (1, LANES), 1)
    head0 = lane < HEAD_DIM_A
    head_sel = (head0, jnp.logical_not(head0))

    def band_blocks(pi, s, first_idx):
        q_ref, k_ref, v_ref = sorted_refs[pi], sorted_refs[3 + pi], sorted_refs[6 + pi]
        n_blk = t // s // blk
        blocks = []
        for u in range(ATT_UNROLL):
            idx = first_idx + u
            r = idx // n_blk
            j = idx % n_blk
            q0 = pl.multiple_of(j * blk, blk)
            w0 = pl.multiple_of(q0 - jnp.minimum(j, 1) * blk, blk)
            rows = pl.ds(q0, blk) if s == 1 else pl.ds(r + s * q0, blk, stride=s)
            blocks.append((q_ref[r, pl.ds(q0, blk), :], k_ref[r, pl.ds(w0, 2 * blk), :],
                           v_ref[r, pl.ds(w0, 2 * blk), :], bias_ref[jnp.minimum(j, 1)], rows))
        heads = [(u, sel) for u in range(ATT_UNROLL) for sel in head_sel]
        sc = [lax.dot_general(jnp.where(sel, blocks[u][0], jnp.zeros_like(blocks[u][0])), blocks[u][1], _NT,
                              preferred_element_type=F32) + blocks[u][3] for u, sel in heads]
        m = [jnp.max(x, axis=1, keepdims=True) for x in sc]
        p = [jnp.exp(x - mx) for x, mx in zip(sc, m)]
        l = [jnp.sum(x, axis=1, keepdims=True) for x in p]
        o = [jnp.dot(x.astype(BF16), blocks[u][2], preferred_element_type=F32) for x, (u, _) in zip(p, heads)]
        for u in range(ATT_UNROLL):
            rows = blocks[u][4]
            o0, o1 = o[2 * u] / l[2 * u], o[2 * u + 1] / l[2 * u + 1]
            lse0, lse1 = m[2 * u] + jnp.log(l[2 * u]), m[2 * u + 1] + jnp.log(l[2 * u + 1])
            on.at[pi][rows, :] = jnp.where(head0, o0, o1)
            ln.at[pi][rows, :] = jnp.where(head0, lse0, lse1)

    for pi, s in enumerate(DILATIONS):
        def one_group(gi, c, pi=pi, s=s):
            band_blocks(pi, s, gi * ATT_UNROLL)
            return c

        lax.fori_loop(0, t // blk // ATT_UNROLL, one_group, 0)

    def merge(i, c):
        rows = pl.ds(pl.multiple_of(i * MERGE_ROWS, MERGE_ROWS), MERGE_ROWS)
        lse = [ln[pi, rows, :] for pi in range(len(DILATIONS))]
        top = jnp.maximum(jnp.maximum(lse[0], lse[1]), lse[2])
        w = [jnp.exp(x - top) for x in lse]
        num = w[0] * on[0, rows, :] + w[1] * on[1, rows, :] + w[2] * on[2, rows, :]
        o_ref[rows, :] = num / (w[0] + w[1] + w[2])
        return c

    lax.fori_loop(0, t // MERGE_ROWS, merge, 0)


def _band_bias():
    iq = jnp.arange(BAND, dtype=jnp.int32)[:, None]
    c = jnp.arange(2 * BAND, dtype=jnp.int32)[None, :]
    d = jnp.stack([iq - c, iq + BAND - c])
    return jnp.where((d >= 0) & (d <= BAND), 0.0, NEG_BIG).astype(F32)


def _attn_prompt(sorted_qkv):
    b, _, t, _ = sorted_qkv[0].shape
    n_pair = D_A // LANES
    n_pat = len(DILATIONS)
    bias = _band_bias()
    pair4 = lambda bi, p: (bi, 0, 0, p)
    in_specs = [pl.BlockSpec((None,) + a.shape[1:3] + (LANES,), pair4) for a in sorted_qkv]
    in_specs.append(pl.BlockSpec(bias.shape, lambda bi, p: (0, 0, 0)))
    return pl.pallas_call(
        functools.partial(_attn_prompt_kernel, t=t),
        grid=(b, n_pair),
        in_specs=in_specs,
        out_specs=pl.BlockSpec((None, t, LANES), lambda bi, p: (bi, 0, p)),
        out_shape=jax.ShapeDtypeStruct((b, t, D_A), F32),
        scratch_shapes=[pltpu.VMEM((n_pat, t, LANES), F32)] * 2,
        compiler_params=pltpu.CompilerParams(dimension_semantics=("arbitrary", "arbitrary"),
                                             vmem_limit_bytes=VMEM_LIMIT),
        name="attn_prompt",
    )(*sorted_qkv, bias)


def _attn_sample_kernel(q_ref, kt_ref, vt_ref, kn_ref, vn_ref, o_ref, *, win):
    q = q_ref[...].astype(BF16)
    pad = jnp.zeros((LANES - SAMPLE_PAD, D_A), F32)
    kn_t = jnp.concatenate([kn_ref[...], pad], axis=0).T.astype(BF16)
    vn_t = jnp.concatenate([vn_ref[...], pad], axis=0).T.astype(BF16)
    n_keys = win + LANES
    tok = lax.broadcasted_iota(jnp.int32, (SAMPLE_PAD, n_keys), 0)
    key = lax.broadcasted_iota(jnp.int32, (SAMPLE_PAD, n_keys), 1)
    mult = _dilation_multiplicity(win + tok - key)
    live = mult > 0.0
    hs = [slice(HEAD_DIM_A * h, HEAD_DIM_A * (h + 1)) for h in range(N_HEADS_A)]
    kts = [jnp.concatenate([kt_ref[h].astype(BF16), kn_t[hs[h], :]], axis=1) for h in range(N_HEADS_A)]
    ss = [jnp.where(live, jnp.dot(q[:, hs[h]], kts[h], preferred_element_type=F32), NEG_BIG)
          for h in range(N_HEADS_A)]
    ps = []
    for s in ss:
        m = jnp.max(s, axis=1, keepdims=True)
        p = jnp.exp(s - m) * mult
        ps.append((p / jnp.sum(p, axis=1, keepdims=True)).astype(BF16))
    vts = [jnp.concatenate([vt_ref[h].astype(BF16), vn_t[hs[h], :]], axis=1) for h in range(N_HEADS_A)]
    head_cols = [lax.dot_general(vts[h], ps[h], _NT, preferred_element_type=F32) for h in range(N_HEADS_A)]
    o_ref[...] = jnp.concatenate(head_cols, axis=0).T


def _attn_sample(q, kt, vt, kn, vn, *, layer):
    bs = q.shape[0]
    win = kt.shape[-1]
    seq = lambda i: (i, 0, 0)
    slab = lambda i: (layer, i, 0, 0, 0)
    return pl.pallas_call(
        functools.partial(_attn_sample_kernel, win=win),
        grid=(bs,),
        in_specs=[pl.BlockSpec((None, SAMPLE_PAD, D_A), seq),
                  pl.BlockSpec((None, None, N_HEADS_A, HEAD_DIM_A, win), slab),
                  pl.BlockSpec((None, None, N_HEADS_A, HEAD_DIM_A, win), slab),
                  pl.BlockSpec((None, SAMPLE_PAD, D_A), seq), pl.BlockSpec((None, SAMPLE_PAD, D_A), seq)],
        out_specs=pl.BlockSpec((None, SAMPLE_PAD, D_A), seq),
        out_shape=jax.ShapeDtypeStruct((bs, SAMPLE_PAD, D_A), F32),
        compiler_params=pltpu.CompilerParams(dimension_semantics=("arbitrary",),
                                             vmem_limit_bytes=VMEM_LIMIT),
        name="attn_sample",
    )(q, kt, vt, kn, vn)


def _gla_kernel(q_ref, k_ref, v_ref, b_ref, r_ref, g_ref, s0_ref, o_ref, sf_ref, s_scr,
                *, chunk, sub, n_chunks, t_valid):
    n_pair = N_HEADS_B // 2
    pair_v = 2 * DV_HEAD

    @pl.when(pl.program_id(1) == 0)
    def _():
        zero = jnp.zeros((DK_HEAD, DV_HEAD), F32)
        for p in range(n_pair):
            s0 = s0_ref[p]
            s_scr[p] = jnp.concatenate([jnp.concatenate([s0[:DK_HEAD], zero], axis=1),
                                        jnp.concatenate([zero, s0[DK_HEAD:]], axis=1)], axis=0)

    n_sub = chunk // sub
    row = lax.broadcasted_iota(jnp.int32, (chunk, 1), 0)
    lane = lax.broadcasted_iota(jnp.int32, (1, LANES), 1)
    head0 = lane < DK_HEAD
    causal = (lax.broadcasted_iota(jnp.int32, (chunk, 2 * chunk), 1) % chunk
              <= lax.broadcasted_iota(jnp.int32, (chunk, 2 * chunk), 0))
    own_block = (lax.broadcasted_iota(jnp.int32, (LANES, pair_v), 0) // DK_HEAD
                 == lax.broadcasted_iota(jnp.int32, (LANES, pair_v), 1) // DV_HEAD)
    g = g_ref[...]
    keep = row < t_valid
    zero_v = jnp.zeros((chunk, DV_HEAD), BF16)

    items = [(c, p) for c in range(n_chunks) for p in range(n_pair)]
    rows_of = lambda c: slice(c * chunk, (c + 1) * chunk)
    b_, q_, k_, v_ = {}, {}, {}, {}
    for c, p in items:
        ls = slice(LANES * p, LANES * (p + 1))
        b_[c, p] = b_ref[rows_of(c), ls]
        q_[c, p] = q_ref[rows_of(c), ls]
        kp = k_ref[rows_of(c), ls]
        k_[c, p] = jnp.where(keep, kp, 0.0) if t_valid < chunk else kp
        v_[c, p] = v_ref[rows_of(c), pair_v * p:pair_v * (p + 1)].astype(BF16)
    b_last = {it: b_[it][chunk - 1:chunk, :] for it in items}
    q_dec = {it: (q_[it] * jnp.exp(b_[it])).astype(BF16) for it in items}
    k_hat = {it: (k_[it] * jnp.exp(b_last[it] - b_[it])).astype(BF16) for it in items}
    q_sub, k_sub = {}, {}
    for it in items:
        b = b_[it]
        for i in range(n_sub):
            r0, r1 = sub * i, sub * (i + 1)
            beta = b[r0:r0 + 1, :]
            q_sub[it, i] = (q_[it][r0:r1] * jnp.exp(b[r0:r1] - beta)).astype(BF16)
            k_i = (k_[it] * jnp.exp(jnp.where(row < r1, beta - b, 0.0))).astype(BF16)
            k_sub[it, i] = jnp.concatenate([jnp.where(head0, k_i, jnp.zeros_like(k_i)),
                                            jnp.where(head0, jnp.zeros_like(k_i), k_i)], axis=0)
    parts = {key: lax.dot_general(q_sub[key], k_sub[key], _NT, preferred_element_type=F32) for key in q_sub}
    attn = {}
    for it in items:
        a = parts[it, 0] if n_sub == 1 else jnp.concatenate([parts[it, i] for i in range(n_sub)], axis=0)
        attn[it] = jnp.where(causal, a, 0.0).astype(BF16)
    v_bd = {it: jnp.concatenate([jnp.concatenate([v_[it][:, :DV_HEAD], zero_v], axis=1),
                                 jnp.concatenate([zero_v, v_[it][:, DV_HEAD:]], axis=1)], axis=0) for it in items}
    o_intra = {it: jnp.dot(attn[it], v_bd[it], preferred_element_type=F32) for it in items}
    ds_ = {it: lax.dot_general(k_hat[it], v_[it], _TN, preferred_element_type=F32) for it in items}
    decay_col = {it: jnp.exp(jnp.broadcast_to(b_last[it], (SUBLANES, LANES)).T[:, 0:1]) for it in items}
    state = [s_scr[p] for p in range(n_pair)]
    for c, p in items:
        o_pair = o_intra[c, p] + jnp.dot(q_dec[c, p], state[p].astype(BF16), preferred_element_type=F32)
        state[p] = jnp.where(own_block, decay_col[c, p] * state[p] + ds_[c, p], 0.0)
        for hh in range(2):
            vs = slice(DV_HEAD * (2 * p + hh), DV_HEAD * (2 * p + hh + 1))
            r = r_ref[rows_of(c), vs]
            o = o_pair[:, DV_HEAD * hh:DV_HEAD * (hh + 1)]
            o_ref[rows_of(c), vs] = _rms(o, g) * (r * (1.0 / (1.0 + jnp.exp(-r))))
    for p in range(n_pair):
        s_scr[p] = state[p]
        sf_ref[p] = jnp.concatenate([state[p][:DK_HEAD, :DV_HEAD], state[p][DK_HEAD:, DV_HEAD:]], axis=0)


def _gla(q, k, v, la, r, g, s0, *, layer, n_seq, rows_per_step, chunk, sub, t_valid):
    n = q.shape[0]
    t = n // n_seq
    steps = t // rows_per_step
    row = lambda b, i: (b * steps + i, 0)
    st = lambda b, i: (b, 0, 0, 0)
    n_pair = N_HEADS_B // 2
    kern = functools.partial(_gla_kernel, chunk=chunk, sub=sub, n_chunks=rows_per_step // chunk,
                             t_valid=t_valid)
    return pl.pallas_call(
        kern,
        grid=(n_seq, steps),
        in_specs=[pl.BlockSpec((rows_per_step, DK_B), row), pl.BlockSpec((rows_per_step, DK_B), row),
                  pl.BlockSpec((rows_per_step, DV_B), row), pl.BlockSpec((rows_per_step, DK_B), row),
                  pl.BlockSpec((rows_per_step, DV_B), row),
                  pl.BlockSpec((None, 1, DV_HEAD), lambda b, i: (layer, 0, 0)),
                  pl.BlockSpec((None, n_pair, LANES, DV_HEAD), st)],
        out_specs=(pl.BlockSpec((rows_per_step, DV_B), row),
                   pl.BlockSpec((None, n_pair, LANES, DV_HEAD), st)),
        out_shape=(jax.ShapeDtypeStruct((n, DV_B), F32),
                   jax.ShapeDtypeStruct((n_seq, n_pair, LANES, DV_HEAD), F32)),
        scratch_shapes=[pltpu.VMEM((n_pair, LANES, 2 * DV_HEAD), F32)],
        compiler_params=pltpu.CompilerParams(dimension_semantics=("arbitrary", "arbitrary"),
                                             vmem_limit_bytes=VMEM_LIMIT),
        name="gla",
    )(q, k, v, la, r, g, s0)


def _gelu_tanh(x):
    c = math.sqrt(2.0 / math.pi)
    return 0.5 * x * (1.0 + jnp.tanh(c * (x + 0.044715 * (x * x * x))))


def _causal_conv(u, prev, cw, cb):
    rows, width = u.shape
    tiles = rows // SUBLANES
    u3 = u.reshape(tiles, SUBLANES, width)
    p3 = prev.reshape(tiles, SUBLANES, width)
    r = lax.broadcasted_iota(jnp.int32, (tiles, SUBLANES, width), 1)
    back1 = pltpu.roll(jnp.where(r >= SUBLANES - 1, p3, u3), 1, 1).reshape(rows, width)
    back2 = pltpu.roll(jnp.where(r >= SUBLANES - 2, p3, u3), 2, 1).reshape(rows, width)
    return cb + cw[0:1, :] * back2 + cw[1:2, :] * back1 + cw[2:3, :] * u


def _mix_ffn_core(oa_ref, ob_ref, x_ref, wo_ref, gpost_ref, g1_ref, wup_ref, cw_ref, cb_ref, wdn_ref, g2_ref,
                  y_ref, prev_cols, emit_u):
    d_ff = wdn_ref.shape[0]
    m = (jnp.dot(oa_ref[...].astype(BF16), wo_ref[0:D_A, :], preferred_element_type=F32)
         + jnp.dot(ob_ref[...].astype(BF16), wo_ref[D_A:D_A + DV_B, :], preferred_element_type=F32))
    x1 = x_ref[...] + _rms(m, gpost_ref[...])
    h = _rms(x1, g1_ref[...]).astype(BF16)
    ys = []
    for j in range(d_ff // FF_CHUNK):
        halves = []
        for base in (0, d_ff):
            cols = slice(base + j * FF_CHUNK, base + (j + 1) * FF_CHUNK)
            u = jnp.dot(h, wup_ref[:, cols], preferred_element_type=F32)
            halves.append(_causal_conv(u, prev_cols(cols, u), cw_ref[:, cols], cb_ref[:, cols]))
            emit_u(cols, u)
        ys.append((_gelu_tanh(halves[0]) * halves[1]).astype(BF16))
    f = jnp.dot(jnp.concatenate(ys, axis=1), wdn_ref[...], preferred_element_type=F32)
    y_ref[...] = x1 + _rms(f, g2_ref[...])


def _mix_ffn_prompt_kernel(*refs, tiles_per_seq):
    in_refs, (y_ref, tail_ref, carry) = refs[:11], refs[11:]
    tm = y_ref.shape[0]

    @pl.when(pl.program_id(0) % tiles_per_seq == 0)
    def _():
        carry[...] = jnp.zeros_like(carry)

    def prev_cols(cols, u):
        return jnp.concatenate([carry[:, cols], u[:tm - SUBLANES]], axis=0)

    def emit_u(cols, u):
        tail = u[tm - SUBLANES:]
        carry[:, cols] = tail
        tail_ref[:, cols] = tail

    _mix_ffn_core(*in_refs, y_ref, prev_cols, emit_u)


def _mix_ffn_sample_kernel(*refs):
    in_refs, (prev_ref, y_ref, u_ref) = refs[:11], refs[11:]

    def emit_u(cols, u):
        u_ref[:, cols] = u

    _mix_ffn_core(*in_refs, y_ref, lambda cols, u: prev_ref[:, cols], emit_u)


def _mix_ffn(oa, ob, x, wo, gpost, g1, wup, cw, cb, wdn, g2, *, layer, tm, n_seq=1, prev=None):
    n, d = x.shape
    width = wup.shape[-1]
    row = lambda i: (i, 0)
    lay3 = lambda i: (layer, 0, 0)
    single = pl.Buffered(1)
    weight = lambda a: pl.BlockSpec((None,) + a.shape[1:], lay3, pipeline_mode=single)
    small = lambda a: pl.BlockSpec((None,) + a.shape[1:], lay3)
    in_specs = [pl.BlockSpec((tm, D_A), row), pl.BlockSpec((tm, DV_B), row), pl.BlockSpec((tm, d), row),
                weight(wo), small(gpost), small(g1), weight(wup), small(cw), small(cb), weight(wdn), small(g2)]
    args = [oa, ob, x, wo, gpost, g1, wup, cw, cb, wdn, g2]
    if prev is None:
        tiles_per_seq = n // n_seq // tm
        kern = functools.partial(_mix_ffn_prompt_kernel, tiles_per_seq=tiles_per_seq)
        out_specs = (pl.BlockSpec((tm, d), row),
                     pl.BlockSpec((None, SUBLANES, width), lambda i: (i // tiles_per_seq, 0, 0)))
        out_shape = (jax.ShapeDtypeStruct((n, d), F32), jax.ShapeDtypeStruct((n_seq, SUBLANES, width), F32))
        scratch = [pltpu.VMEM((SUBLANES, width), F32)]
    else:
        kern = _mix_ffn_sample_kernel
        in_specs.append(pl.BlockSpec((None, tm, width), lambda i: (layer, i, 0)))
        args.append(prev)
        out_specs = (pl.BlockSpec((tm, d), row), pl.BlockSpec((tm, width), row))
        out_shape = (jax.ShapeDtypeStruct((n, d), F32), jax.ShapeDtypeStruct((n, width), F32))
        scratch = []
    return pl.pallas_call(
        kern,
        grid=(n // tm,),
        in_specs=in_specs,
        out_specs=out_specs,
        out_shape=out_shape,
        scratch_shapes=scratch,
        compiler_params=pltpu.CompilerParams(dimension_semantics=("arbitrary",),
                                             vmem_limit_bytes=VMEM_LIMIT),
        name="mix_ffn",
    )(*args)


def _rope_tables(pos):
    inv_freq = ROPE_THETA ** (-jnp.arange(ROT_HALF, dtype=F32) / ROT_HALF)
    ang = pos.astype(F32)[:, None] * inv_freq[None, :]
    cos, sin = jnp.cos(ang), jnp.sin(ang)
    n = pos.shape[0]
    rest = HEAD_DIM_A - 2 * ROT_HALF
    reps = LANES // HEAD_DIM_A
    c = jnp.tile(jnp.concatenate([cos, cos, jnp.ones((n, rest), F32)], axis=1), (1, reps))
    s_up = jnp.tile(jnp.concatenate([jnp.zeros((n, ROT_HALF), F32), sin, jnp.zeros((n, rest), F32)], axis=1),
                    (1, reps))
    s_dn = jnp.tile(jnp.concatenate([-sin, jnp.zeros((n, ROT_HALF + rest), F32)], axis=1), (1, reps))
    return c, s_up, s_dn


def kernel(x_prompt, x_sample, cache_k_win, cache_v_win, state_gla, state_ffn_conv, g_mix_pre, g_mix_post,
           g_ffn_pre, g_ffn_post, w_in, w_gate2, b_gate, g_gla, w_out, w_up, conv_w, conv_b, w_down):
    depth = w_in.shape[0]
    b, t, d = x_prompt.shape
    bs, ts, _ = x_sample.shape
    win = cache_k_win.shape[2]
    width = w_up.shape[-1]
    conv_taps = conv_w.shape[1]
    n_pair = N_HEADS_B // 2
    win_p = min(win, t)
    n_s = bs * SAMPLE_PAD

    w_in_bf = jnp.pad(w_in, ((0, 0), (0, 0), (0, N_MAIN + LANES - w_in.shape[-1]))).astype(BF16)
    wg2_bf = jnp.pad(w_gate2, ((0, 0), (0, LANES - GATE_RANK), (0, 0))).astype(BF16)
    w_out_bf, w_up_bf, w_dn_bf = w_out.astype(BF16), w_up.astype(BF16), w_down.astype(BF16)
    per_layer_row = lambda a: a[:, None, :]
    bg, gg, cb = per_layer_row(b_gate), per_layer_row(g_gla), per_layer_row(conv_b)
    g_pre, g_post = per_layer_row(g_mix_pre), per_layer_row(g_mix_post)
    g_f1, g_f2 = per_layer_row(g_ffn_pre), per_layer_row(g_ffn_post)

    tabs_p = _rope_tables(jnp.arange(t, dtype=jnp.int32))
    pos_s = PAST_LEN + jnp.arange(SAMPLE_PAD, dtype=jnp.int32)
    tabs_s = tuple(jnp.tile(a, (bs, 1)) for a in _rope_tables(pos_s))

    xp = x_prompt.reshape(b * t, d)
    xs = jnp.pad(x_sample, ((0, 0), (0, SAMPLE_PAD - ts), (0, 0))).reshape(n_s, d)

    cache_kt = jnp.transpose(cache_k_win, (0, 1, 3, 4, 2))
    cache_vt = jnp.transpose(cache_v_win, (0, 1, 3, 4, 2))
    prev_s = jnp.pad(state_ffn_conv, ((0, 0), (0, 0), (SUBLANES - (conv_taps - 1), 0), (0, 0)))
    prev_s = prev_s.reshape(depth, n_s, width)
    s0_p = jnp.zeros((b, n_pair, LANES, DV_HEAD), F32)
    s0_s = state_gla.reshape(depth, bs, n_pair, LANES, DV_HEAD)

    outs_p = {"k": [], "v": [], "s": [], "c": []}
    outs_s = {"k": [], "v": [], "s": [], "c": []}
    for l in range(depth):
        *sorted_qkv, qb, kb, vb, rb, la, kt_win, vt_win = _inproj(
            xp, g_pre, w_in_bf, wg2_bf, bg, tabs_p, layer=l, tm=INPROJ_ROWS, gla_chunk=GLA_CHUNK,
            gla_valid=GLA_CHUNK, n_seq=b, win_rows=win_p)
        q_s, k_s, v_s = sorted_qkv[0:3], sorted_qkv[3:6], sorted_qkv[6:9]
        oa = _attn_prompt(q_s + k_s + v_s)
        ob, s_fin = _gla(qb, kb, vb, la, rb, gg, s0_p, layer=l, n_seq=b, rows_per_step=GLA_ROWS,
                         chunk=GLA_CHUNK, sub=GLA_SUB, t_valid=GLA_CHUNK)
        xp, tail = _mix_ffn(oa.reshape(b * t, D_A), ob, xp, w_out_bf, g_post, g_f1, w_up_bf, conv_w, cb, w_dn_bf,
                            g_f2, layer=l, tm=FFN_ROWS, n_seq=b)
        to_rows = lambda a: jnp.transpose(a.reshape(b, N_HEADS_A, HEAD_DIM_A, win_p), (0, 3, 1, 2))
        outs_p["k"].append(to_rows(kt_win))
        outs_p["v"].append(to_rows(vt_win))
        outs_p["s"].append(s_fin.reshape(b, N_HEADS_B, DK_HEAD, DV_HEAD))
        outs_p["c"].append(tail[:, SUBLANES - (conv_taps - 1):])

        qa, ka, va, qb, kb, vb, rb, la = _inproj(xs, g_pre, w_in_bf, wg2_bf, bg, tabs_s, layer=l, tm=n_s,
                                                 gla_chunk=SAMPLE_PAD, gla_valid=ts)
        ka3 = ka.reshape(bs, SAMPLE_PAD, D_A)
        va3 = va.reshape(bs, SAMPLE_PAD, D_A)
        oa = _attn_sample(qa.reshape(bs, SAMPLE_PAD, D_A), cache_kt, cache_vt, ka3, va3, layer=l)
        ob, s_fin = _gla(qb, kb, vb, la, rb, gg, s0_s[l], layer=l, n_seq=bs, rows_per_step=SAMPLE_PAD,
                         chunk=SAMPLE_PAD, sub=SAMPLE_PAD, t_valid=ts)
        xs, u_all = _mix_ffn(oa.reshape(n_s, D_A), ob, xs, w_out_bf, g_post, g_f1, w_up_bf, conv_w, cb, w_dn_bf,
                             g_f2, layer=l, tm=n_s, prev=prev_s)
        outs_s["k"].append(ka3[:, :ts].reshape(bs, ts, N_HEADS_A, HEAD_DIM_A))
        outs_s["v"].append(va3[:, :ts].reshape(bs, ts, N_HEADS_A, HEAD_DIM_A))
        outs_s["s"].append(s_fin.reshape(bs, N_HEADS_B, DK_HEAD, DV_HEAD))
        outs_s["c"].append(u_all.reshape(bs, SAMPLE_PAD, width)[:, ts - (conv_taps - 1):ts])

    y_prompt = xp.reshape(b, t, d)
    y_sample = xs.reshape(bs, SAMPLE_PAD, d)[:, :ts]
    st = lambda xs_: jnp.stack(xs_)
    return (y_prompt, y_sample, st(outs_p["k"]), st(outs_p["v"]), st(outs_p["s"]), st(outs_p["c"]),
            st(outs_s["k"]), st(outs_s["v"]), st(outs_s["s"]), st(outs_s["c"]))
```

```python
import functools
import math

import jax
import jax.numpy as jnp
from jax import lax
from jax.experimental import pallas as pl
from jax.experimental.pallas import tpu as pltpu

F32 = jnp.float32
BF16 = jnp.bfloat16

PAST_LEN = 16384
N_HEADS_A = 8
HEAD_DIM_A = 64
D_A = N_HEADS_A * HEAD_DIM_A
ROT_HALF = HEAD_DIM_A // 8
ROPE_THETA = 500000.0
WIN_DENSE, WIN_MID, WIN_FAR = 128, 512, 2048
DILATIONS = (1, 4, 16)
BAND = 128
N_HEADS_B = 4
DK_HEAD = 64
DV_HEAD = 128
DK_B = N_HEADS_B * DK_HEAD
DV_B = N_HEADS_B * DV_HEAD
N_MAIN = 3 * D_A + 2 * DK_B + 2 * DV_B
GATE_RANK = 16
GATE_TAU = 16.0
GLA_CHUNK = 64
GLA_SUB = 16
EPS = 1e-6
NEG_BIG = -1e30

LANES = 128
SUBLANES = 8
VMEM_LIMIT = 56 * 1024 * 1024
SAMPLE_PAD = SUBLANES
FF_CHUNK = 2 * LANES
INPROJ_ROWS = 256
GLA_ROWS = 1024
GLA_SAMPLE_SEQS = 8
FFN_ROWS = 512
MERGE_ROWS = 256
ATT_UNROLL = 8

_NT = (((1,), (1,)), ((), ()))
_TN = (((0,), (0,)), ((), ()))


def _rms(x, g):
    ms = jnp.mean(x * x, axis=-1, keepdims=True)
    return x * lax.rsqrt(ms + EPS) * g


def _split3(x):
    hi = x.astype(BF16)
    r1 = x - hi.astype(F32)
    mid = r1.astype(BF16)
    lo = (r1 - mid.astype(F32)).astype(BF16)
    return hi, mid, lo


def _dilation_multiplicity(d):
    nonneg = d >= 0
    m0 = nonneg & (d <= WIN_DENSE)
    m1 = nonneg & (d <= WIN_MID) & ((d & 3) == 0)
    m2 = nonneg & (d <= WIN_FAR) & ((d & 15) == 0)
    return m0.astype(F32) + m1.astype(F32) + m2.astype(F32)


def _inproj_kernel(x_ref, g_ref, w_ref, wg2_ref, bg_ref, c_ref, su_ref, sd_ref, *refs,
                   prompt, gla_chunk, gla_valid):
    if prompt:
        sorted_refs, rest = refs[:9], refs[9:]
        qb_ref, kb_ref, vb_ref, rb_ref, la_ref, kt_ref, vt_ref, scr = rest
    else:
        qa_ref, ka_ref, va_ref, qb_ref, kb_ref, vb_ref, rb_ref, la_ref = refs
    h = _rms(x_ref[...], g_ref[...]).astype(BF16)
    tm = h.shape[0]

    def proj(lo, hi):
        return jnp.dot(h, w_ref[:, lo:hi], preferred_element_type=F32)

    cos = c_ref[...]
    s_up = su_ref[...]
    s_dn = sd_ref[...]

    def rope(y):
        outs = []
        for j in range(D_A // LANES):
            yj = y[:, LANES * j:LANES * (j + 1)]
            outs.append(yj * cos + pltpu.roll(yj, ROT_HALF, 1) * s_up
                        + pltpu.roll(yj, LANES - ROT_HALF, 1) * s_dn)
        return jnp.concatenate(outs, axis=1)

    gate_lr = proj(N_MAIN, N_MAIN + LANES)
    z = jnp.dot(gate_lr.astype(BF16), wg2_ref[...], preferred_element_type=F32) + bg_ref[...]
    log_sig = jnp.minimum(z, 0.0) - jnp.log1p(jnp.exp(-jnp.abs(z)))
    log_a = log_sig * (1.0 / GATE_TAU)
    r = lax.broadcasted_iota(jnp.int32, (tm, tm), 0)
    c = lax.broadcasted_iota(jnp.int32, (tm, tm), 1)
    tri = ((c <= r) & (c // gla_chunk == r // gla_chunk) & (c % gla_chunk < gla_valid)).astype(BF16)
    la_ref[...] = sum(jnp.dot(tri, piece, preferred_element_type=F32) for piece in _split3(log_a))

    qkv = (rope(proj(0, D_A)) * (HEAD_DIM_A ** -0.5), rope(proj(D_A, 2 * D_A)), proj(2 * D_A, 3 * D_A))
    if prompt:
        for a, val in enumerate(qkv):
            sorted_refs[3 * a][0] = val.astype(BF16)
            for c in range(D_A // LANES):
                cl = slice(LANES * c, LANES * (c + 1))
                scr[a, c] = val[:, cl]
                for o_ref, s in zip(sorted_refs[3 * a + 1:3 * a + 3], DILATIONS[1:]):
                    for r in range(s):
                        o_ref[r, :, cl] = scr.at[a, c][pl.ds(r, tm // s, stride=s), :].astype(BF16)
    else:
        qa_ref[...], ka_ref[...], va_ref[...] = qkv
    o = 3 * D_A
    qb_ref[...] = proj(o, o + DK_B) * (DK_HEAD ** -0.5); o += DK_B
    kb_ref[...] = proj(o, o + DK_B); o += DK_B
    vb_ref[...] = proj(o, o + DV_B); o += DV_B
    rb_ref[...] = proj(o, o + DV_B)
    if prompt:
        for c in range(D_A // LANES):
            kt_ref[LANES * c:LANES * (c + 1), :] = scr[1, c].T
            vt_ref[LANES * c:LANES * (c + 1), :] = scr[2, c].T


def _inproj(x, g, w, wg2, bg, tabs, *, layer, tm, gla_chunk, gla_valid, n_seq=1, win_rows=0):
    n, d = x.shape
    prompt = win_rows > 0
    t = n // n_seq
    tiles_per_seq = t // tm
    first_win_tile = tiles_per_seq - win_rows // tm
    assert win_rows % tm == 0 and tm % DILATIONS[-1] == 0
    n_tab = tabs[0].shape[0] // tm
    row = lambda i: (i, 0)
    tab = lambda i: (i % n_tab, 0)
    lay3 = lambda i: (layer, 0, 0)
    f32_rows = lambda w_: (jax.ShapeDtypeStruct((n, w_), F32), pl.BlockSpec((tm, w_), row))
    outs = []
    if prompt:
        tile4 = lambda i: (i // tiles_per_seq, 0, i % tiles_per_seq, 0)
        for _ in range(3):
            for s in DILATIONS:
                outs.append((jax.ShapeDtypeStruct((n_seq, s, t // s, D_A), BF16),
                             pl.BlockSpec((None, s, tm // s, D_A), tile4)))
    else:
        outs += [f32_rows(D_A)] * 3
    outs += [f32_rows(DK_B), f32_rows(DK_B), f32_rows(DV_B), f32_rows(DV_B), f32_rows(DK_B)]
    scratch = []
    if prompt:
        win_tile = lambda i: (i // tiles_per_seq, 0, jnp.maximum(i % tiles_per_seq - first_win_tile, 0))
        outs += [(jax.ShapeDtypeStruct((n_seq, D_A, win_rows), F32), pl.BlockSpec((None, D_A, tm), win_tile))] * 2
        scratch = [pltpu.VMEM((3, D_A // LANES, tm, LANES), F32)]
    kern = functools.partial(_inproj_kernel, prompt=prompt, gla_chunk=gla_chunk, gla_valid=gla_valid)
    return pl.pallas_call(
        kern,
        grid=(n // tm,),
        in_specs=[pl.BlockSpec((tm, d), row), pl.BlockSpec((None, 1, d), lay3),
                  pl.BlockSpec((None,) + w.shape[1:], lay3), pl.BlockSpec((None,) + wg2.shape[1:], lay3),
                  pl.BlockSpec((None, 1, DK_B), lay3),
                  pl.BlockSpec((tm, LANES), tab), pl.BlockSpec((tm, LANES), tab), pl.BlockSpec((tm, LANES), tab)],
        out_specs=tuple(spec for _, spec in outs),
        out_shape=tuple(shape for shape, _ in outs),
        scratch_shapes=scratch,
        compiler_params=pltpu.CompilerParams(dimension_semantics=("arbitrary",),
                                             vmem_limit_bytes=VMEM_LIMIT),
        name="inproj",
    )(x, g, w, wg2, bg, *tabs)


def _attn_prompt_kernel(*refs, t):
    sorted_refs, (bias_ref, o_ref, on, ln) = refs[:9], refs[9:]
    blk = BAND
    lane = lax.broadcasted_iota(jnp.int32, (1, LANES), 1)
    head0 = lane < HEAD_DIM_A
    head_sel = (head0, jnp.logical_not(head0))

    def band_blocks(pi, s, first_idx):
        q_ref, k_ref, v_ref = sorted_refs[pi], sorted_refs[3 + pi], sorted_refs[6 + pi]
        n_blk = t // s // blk
        blocks = []
        for u in range(ATT_UNROLL):
            idx = first_idx + u
            r = idx // n_blk
            j = idx % n_blk
            q0 = pl.multiple_of(j * blk, blk)
            w0 = pl.multiple_of(q0 - jnp.minimum(j, 1) * blk, blk)
            rows = pl.ds(q0, blk) if s == 1 else pl.ds(r + s * q0, blk, stride=s)
            blocks.append((q_ref[r, pl.ds(q0, blk), :], k_ref[r, pl.ds(w0, 2 * blk), :],
                           v_ref[r, pl.ds(w0, 2 * blk), :], bias_ref[jnp.minimum(j, 1)], rows))
        heads = [(u, sel) for u in range(ATT_UNROLL) for sel in head_sel]
        sc = [lax.dot_general(jnp.where(sel, blocks[u][0], jnp.zeros_like(blocks[u][0])), blocks[u][1], _NT,
                              preferred_element_type=F32) + blocks[u][3] for u, sel in heads]
        m = [jnp.max(x, axis=1, keepdims=True) for x in sc]
        p = [jnp.exp(x - mx) for x, mx in zip(sc, m)]
        l = [jnp.sum(x, axis=1, keepdims=True) for x in p]
        o = [jnp.dot(x.astype(BF16), blocks[u][2], preferred_element_type=F32) for x, (u, _) in zip(p, heads)]
        for u in range(ATT_UNROLL):
            rows = blocks[u][4]
            o0, o1 = o[2 * u] / l[2 * u], o[2 * u + 1] / l[2 * u + 1]
            lse0, lse1 = m[2 * u] + jnp.log(l[2 * u]), m[2 * u + 1] + jnp.log(l[2 * u + 1])
            on.at[pi][rows, :] = jnp.where(head0, o0, o1)
            ln.at[pi][rows, :] = jnp.where(head0, lse0, lse1)

    for pi, s in enumerate(DILATIONS):
        def one_group(gi, c, pi=pi, s=s):
            band_blocks(pi, s, gi * ATT_UNROLL)
            return c

        lax.fori_loop(0, t // blk // ATT_UNROLL, one_group, 0)

    def merge(i, c):
        rows = pl.ds(pl.multiple_of(i * MERGE_ROWS, MERGE_ROWS), MERGE_ROWS)
        lse = [ln[pi, rows, :] for pi in range(len(DILATIONS))]
        top = jnp.maximum(jnp.maximum(lse[0], lse[1]), lse[2])
        w = [jnp.exp(x - top) for x in lse]
        num = w[0] * on[0, rows, :] + w[1] * on[1, rows, :] + w[2] * on[2, rows, :]
        o_ref[rows, :] = num / (w[0] + w[1] + w[2])
        return c

    lax.fori_loop(0, t // MERGE_ROWS, merge, 0)


def _band_bias():
    iq = jnp.arange(BAND, dtype=jnp.int32)[:, None]
    c = jnp.arange(2 * BAND, dtype=jnp.int32)[None, :]
    d = jnp.stack([iq - c, iq + BAND - c])
    return jnp.where((d >= 0) & (d <= BAND), 0.0, NEG_BIG).astype(F32)


def _attn_prompt(sorted_qkv):
    b, _, t, _ = sorted_qkv[0].shape
    n_pair = D_A // LANES
    n_pat = len(DILATIONS)
    bias = _band_bias()
    pair4 = lambda bi, p: (bi, 0, 0, p)
    in_specs = [pl.BlockSpec((None,) + a.shape[1:3] + (LANES,), pair4) for a in sorted_qkv]
    in_specs.append(pl.BlockSpec(bias.shape, lambda bi, p: (0, 0, 0)))
    return pl.pallas_call(
        functools.partial(_attn_prompt_kernel, t=t),
        grid=(b, n_pair),
        in_specs=in_specs,
        out_specs=pl.BlockSpec((None, t, LANES), lambda bi, p: (bi, 0, p)),
        out_shape=jax.ShapeDtypeStruct((b, t, D_A), F32),
        scratch_shapes=[pltpu.VMEM((n_pat, t, LANES), F32)] * 2,
        compiler_params=pltpu.CompilerParams(dimension_semantics=("arbitrary", "arbitrary"),
                                             vmem_limit_bytes=VMEM_LIMIT),
        name="attn_prompt",
    )(*sorted_qkv, bias)


def _attn_sample_kernel(q_ref, kt_ref, vt_ref, kn_ref, vn_ref, o_ref, *, win):
    q = q_ref[...].astype(BF16)
    pad = jnp.zeros((LANES - SAMPLE_PAD, D_A), F32)
    kn_t = jnp.concatenate([kn_ref[...], pad], axis=0).T.astype(BF16)
    vn_t = jnp.concatenate([vn_ref[...], pad], axis=0).T.astype(BF16)
    n_keys = win + LANES
    tok = lax.broadcasted_iota(jnp.int32, (SAMPLE_PAD, n_keys), 0)
    key = lax.broadcasted_iota(jnp.int32, (SAMPLE_PAD, n_keys), 1)
    mult = _dilation_multiplicity(win + tok - key)
    live = mult > 0.0
    hs = [slice(HEAD_DIM_A * h, HEAD_DIM_A * (h + 1)) for h in range(N_HEADS_A)]
    kts = [jnp.concatenate([kt_ref[h].astype(BF16), kn_t[hs[h], :]], axis=1) for h in range(N_HEADS_A)]
    ss = [jnp.where(live, jnp.dot(q[:, hs[h]], kts[h], preferred_element_type=F32), NEG_BIG)
          for h in range(N_HEADS_A)]
    ps = []
    for s in ss:
        m = jnp.max(s, axis=1, keepdims=True)
        p = jnp.exp(s - m) * mult
        ps.append((p / jnp.sum(p, axis=1, keepdims=True)).astype(BF16))
    vts = [jnp.concatenate([vt_ref[h].astype(BF16), vn_t[hs[h], :]], axis=1) for h in range(N_HEADS_A)]
    head_cols = [lax.dot_general(vts[h], ps[h], _NT, preferred_element_type=F32) for h in range(N_HEADS_A)]
    o_ref[...] = jnp.concatenate(head_cols, axis=0).T


def _attn_sample(q, kt, vt, kn, vn, *, layer):
    bs = q.shape[0]
    win = kt.shape[-1]
    seq = lambda i: (i, 0, 0)
    slab = lambda i: (layer, i, 0, 0, 0)
    return pl.pallas_call(
        functools.partial(_attn_sample_kernel, win=win),
        grid=(bs,),
        in_specs=[pl.BlockSpec((None, SAMPLE_PAD, D_A), seq),
                  pl.BlockSpec((None, None, N_HEADS_A, HEAD_DIM_A, win), slab),
                  pl.BlockSpec((None, None, N_HEADS_A, HEAD_DIM_A, win), slab),
                  pl.BlockSpec((None, SAMPLE_PAD, D_A), seq), pl.BlockSpec((None, SAMPLE_PAD, D_A), seq)],
        out_specs=pl.BlockSpec((None, SAMPLE_PAD, D_A), seq),
        out_shape=jax.ShapeDtypeStruct((bs, SAMPLE_PAD, D_A), F32),
        compiler_params=pltpu.CompilerParams(dimension_semantics=("arbitrary",),
                                             vmem_limit_bytes=VMEM_LIMIT),
        name="attn_sample",
    )(q, kt, vt, kn, vn)


def _gla_kernel(q_ref, k_ref, v_ref, b_ref, r_ref, g_ref, s0_ref, o_ref, sf_ref, s_scr,
                *, chunk, sub, n_seqs, n_chunks, t_valid):
    n_pair = N_HEADS_B // 2
    pair_v = 2 * DV_HEAD
    chains = [(sq, p) for sq in range(n_seqs) for p in range(n_pair)]

    @pl.when(pl.program_id(1) == 0)
    def _():
        zero = jnp.zeros((DK_HEAD, DV_HEAD), F32)
        for sq, p in chains:
            s0 = s0_ref[sq, p]
            s_scr[sq, p] = jnp.concatenate([jnp.concatenate([s0[:DK_HEAD], zero], axis=1),
                                            jnp.concatenate([zero, s0[DK_HEAD:]], axis=1)], axis=0)

    n_sub = chunk // sub
    row = lax.broadcasted_iota(jnp.int32, (chunk, 1), 0)
    lane = lax.broadcasted_iota(jnp.int32, (1, LANES), 1)
    head0 = lane < DK_HEAD
    causal = (lax.broadcasted_iota(jnp.int32, (chunk, 2 * chunk), 1) % chunk
              <= lax.broadcasted_iota(jnp.int32, (chunk, 2 * chunk), 0))
    own_block = (lax.broadcasted_iota(jnp.int32, (LANES, pair_v), 0) // DK_HEAD
                 == lax.broadcasted_iota(jnp.int32, (LANES, pair_v), 1) // DV_HEAD)
    g = g_ref[...]
    keep = row < t_valid
    zero_v = jnp.zeros((chunk, DV_HEAD), BF16)

    items = [(sq, c, p) for c in range(n_chunks) for sq, p in chains]
    rows_of = lambda sq, c: slice((sq * n_chunks + c) * chunk, (sq * n_chunks + c + 1) * chunk)
    b_, q_, k_, v_ = {}, {}, {}, {}
    for sq, c, p in items:
        ls = slice(LANES * p, LANES * (p + 1))
        b_[sq, c, p] = b_ref[rows_of(sq, c), ls]
        q_[sq, c, p] = q_ref[rows_of(sq, c), ls]
        kp = k_ref[rows_of(sq, c), ls]
        k_[sq, c, p] = jnp.where(keep, kp, 0.0) if t_valid < chunk else kp
        v_[sq, c, p] = v_ref[rows_of(sq, c), pair_v * p:pair_v * (p + 1)].astype(BF16)
    b_last = {it: b_[it][chunk - 1:chunk, :] for it in items}
    q_dec = {it: (q_[it] * jnp.exp(b_[it])).astype(BF16) for it in items}
    k_hat = {it: (k_[it] * jnp.exp(b_last[it] - b_[it])).astype(BF16) for it in items}
    q_sub, k_sub = {}, {}
    for it in items:
        b = b_[it]
        for i in range(n_sub):
            r0, r1 = sub * i, sub * (i + 1)
            beta = b[r0:r0 + 1, :]
            q_sub[it, i] = (q_[it][r0:r1] * jnp.exp(b[r0:r1] - beta)).astype(BF16)
            k_i = (k_[it] * jnp.exp(jnp.where(row < r1, beta - b, 0.0))).astype(BF16)
            k_sub[it, i] = jnp.concatenate([jnp.where(head0, k_i, jnp.zeros_like(k_i)),
                                            jnp.where(head0, jnp.zeros_like(k_i), k_i)], axis=0)
    parts = {key: lax.dot_general(q_sub[key], k_sub[key], _NT, preferred_element_type=F32) for key in q_sub}
    attn = {}
    for it in items:
        a = parts[it, 0] if n_sub == 1 else jnp.concatenate([parts[it, i] for i in range(n_sub)], axis=0)
        attn[it] = jnp.where(causal, a, 0.0).astype(BF16)
    v_bd = {it: jnp.concatenate([jnp.concatenate([v_[it][:, :DV_HEAD], zero_v], axis=1),
                                 jnp.concatenate([zero_v, v_[it][:, DV_HEAD:]], axis=1)], axis=0) for it in items}
    o_intra = {it: jnp.dot(attn[it], v_bd[it], preferred_element_type=F32) for it in items}
    ds_ = {it: lax.dot_general(k_hat[it], v_[it], _TN, preferred_element_type=F32) for it in items}
    decay_col = {it: jnp.exp(jnp.broadcast_to(b_last[it], (SUBLANES, LANES)).T[:, 0:1]) for it in items}
    state = {ch: s_scr[ch] for ch in chains}
    for it in items:
        sq, c, p = it
        st = state[sq, p]
        o_pair = o_intra[it] + jnp.dot(q_dec[it], st.astype(BF16), preferred_element_type=F32)
        state[sq, p] = jnp.where(own_block, decay_col[it] * st + ds_[it], 0.0)
        for hh in range(2):
            vs = slice(DV_HEAD * (2 * p + hh), DV_HEAD * (2 * p + hh + 1))
            r = r_ref[rows_of(sq, c), vs]
            o = o_pair[:, DV_HEAD * hh:DV_HEAD * (hh + 1)]
            o_ref[rows_of(sq, c), vs] = _rms(o, g) * (r * (1.0 / (1.0 + jnp.exp(-r))))
    for ch in chains:
        s_scr[ch] = state[ch]
        sf_ref[ch] = jnp.concatenate([state[ch][:DK_HEAD, :DV_HEAD], state[ch][DK_HEAD:, DV_HEAD:]], axis=0)


def _gla(q, k, v, la, r, g, s0, *, layer, n_seq, seqs_per_step, rows_per_seq_step, chunk, sub, t_valid):
    n = q.shape[0]
    t = n // n_seq
    steps = t // rows_per_seq_step
    assert seqs_per_step == 1 or steps == 1
    rows_per_step = seqs_per_step * rows_per_seq_step
    row = lambda b, i: (b * steps + i, 0)
    st = lambda b, i: (b, 0, 0, 0)
    n_pair = N_HEADS_B // 2
    kern = functools.partial(_gla_kernel, chunk=chunk, sub=sub, n_seqs=seqs_per_step,
                             n_chunks=rows_per_seq_step // chunk, t_valid=t_valid)
    return pl.pallas_call(
        kern,
        grid=(n_seq // seqs_per_step, steps),
        in_specs=[pl.BlockSpec((rows_per_step, DK_B), row), pl.BlockSpec((rows_per_step, DK_B), row),
                  pl.BlockSpec((rows_per_step, DV_B), row), pl.BlockSpec((rows_per_step, DK_B), row),
                  pl.BlockSpec((rows_per_step, DV_B), row),
                  pl.BlockSpec((None, 1, DV_HEAD), lambda b, i: (layer, 0, 0)),
                  pl.BlockSpec((seqs_per_step, n_pair, LANES, DV_HEAD), st)],
        out_specs=(pl.BlockSpec((rows_per_step, DV_B), row),
                   pl.BlockSpec((seqs_per_step, n_pair, LANES, DV_HEAD), st)),
        out_shape=(jax.ShapeDtypeStruct((n, DV_B), F32),
                   jax.ShapeDtypeStruct((n_seq, n_pair, LANES, DV_HEAD), F32)),
        scratch_shapes=[pltpu.VMEM((seqs_per_step, n_pair, LANES, 2 * DV_HEAD), F32)],
        compiler_params=pltpu.CompilerParams(dimension_semantics=("arbitrary", "arbitrary"),
                                             vmem_limit_bytes=VMEM_LIMIT),
        name="gla",
    )(q, k, v, la, r, g, s0)


def _gelu_tanh(x):
    c = math.sqrt(2.0 / math.pi)
    return 0.5 * x * (1.0 + jnp.tanh(c * (x + 0.044715 * (x * x * x))))


def _causal_conv(u, prev, cw, cb):
    rows, width = u.shape
    tiles = rows // SUBLANES
    u3 = u.reshape(tiles, SUBLANES, width)
    p3 = prev.reshape(tiles, SUBLANES, width)
    r = lax.broadcasted_iota(jnp.int32, (tiles, SUBLANES, width), 1)
    back1 = pltpu.roll(jnp.where(r >= SUBLANES - 1, p3, u3), 1, 1).reshape(rows, width)
    back2 = pltpu.roll(jnp.where(r >= SUBLANES - 2, p3, u3), 2, 1).reshape(rows, width)
    return cb + cw[0:1, :] * back2 + cw[1:2, :] * back1 + cw[2:3, :] * u


def _mix_ffn_core(oa_ref, ob_ref, x_ref, wo_ref, gpost_ref, g1_ref, wup_ref, cw_ref, cb_ref, wdn_ref, g2_ref,
                  y_ref, prev_cols, emit_u):
    d_ff = wdn_ref.shape[0]
    m = (jnp.dot(oa_ref[...].astype(BF16), wo_ref[0:D_A, :], preferred_element_type=F32)
         + jnp.dot(ob_ref[...].astype(BF16), wo_ref[D_A:D_A + DV_B, :], preferred_element_type=F32))
    x1 = x_ref[...] + _rms(m, gpost_ref[...])
    h = _rms(x1, g1_ref[...]).astype(BF16)
    ys = []
    for j in range(d_ff // FF_CHUNK):
        halves = []
        for base in (0, d_ff):
            cols = slice(base + j * FF_CHUNK, base + (j + 1) * FF_CHUNK)
            u = jnp.dot(h, wup_ref[:, cols], preferred_element_type=F32)
            halves.append(_causal_conv(u, prev_cols(cols, u), cw_ref[:, cols], cb_ref[:, cols]))
            emit_u(cols, u)
        ys.append((_gelu_tanh(halves[0]) * halves[1]).astype(BF16))
    f = jnp.dot(jnp.concatenate(ys, axis=1), wdn_ref[...], preferred_element_type=F32)
    y_ref[...] = x1 + _rms(f, g2_ref[...])


def _mix_ffn_prompt_kernel(*refs, tiles_per_seq):
    in_refs, (y_ref, tail_ref, carry) = refs[:11], refs[11:]
    tm = y_ref.shape[0]

    @pl.when(pl.program_id(0) % tiles_per_seq == 0)
    def _():
        carry[...] = jnp.zeros_like(carry)

    def prev_cols(cols, u):
        return jnp.concatenate([carry[:, cols], u[:tm - SUBLANES]], axis=0)

    def emit_u(cols, u):
        tail = u[tm - SUBLANES:]
        carry[:, cols] = tail
        tail_ref[:, cols] = tail

    _mix_ffn_core(*in_refs, y_ref, prev_cols, emit_u)


def _mix_ffn_sample_kernel(*refs):
    in_refs, (prev_ref, y_ref, u_ref) = refs[:11], refs[11:]

    def emit_u(cols, u):
        u_ref[:, cols] = u

    _mix_ffn_core(*in_refs, y_ref, lambda cols, u: prev_ref[:, cols], emit_u)


def _mix_ffn(oa, ob, x, wo, gpost, g1, wup, cw, cb, wdn, g2, *, layer, tm, n_seq=1, prev=None):
    n, d = x.shape
    width = wup.shape[-1]
    row = lambda i: (i, 0)
    lay3 = lambda i: (layer, 0, 0)
    single = pl.Buffered(1)
    weight = lambda a: pl.BlockSpec((None,) + a.shape[1:], lay3, pipeline_mode=single)
    small = lambda a: pl.BlockSpec((None,) + a.shape[1:], lay3)
    in_specs = [pl.BlockSpec((tm, D_A), row), pl.BlockSpec((tm, DV_B), row), pl.BlockSpec((tm, d), row),
                weight(wo), small(gpost), small(g1), weight(wup), small(cw), small(cb), weight(wdn), small(g2)]
    args = [oa, ob, x, wo, gpost, g1, wup, cw, cb, wdn, g2]
    if prev is None:
        tiles_per_seq = n // n_seq // tm
        kern = functools.partial(_mix_ffn_prompt_kernel, tiles_per_seq=tiles_per_seq)
        out_specs = (pl.BlockSpec((tm, d), row),
                     pl.BlockSpec((None, SUBLANES, width), lambda i: (i // tiles_per_seq, 0, 0)))
        out_shape = (jax.ShapeDtypeStruct((n, d), F32), jax.ShapeDtypeStruct((n_seq, SUBLANES, width), F32))
        scratch = [pltpu.VMEM((SUBLANES, width), F32)]
    else:
        kern = _mix_ffn_sample_kernel
        in_specs.append(pl.BlockSpec((None, tm, width), lambda i: (layer, i, 0)))
        args.append(prev)
        out_specs = (pl.BlockSpec((tm, d), row), pl.BlockSpec((tm, width), row))
        out_shape = (jax.ShapeDtypeStruct((n, d), F32), jax.ShapeDtypeStruct((n, width), F32))
        scratch = []
    return pl.pallas_call(
        kern,
        grid=(n // tm,),
        in_specs=in_specs,
        out_specs=out_specs,
        out_shape=out_shape,
        scratch_shapes=scratch,
        compiler_params=pltpu.CompilerParams(dimension_semantics=("arbitrary",),
                                             vmem_limit_bytes=VMEM_LIMIT),
        name="mix_ffn",
    )(*args)


def _rope_tables(pos):
    inv_freq = ROPE_THETA ** (-jnp.arange(ROT_HALF, dtype=F32) / ROT_HALF)
    ang = pos.astype(F32)[:, None] * inv_freq[None, :]
    cos, sin = jnp.cos(ang), jnp.sin(ang)
    n = pos.shape[0]
    rest = HEAD_DIM_A - 2 * ROT_HALF
    reps = LANES // HEAD_DIM_A
    c = jnp.tile(jnp.concatenate([cos, cos, jnp.ones((n, rest), F32)], axis=1), (1, reps))
    s_up = jnp.tile(jnp.concatenate([jnp.zeros((n, ROT_HALF), F32), sin, jnp.zeros((n, rest), F32)], axis=1),
                    (1, reps))
    s_dn = jnp.tile(jnp.concatenate([-sin, jnp.zeros((n, ROT_HALF + rest), F32)], axis=1), (1, reps))
    return c, s_up, s_dn


def kernel(x_prompt, x_sample, cache_k_win, cache_v_win, state_gla, state_ffn_conv, g_mix_pre, g_mix_post,
           g_ffn_pre, g_ffn_post, w_in, w_gate2, b_gate, g_gla, w_out, w_up, conv_w, conv_b, w_down):
    depth = w_in.shape[0]
    b, t, d = x_prompt.shape
    bs, ts, _ = x_sample.shape
    win = cache_k_win.shape[2]
    width = w_up.shape[-1]
    conv_taps = conv_w.shape[1]
    n_pair = N_HEADS_B // 2
    win_p = min(win, t)
    n_s = bs * SAMPLE_PAD

    w_in_bf = jnp.pad(w_in, ((0, 0), (0, 0), (0, N_MAIN + LANES - w_in.shape[-1]))).astype(BF16)
    wg2_bf = jnp.pad(w_gate2, ((0, 0), (0, LANES - GATE_RANK), (0, 0))).astype(BF16)
    w_out_bf, w_up_bf, w_dn_bf = w_out.astype(BF16), w_up.astype(BF16), w_down.astype(BF16)
    per_layer_row = lambda a: a[:, None, :]
    bg, gg, cb = per_layer_row(b_gate), per_layer_row(g_gla), per_layer_row(conv_b)
    g_pre, g_post = per_layer_row(g_mix_pre), per_layer_row(g_mix_post)
    g_f1, g_f2 = per_layer_row(g_ffn_pre), per_layer_row(g_ffn_post)

    tabs_p = _rope_tables(jnp.arange(t, dtype=jnp.int32))
    pos_s = PAST_LEN + jnp.arange(SAMPLE_PAD, dtype=jnp.int32)
    tabs_s = tuple(jnp.tile(a, (bs, 1)) for a in _rope_tables(pos_s))

    xp = x_prompt.reshape(b * t, d)
    xs = jnp.pad(x_sample, ((0, 0), (0, SAMPLE_PAD - ts), (0, 0))).reshape(n_s, d)

    cache_kt = jnp.transpose(cache_k_win, (0, 1, 3, 4, 2))
    cache_vt = jnp.transpose(cache_v_win, (0, 1, 3, 4, 2))
    prev_s = jnp.pad(state_ffn_conv, ((0, 0), (0, 0), (SUBLANES - (conv_taps - 1), 0), (0, 0)))
    prev_s = prev_s.reshape(depth, n_s, width)
    s0_p = jnp.zeros((b, n_pair, LANES, DV_HEAD), F32)
    s0_s = state_gla.reshape(depth, bs, n_pair, LANES, DV_HEAD)

    outs_p = {"k": [], "v": [], "s": [], "c": []}
    outs_s = {"k": [], "v": [], "s": [], "c": []}
    for l in range(depth):
        *sorted_qkv, qb, kb, vb, rb, la, kt_win, vt_win = _inproj(
            xp, g_pre, w_in_bf, wg2_bf, bg, tabs_p, layer=l, tm=INPROJ_ROWS, gla_chunk=GLA_CHUNK,
            gla_valid=GLA_CHUNK, n_seq=b, win_rows=win_p)
        q_s, k_s, v_s = sorted_qkv[0:3], sorted_qkv[3:6], sorted_qkv[6:9]
        oa = _attn_prompt(q_s + k_s + v_s)
        ob, s_fin = _gla(qb, kb, vb, la, rb, gg, s0_p, layer=l, n_seq=b, seqs_per_step=1,
                         rows_per_seq_step=GLA_ROWS, chunk=GLA_CHUNK, sub=GLA_SUB, t_valid=GLA_CHUNK)
        xp, tail = _mix_ffn(oa.reshape(b * t, D_A), ob, xp, w_out_bf, g_post, g_f1, w_up_bf, conv_w, cb, w_dn_bf,
                            g_f2, layer=l, tm=FFN_ROWS, n_seq=b)
        to_rows = lambda a: jnp.transpose(a.reshape(b, N_HEADS_A, HEAD_DIM_A, win_p), (0, 3, 1, 2))
        outs_p["k"].append(to_rows(kt_win))
        outs_p["v"].append(to_rows(vt_win))
        outs_p["s"].append(s_fin.reshape(b, N_HEADS_B, DK_HEAD, DV_HEAD))
        outs_p["c"].append(tail[:, SUBLANES - (conv_taps - 1):])

        qa, ka, va, qb, kb, vb, rb, la = _inproj(xs, g_pre, w_in_bf, wg2_bf, bg, tabs_s, layer=l, tm=n_s,
                                                 gla_chunk=SAMPLE_PAD, gla_valid=ts)
        ka3 = ka.reshape(bs, SAMPLE_PAD, D_A)
        va3 = va.reshape(bs, SAMPLE_PAD, D_A)
        oa = _attn_sample(qa.reshape(bs, SAMPLE_PAD, D_A), cache_kt, cache_vt, ka3, va3, layer=l)
        ob, s_fin = _gla(qb, kb, vb, la, rb, gg, s0_s[l], layer=l, n_seq=bs, seqs_per_step=math.gcd(bs, GLA_SAMPLE_SEQS),
                         rows_per_seq_step=SAMPLE_PAD, chunk=SAMPLE_PAD, sub=SAMPLE_PAD, t_valid=ts)
        xs, u_all = _mix_ffn(oa.reshape(n_s, D_A), ob, xs, w_out_bf, g_post, g_f1, w_up_bf, conv_w, cb, w_dn_bf,
                             g_f2, layer=l, tm=n_s, prev=prev_s)
        outs_s["k"].append(ka3[:, :ts].reshape(bs, ts, N_HEADS_A, HEAD_DIM_A))
        outs_s["v"].append(va3[:, :ts].reshape(bs, ts, N_HEADS_A, HEAD_DIM_A))
        outs_s["s"].append(s_fin.reshape(bs, N_HEADS_B, DK_HEAD, DV_HEAD))
        outs_s["c"].append(u_all.reshape(bs, SAMPLE_PAD, width)[:, ts - (conv_taps - 1):ts])

    y_prompt = xp.reshape(b, t, d)
    y_sample = xs.reshape(bs, SAMPLE_PAD, d)[:, :ts]
    st = lambda xs_: jnp.stack(xs_)
    return (y_prompt, y_sample, st(outs_p["k"]), st(outs_p["v"]), st(outs_p["s"]), st(outs_p["c"]),
            st(outs_s["k"]), st(outs_s["v"]), st(outs_s["s"]), st(outs_s["c"]))
```

```python
import functools
import math

import jax
import jax.numpy as jnp
from jax import lax
from jax.experimental import pallas as pl
from jax.experimental.pallas import tpu as pltpu

F32 = jnp.float32
BF16 = jnp.bfloat16

PAST_LEN = 16384
N_HEADS_A = 8
HEAD_DIM_A = 64
D_A = N_HEADS_A * HEAD_DIM_A
ROT_HALF = HEAD_DIM_A // 8
ROPE_THETA = 500000.0
WIN_DENSE, WIN_MID, WIN_FAR = 128, 512, 2048
DILATIONS = (1, 4, 16)
BAND = 128
N_HEADS_B = 4
DK_HEAD = 64
DV_HEAD = 128
DK_B = N_HEADS_B * DK_HEAD
DV_B = N_HEADS_B * DV_HEAD
N_MAIN = 3 * D_A + 2 * DK_B + 2 * DV_B
GATE_RANK = 16
GATE_TAU = 16.0
GLA_CHUNK = 64
GLA_SUB = 16
EPS = 1e-6
NEG_BIG = -1e30

LANES = 128
SUBLANES = 8
VMEM_LIMIT = 56 * 1024 * 1024
SAMPLE_PAD = SUBLANES
FF_CHUNK = 2 * LANES
INPROJ_ROWS = 256
GLA_ROWS = 1024
GLA_SAMPLE_SEQS = 8
FFN_ROWS = 512
MERGE_ROWS = 256
ATT_UNROLL = 8

_NT = (((1,), (1,)), ((), ()))
_TN = (((0,), (0,)), ((), ()))


def _rms(x, g):
    ms = jnp.mean(x * x, axis=-1, keepdims=True)
    return x * lax.rsqrt(ms + EPS) * g


def _split3(x):
    hi = x.astype(BF16)
    r1 = x - hi.astype(F32)
    mid = r1.astype(BF16)
    lo = (r1 - mid.astype(F32)).astype(BF16)
    return hi, mid, lo


def _dilation_multiplicity(d):
    nonneg = d >= 0
    m0 = nonneg & (d <= WIN_DENSE)
    m1 = nonneg & (d <= WIN_MID) & ((d & 3) == 0)
    m2 = nonneg & (d <= WIN_FAR) & ((d & 15) == 0)
    return m0.astype(F32) + m1.astype(F32) + m2.astype(F32)


def _inproj_kernel(x_ref, g_ref, w_ref, wg2_ref, bg_ref, c_ref, su_ref, sd_ref, *refs,
                   prompt, gla_chunk, gla_valid):
    if prompt:
        sorted_refs, rest = refs[:9], refs[9:]
        qb_ref, kb_ref, vb_ref, rb_ref, la_ref, kt_ref, vt_ref, scr = rest
    else:
        qa_ref, ka_ref, va_ref, qb_ref, kb_ref, vb_ref, rb_ref, la_ref = refs
    h = _rms(x_ref[...], g_ref[...]).astype(BF16)
    tm = h.shape[0]

    def proj(lo, hi):
        return jnp.dot(h, w_ref[:, lo:hi], preferred_element_type=F32)

    cos = c_ref[...]
    s_up = su_ref[...]
    s_dn = sd_ref[...]

    def rope(y):
        outs = []
        for j in range(D_A // LANES):
            yj = y[:, LANES * j:LANES * (j + 1)]
            outs.append(yj * cos + pltpu.roll(yj, ROT_HALF, 1) * s_up
                        + pltpu.roll(yj, LANES - ROT_HALF, 1) * s_dn)
        return jnp.concatenate(outs, axis=1)

    gate_lr = proj(N_MAIN, N_MAIN + LANES)
    z = jnp.dot(gate_lr.astype(BF16), wg2_ref[...], preferred_element_type=F32) + bg_ref[...]
    log_sig = jnp.minimum(z, 0.0) - jnp.log1p(jnp.exp(-jnp.abs(z)))
    log_a = log_sig * (1.0 / GATE_TAU)
    r = lax.broadcasted_iota(jnp.int32, (tm, tm), 0)
    c = lax.broadcasted_iota(jnp.int32, (tm, tm), 1)
    tri = ((c <= r) & (c // gla_chunk == r // gla_chunk) & (c % gla_chunk < gla_valid)).astype(BF16)
    la_ref[...] = sum(jnp.dot(tri, piece, preferred_element_type=F32) for piece in _split3(log_a))

    qkv = (rope(proj(0, D_A)) * (HEAD_DIM_A ** -0.5), rope(proj(D_A, 2 * D_A)), proj(2 * D_A, 3 * D_A))
    if prompt:
        for a, val in enumerate(qkv):
            sorted_refs[3 * a][0] = val.astype(BF16)
            for c in range(D_A // LANES):
                cl = slice(LANES * c, LANES * (c + 1))
                scr[a, c] = val[:, cl]
                for o_ref, s in zip(sorted_refs[3 * a + 1:3 * a + 3], DILATIONS[1:]):
                    for r in range(s):
                        o_ref[r, :, cl] = scr.at[a, c][pl.ds(r, tm // s, stride=s), :].astype(BF16)
    else:
        qa_ref[...], ka_ref[...], va_ref[...] = qkv
    o = 3 * D_A
    qb_ref[...] = proj(o, o + DK_B) * (DK_HEAD ** -0.5); o += DK_B
    kb_ref[...] = proj(o, o + DK_B); o += DK_B
    vb_ref[...] = proj(o, o + DV_B); o += DV_B
    rb_ref[...] = proj(o, o + DV_B)
    if prompt:
        for c in range(D_A // LANES):
            kt_ref[LANES * c:LANES * (c + 1), :] = scr[1, c].T
            vt_ref[LANES * c:LANES * (c + 1), :] = scr[2, c].T


def _inproj(x, g, w, wg2, bg, tabs, *, layer, tm, gla_chunk, gla_valid, n_seq=1, win_rows=0):
    n, d = x.shape
    prompt = win_rows > 0
    t = n // n_seq
    tiles_per_seq = t // tm
    first_win_tile = tiles_per_seq - win_rows // tm
    assert win_rows % tm == 0 and tm % DILATIONS[-1] == 0
    n_tab = tabs[0].shape[0] // tm
    row = lambda i: (i, 0)
    tab = lambda i: (i % n_tab, 0)
    lay3 = lambda i: (layer, 0, 0)
    f32_rows = lambda w_: (jax.ShapeDtypeStruct((n, w_), F32), pl.BlockSpec((tm, w_), row))
    outs = []
    if prompt:
        tile4 = lambda i: (i // tiles_per_seq, 0, i % tiles_per_seq, 0)
        for _ in range(3):
            for s in DILATIONS:
                outs.append((jax.ShapeDtypeStruct((n_seq, s, t // s, D_A), BF16),
                             pl.BlockSpec((None, s, tm // s, D_A), tile4)))
    else:
        outs += [f32_rows(D_A)] * 3
    outs += [f32_rows(DK_B), f32_rows(DK_B), f32_rows(DV_B), f32_rows(DV_B), f32_rows(DK_B)]
    scratch = []
    if prompt:
        win_tile = lambda i: (i // tiles_per_seq, 0, jnp.maximum(i % tiles_per_seq - first_win_tile, 0))
        outs += [(jax.ShapeDtypeStruct((n_seq, D_A, win_rows), F32), pl.BlockSpec((None, D_A, tm), win_tile))] * 2
        scratch = [pltpu.VMEM((3, D_A // LANES, tm, LANES), F32)]
    kern = functools.partial(_inproj_kernel, prompt=prompt, gla_chunk=gla_chunk, gla_valid=gla_valid)
    return pl.pallas_call(
        kern,
        grid=(n // tm,),
        in_specs=[pl.BlockSpec((tm, d), row), pl.BlockSpec((None, 1, d), lay3),
                  pl.BlockSpec((None,) + w.shape[1:], lay3), pl.BlockSpec((None,) + wg2.shape[1:], lay3),
                  pl.BlockSpec((None, 1, DK_B), lay3),
                  pl.BlockSpec((tm, LANES), tab), pl.BlockSpec((tm, LANES), tab), pl.BlockSpec((tm, LANES), tab)],
        out_specs=tuple(spec for _, spec in outs),
        out_shape=tuple(shape for shape, _ in outs),
        scratch_shapes=scratch,
        compiler_params=pltpu.CompilerParams(dimension_semantics=("arbitrary",),
                                             vmem_limit_bytes=VMEM_LIMIT),
        name="inproj",
    )(x, g, w, wg2, bg, *tabs)


def _attn_prompt_kernel(*refs, t):
    sorted_refs, (bias_ref, o_ref, on, ln) = refs[:9], refs[9:]
    blk = BAND
    lane = lax.broadcasted_iota(jnp.int32, (1, LANES), 1)
    head0 = lane < HEAD_DIM_A
    head_sel = (head0, jnp.logical_not(head0))

    def band_blocks(pi, s, first_idx):
        q_ref, k_ref, v_ref = sorted_refs[pi], sorted_refs[3 + pi], sorted_refs[6 + pi]
        n_blk = t // s // blk
        blocks = []
        for u in range(ATT_UNROLL):
            idx = first_idx + u
            r = idx // n_blk
            j = idx % n_blk
            q0 = pl.multiple_of(j * blk, blk)
            w0 = pl.multiple_of(q0 - jnp.minimum(j, 1) * blk, blk)
            rows = pl.ds(q0, blk) if s == 1 else pl.ds(r + s * q0, blk, stride=s)
            blocks.append((q_ref[r, pl.ds(q0, blk), :], k_ref[r, pl.ds(w0, 2 * blk), :],
                           v_ref[r, pl.ds(w0, 2 * blk), :], bias_ref[jnp.minimum(j, 1)], rows))
        heads = [(u, sel) for u in range(ATT_UNROLL) for sel in head_sel]
        sc = [lax.dot_general(jnp.where(sel, blocks[u][0], jnp.zeros_like(blocks[u][0])), blocks[u][1], _NT,
                              preferred_element_type=F32) + blocks[u][3] for u, sel in heads]
        m = [jnp.max(x, axis=1, keepdims=True) for x in sc]
        p = [jnp.exp(x - mx) for x, mx in zip(sc, m)]
        l = [jnp.sum(x, axis=1, keepdims=True) for x in p]
        o = [jnp.dot(x.astype(BF16), blocks[u][2], preferred_element_type=F32) for x, (u, _) in zip(p, heads)]
        for u in range(ATT_UNROLL):
            rows = blocks[u][4]
            l_pair = jnp.where(head0, l[2 * u], l[2 * u + 1])
            on.at[pi][rows, :] = jnp.where(head0, o[2 * u], o[2 * u + 1]) / l_pair
            ln.at[pi][rows, :] = jnp.where(head0, m[2 * u], m[2 * u + 1]) + jnp.log(l_pair)

    for pi, s in enumerate(DILATIONS):
        def one_group(gi, c, pi=pi, s=s):
            band_blocks(pi, s, gi * ATT_UNROLL)
            return c

        lax.fori_loop(0, t // blk // ATT_UNROLL, one_group, 0)

    def merge(i, c):
        rows = pl.ds(pl.multiple_of(i * MERGE_ROWS, MERGE_ROWS), MERGE_ROWS)
        lse = [ln[pi, rows, :] for pi in range(len(DILATIONS))]
        top = jnp.maximum(jnp.maximum(lse[0], lse[1]), lse[2])
        w = [jnp.exp(x - top) for x in lse]
        num = w[0] * on[0, rows, :] + w[1] * on[1, rows, :] + w[2] * on[2, rows, :]
        o_ref[rows, :] = num / (w[0] + w[1] + w[2])
        return c

    lax.fori_loop(0, t // MERGE_ROWS, merge, 0)


def _band_bias():
    iq = jnp.arange(BAND, dtype=jnp.int32)[:, None]
    c = jnp.arange(2 * BAND, dtype=jnp.int32)[None, :]
    d = jnp.stack([iq - c, iq + BAND - c])
    return jnp.where((d >= 0) & (d <= BAND), 0.0, NEG_BIG).astype(F32)


def _attn_prompt(sorted_qkv):
    b, _, t, _ = sorted_qkv[0].shape
    n_pair = D_A // LANES
    n_pat = len(DILATIONS)
    bias = _band_bias()
    pair4 = lambda bi, p: (bi, 0, 0, p)
    in_specs = [pl.BlockSpec((None,) + a.shape[1:3] + (LANES,), pair4) for a in sorted_qkv]
    in_specs.append(pl.BlockSpec(bias.shape, lambda bi, p: (0, 0, 0)))
    return pl.pallas_call(
        functools.partial(_attn_prompt_kernel, t=t),
        grid=(b, n_pair),
        in_specs=in_specs,
        out_specs=pl.BlockSpec((None, t, LANES), lambda bi, p: (bi, 0, p)),
        out_shape=jax.ShapeDtypeStruct((b, t, D_A), F32),
        scratch_shapes=[pltpu.VMEM((n_pat, t, LANES), F32)] * 2,
        compiler_params=pltpu.CompilerParams(dimension_semantics=("arbitrary", "arbitrary"),
                                             vmem_limit_bytes=VMEM_LIMIT),
        name="attn_prompt",
    )(*sorted_qkv, bias)


def _attn_sample_kernel(q_ref, kt_ref, vt_ref, kn_ref, vn_ref, o_ref, *, win):
    q = q_ref[...].astype(BF16)
    pad = jnp.zeros((LANES - SAMPLE_PAD, D_A), F32)
    kn_t = jnp.concatenate([kn_ref[...], pad], axis=0).T.astype(BF16)
    vn_t = jnp.concatenate([vn_ref[...], pad], axis=0).T.astype(BF16)
    n_keys = win + LANES
    tok = lax.broadcasted_iota(jnp.int32, (SAMPLE_PAD, n_keys), 0)
    key = lax.broadcasted_iota(jnp.int32, (SAMPLE_PAD, n_keys), 1)
    mult = _dilation_multiplicity(win + tok - key)
    live = mult > 0.0
    hs = [slice(HEAD_DIM_A * h, HEAD_DIM_A * (h + 1)) for h in range(N_HEADS_A)]
    kts = [jnp.concatenate([kt_ref[h].astype(BF16), kn_t[hs[h], :]], axis=1) for h in range(N_HEADS_A)]
    ss = [jnp.where(live, jnp.dot(q[:, hs[h]], kts[h], preferred_element_type=F32), NEG_BIG)
          for h in range(N_HEADS_A)]
    ps = []
    for s in ss:
        m = jnp.max(s, axis=1, keepdims=True)
        p = jnp.exp(s - m) * mult
        ps.append((p / jnp.sum(p, axis=1, keepdims=True)).astype(BF16))
    vts = [jnp.concatenate([vt_ref[h].astype(BF16), vn_t[hs[h], :]], axis=1) for h in range(N_HEADS_A)]
    head_cols = [lax.dot_general(vts[h], ps[h], _NT, preferred_element_type=F32) for h in range(N_HEADS_A)]
    o_ref[...] = jnp.concatenate(head_cols, axis=0).T


def _attn_sample(q, kt, vt, kn, vn, *, layer):
    bs = q.shape[0]
    win = kt.shape[-1]
    seq = lambda i: (i, 0, 0)
    slab = lambda i: (layer, i, 0, 0, 0)
    return pl.pallas_call(
        functools.partial(_attn_sample_kernel, win=win),
        grid=(bs,),
        in_specs=[pl.BlockSpec((None, SAMPLE_PAD, D_A), seq),
                  pl.BlockSpec((None, None, N_HEADS_A, HEAD_DIM_A, win), slab),
                  pl.BlockSpec((None, None, N_HEADS_A, HEAD_DIM_A, win), slab),
                  pl.BlockSpec((None, SAMPLE_PAD, D_A), seq), pl.BlockSpec((None, SAMPLE_PAD, D_A), seq)],
        out_specs=pl.BlockSpec((None, SAMPLE_PAD, D_A), seq),
        out_shape=jax.ShapeDtypeStruct((bs, SAMPLE_PAD, D_A), F32),
        compiler_params=pltpu.CompilerParams(dimension_semantics=("arbitrary",),
                                             vmem_limit_bytes=VMEM_LIMIT),
        name="attn_sample",
    )(q, kt, vt, kn, vn)


def _gla_kernel(q_ref, k_ref, v_ref, b_ref, r_ref, g_ref, s0_ref, o_ref, sf_ref, s_scr,
                *, chunk, sub, n_seqs, n_chunks, t_valid):
    n_pair = N_HEADS_B // 2
    pair_v = 2 * DV_HEAD
    chains = [(sq, p) for sq in range(n_seqs) for p in range(n_pair)]

    @pl.when(pl.program_id(1) == 0)
    def _():
        zero = jnp.zeros((DK_HEAD, DV_HEAD), F32)
        for sq, p in chains:
            s0 = s0_ref[sq, p]
            s_scr[sq, p] = jnp.concatenate([jnp.concatenate([s0[:DK_HEAD], zero], axis=1),
                                            jnp.concatenate([zero, s0[DK_HEAD:]], axis=1)], axis=0)

    n_sub = chunk // sub
    row = lax.broadcasted_iota(jnp.int32, (chunk, 1), 0)
    lane = lax.broadcasted_iota(jnp.int32, (1, LANES), 1)
    head0 = lane < DK_HEAD
    causal = (lax.broadcasted_iota(jnp.int32, (chunk, 2 * chunk), 1) % chunk
              <= lax.broadcasted_iota(jnp.int32, (chunk, 2 * chunk), 0))
    own_block = (lax.broadcasted_iota(jnp.int32, (LANES, pair_v), 0) // DK_HEAD
                 == lax.broadcasted_iota(jnp.int32, (LANES, pair_v), 1) // DV_HEAD)
    g = g_ref[...]
    keep = row < t_valid
    zero_v = jnp.zeros((chunk, DV_HEAD), BF16)

    items = [(sq, c, p) for c in range(n_chunks) for sq, p in chains]
    rows_of = lambda sq, c: slice((sq * n_chunks + c) * chunk, (sq * n_chunks + c + 1) * chunk)
    b_, q_, k_, v_ = {}, {}, {}, {}
    for sq, c, p in items:
        ls = slice(LANES * p, LANES * (p + 1))
        b_[sq, c, p] = b_ref[rows_of(sq, c), ls]
        q_[sq, c, p] = q_ref[rows_of(sq, c), ls]
        kp = k_ref[rows_of(sq, c), ls]
        k_[sq, c, p] = jnp.where(keep, kp, 0.0) if t_valid < chunk else kp
        v_[sq, c, p] = v_ref[rows_of(sq, c), pair_v * p:pair_v * (p + 1)].astype(BF16)
    b_last = {it: b_[it][chunk - 1:chunk, :] for it in items}
    q_dec = {it: (q_[it] * jnp.exp(b_[it])).astype(BF16) for it in items}
    k_hat = {it: (k_[it] * jnp.exp(b_last[it] - b_[it])).astype(BF16) for it in items}
    q_sub, k_sub = {}, {}
    for it in items:
        b = b_[it]
        for i in range(n_sub):
            r0, r1 = sub * i, sub * (i + 1)
            beta = b[r0:r0 + 1, :]
            q_sub[it, i] = (q_[it][r0:r1] * jnp.exp(b[r0:r1] - beta)).astype(BF16)
            k_i = (k_[it] * jnp.exp(jnp.where(row < r1, beta - b, 0.0))).astype(BF16)
            k_sub[it, i] = jnp.concatenate([jnp.where(head0, k_i, jnp.zeros_like(k_i)),
                                            jnp.where(head0, jnp.zeros_like(k_i), k_i)], axis=0)
    parts = {key: lax.dot_general(q_sub[key], k_sub[key], _NT, preferred_element_type=F32) for key in q_sub}
    attn = {}
    for it in items:
        a = parts[it, 0] if n_sub == 1 else jnp.concatenate([parts[it, i] for i in range(n_sub)], axis=0)
        attn[it] = jnp.where(causal, a, 0.0).astype(BF16)
    v_bd = {it: jnp.concatenate([jnp.concatenate([v_[it][:, :DV_HEAD], zero_v], axis=1),
                                 jnp.concatenate([zero_v, v_[it][:, DV_HEAD:]], axis=1)], axis=0) for it in items}
    o_intra = {it: jnp.dot(attn[it], v_bd[it], preferred_element_type=F32) for it in items}
    ds_ = {it: lax.dot_general(k_hat[it], v_[it], _TN, preferred_element_type=F32) for it in items}
    decay_col = {it: jnp.exp(jnp.broadcast_to(b_last[it], (SUBLANES, LANES)).T[:, 0:1]) for it in items}
    state = {ch: s_scr[ch] for ch in chains}
    for it in items:
        sq, c, p = it
        st = state[sq, p]
        o_pair = o_intra[it] + jnp.dot(q_dec[it], st.astype(BF16), preferred_element_type=F32)
        state[sq, p] = jnp.where(own_block, decay_col[it] * st + ds_[it], 0.0)
        for hh in range(2):
            vs = slice(DV_HEAD * (2 * p + hh), DV_HEAD * (2 * p + hh + 1))
            r = r_ref[rows_of(sq, c), vs]
            o = o_pair[:, DV_HEAD * hh:DV_HEAD * (hh + 1)]
            o_ref[rows_of(sq, c), vs] = _rms(o, g) * (r * (1.0 / (1.0 + jnp.exp(-r))))
    for ch in chains:
        s_scr[ch] = state[ch]
        sf_ref[ch] = jnp.concatenate([state[ch][:DK_HEAD, :DV_HEAD], state[ch][DK_HEAD:, DV_HEAD:]], axis=0)


def _gla(q, k, v, la, r, g, s0, *, layer, n_seq, seqs_per_step, rows_per_seq_step, chunk, sub, t_valid):
    n = q.shape[0]
    t = n // n_seq
    steps = t // rows_per_seq_step
    assert seqs_per_step == 1 or steps == 1
    rows_per_step = seqs_per_step * rows_per_seq_step
    row = lambda b, i: (b * steps + i, 0)
    st = lambda b, i: (b, 0, 0, 0)
    n_pair = N_HEADS_B // 2
    kern = functools.partial(_gla_kernel, chunk=chunk, sub=sub, n_seqs=seqs_per_step,
                             n_chunks=rows_per_seq_step // chunk, t_valid=t_valid)
    return pl.pallas_call(
        kern,
        grid=(n_seq // seqs_per_step, steps),
        in_specs=[pl.BlockSpec((rows_per_step, DK_B), row), pl.BlockSpec((rows_per_step, DK_B), row),
                  pl.BlockSpec((rows_per_step, DV_B), row), pl.BlockSpec((rows_per_step, DK_B), row),
                  pl.BlockSpec((rows_per_step, DV_B), row),
                  pl.BlockSpec((None, 1, DV_HEAD), lambda b, i: (layer, 0, 0)),
                  pl.BlockSpec((seqs_per_step, n_pair, LANES, DV_HEAD), st)],
        out_specs=(pl.BlockSpec((rows_per_step, DV_B), row),
                   pl.BlockSpec((seqs_per_step, n_pair, LANES, DV_HEAD), st)),
        out_shape=(jax.ShapeDtypeStruct((n, DV_B), F32),
                   jax.ShapeDtypeStruct((n_seq, n_pair, LANES, DV_HEAD), F32)),
        scratch_shapes=[pltpu.VMEM((seqs_per_step, n_pair, LANES, 2 * DV_HEAD), F32)],
        compiler_params=pltpu.CompilerParams(dimension_semantics=("arbitrary", "arbitrary"),
                                             vmem_limit_bytes=VMEM_LIMIT),
        name="gla",
    )(q, k, v, la, r, g, s0)


def _gelu_tanh(x):
    c = math.sqrt(2.0 / math.pi)
    return 0.5 * x * (1.0 + jnp.tanh(c * (x + 0.044715 * (x * x * x))))


def _causal_conv(u, prev, cw, cb):
    rows, width = u.shape
    tiles = rows // SUBLANES
    u3 = u.reshape(tiles, SUBLANES, width)
    p3 = prev.reshape(tiles, SUBLANES, width)
    r = lax.broadcasted_iota(jnp.int32, (tiles, SUBLANES, width), 1)
    back1 = pltpu.roll(jnp.where(r >= SUBLANES - 1, p3, u3), 1, 1).reshape(rows, width)
    back2 = pltpu.roll(jnp.where(r >= SUBLANES - 2, p3, u3), 2, 1).reshape(rows, width)
    return cb + cw[0:1, :] * back2 + cw[1:2, :] * back1 + cw[2:3, :] * u


def _mix_ffn_core(oa_ref, ob_ref, x_ref, wo_ref, gpost_ref, g1_ref, wup_ref, cw_ref, cb_ref, wdn_ref, g2_ref,
                  y_ref, prev_cols, emit_u):
    d_ff = wdn_ref.shape[0]
    m = (jnp.dot(oa_ref[...].astype(BF16), wo_ref[0:D_A, :], preferred_element_type=F32)
         + jnp.dot(ob_ref[...].astype(BF16), wo_ref[D_A:D_A + DV_B, :], preferred_element_type=F32))
    x1 = x_ref[...] + _rms(m, gpost_ref[...])
    h = _rms(x1, g1_ref[...]).astype(BF16)
    ys = []
    for j in range(d_ff // FF_CHUNK):
        halves = []
        for base in (0, d_ff):
            cols = slice(base + j * FF_CHUNK, base + (j + 1) * FF_CHUNK)
            u = jnp.dot(h, wup_ref[:, cols], preferred_element_type=F32)
            halves.append(_causal_conv(u, prev_cols(cols, u), cw_ref[:, cols], cb_ref[:, cols]))
            emit_u(cols, u)
        ys.append((_gelu_tanh(halves[0]) * halves[1]).astype(BF16))
    f = jnp.dot(jnp.concatenate(ys, axis=1), wdn_ref[...], preferred_element_type=F32)
    y_ref[...] = x1 + _rms(f, g2_ref[...])


def _mix_ffn_prompt_kernel(*refs, tiles_per_seq):
    in_refs, (y_ref, tail_ref, carry) = refs[:11], refs[11:]
    tm = y_ref.shape[0]

    @pl.when(pl.program_id(0) % tiles_per_seq == 0)
    def _():
        carry[...] = jnp.zeros_like(carry)

    def prev_cols(cols, u):
        return jnp.concatenate([carry[:, cols], u[:tm - SUBLANES]], axis=0)

    def emit_u(cols, u):
        tail = u[tm - SUBLANES:]
        carry[:, cols] = tail
        tail_ref[:, cols] = tail

    _mix_ffn_core(*in_refs, y_ref, prev_cols, emit_u)


def _mix_ffn_sample_kernel(*refs):
    in_refs, (prev_ref, y_ref, u_ref) = refs[:11], refs[11:]

    def emit_u(cols, u):
        u_ref[:, cols] = u

    _mix_ffn_core(*in_refs, y_ref, lambda cols, u: prev_ref[:, cols], emit_u)


def _mix_ffn(oa, ob, x, wo, gpost, g1, wup, cw, cb, wdn, g2, *, layer, tm, n_seq=1, prev=None):
    n, d = x.shape
    width = wup.shape[-1]
    row = lambda i: (i, 0)
    lay3 = lambda i: (layer, 0, 0)
    single = pl.Buffered(1)
    weight = lambda a: pl.BlockSpec((None,) + a.shape[1:], lay3, pipeline_mode=single)
    small = lambda a: pl.BlockSpec((None,) + a.shape[1:], lay3)
    in_specs = [pl.BlockSpec((tm, D_A), row), pl.BlockSpec((tm, DV_B), row), pl.BlockSpec((tm, d), row),
                weight(wo), small(gpost), small(g1), weight(wup), small(cw), small(cb), weight(wdn), small(g2)]
    args = [oa, ob, x, wo, gpost, g1, wup, cw, cb, wdn, g2]
    if prev is None:
        tiles_per_seq = n // n_seq // tm
        kern = functools.partial(_mix_ffn_prompt_kernel, tiles_per_seq=tiles_per_seq)
        out_specs = (pl.BlockSpec((tm, d), row),
                     pl.BlockSpec((None, SUBLANES, width), lambda i: (i // tiles_per_seq, 0, 0)))
        out_shape = (jax.ShapeDtypeStruct((n, d), F32), jax.ShapeDtypeStruct((n_seq, SUBLANES, width), F32))
        scratch = [pltpu.VMEM((SUBLANES, width), F32)]
    else:
        kern = _mix_ffn_sample_kernel
        in_specs.append(pl.BlockSpec((None, tm, width), lambda i: (layer, i, 0)))
        args.append(prev)
        out_specs = (pl.BlockSpec((tm, d), row), pl.BlockSpec((tm, width), row))
        out_shape = (jax.ShapeDtypeStruct((n, d), F32), jax.ShapeDtypeStruct((n, width), F32))
        scratch = []
    return pl.pallas_call(
        kern,
        grid=(n // tm,),
        in_specs=in_specs,
        out_specs=out_specs,
        out_shape=out_shape,
        scratch_shapes=scratch,
        compiler_params=pltpu.CompilerParams(dimension_semantics=("arbitrary",),
                                             vmem_limit_bytes=VMEM_LIMIT),
        name="mix_ffn",
    )(*args)


def _rope_tables(pos):
    inv_freq = ROPE_THETA ** (-jnp.arange(ROT_HALF, dtype=F32) / ROT_HALF)
    ang = pos.astype(F32)[:, None] * inv_freq[None, :]
    cos, sin = jnp.cos(ang), jnp.sin(ang)
    n = pos.shape[0]
    rest = HEAD_DIM_A - 2 * ROT_HALF
    reps = LANES // HEAD_DIM_A
    c = jnp.tile(jnp.concatenate([cos, cos, jnp.ones((n, rest), F32)], axis=1), (1, reps))
    s_up = jnp.tile(jnp.concatenate([jnp.zeros((n, ROT_HALF), F32), sin, jnp.zeros((n, rest), F32)], axis=1),
                    (1, reps))
    s_dn = jnp.tile(jnp.concatenate([-sin, jnp.zeros((n, ROT_HALF + rest), F32)], axis=1), (1, reps))
    return c, s_up, s_dn


def kernel(x_prompt, x_sample, cache_k_win, cache_v_win, state_gla, state_ffn_conv, g_mix_pre, g_mix_post,
           g_ffn_pre, g_ffn_post, w_in, w_gate2, b_gate, g_gla, w_out, w_up, conv_w, conv_b, w_down):
    depth = w_in.shape[0]
    b, t, d = x_prompt.shape
    bs, ts, _ = x_sample.shape
    win = cache_k_win.shape[2]
    width = w_up.shape[-1]
    conv_taps = conv_w.shape[1]
    n_pair = N_HEADS_B // 2
    win_p = min(win, t)
    n_s = bs * SAMPLE_PAD

    w_in_bf = jnp.pad(w_in, ((0, 0), (0, 0), (0, N_MAIN + LANES - w_in.shape[-1]))).astype(BF16)
    wg2_bf = jnp.pad(w_gate2, ((0, 0), (0, LANES - GATE_RANK), (0, 0))).astype(BF16)
    w_out_bf, w_up_bf, w_dn_bf = w_out.astype(BF16), w_up.astype(BF16), w_down.astype(BF16)
    per_layer_row = lambda a: a[:, None, :]
    bg, gg, cb = per_layer_row(b_gate), per_layer_row(g_gla), per_layer_row(conv_b)
    g_pre, g_post = per_layer_row(g_mix_pre), per_layer_row(g_mix_post)
    g_f1, g_f2 = per_layer_row(g_ffn_pre), per_layer_row(g_ffn_post)

    tabs_p = _rope_tables(jnp.arange(t, dtype=jnp.int32))
    pos_s = PAST_LEN + jnp.arange(SAMPLE_PAD, dtype=jnp.int32)
    tabs_s = tuple(jnp.tile(a, (bs, 1)) for a in _rope_tables(pos_s))

    xp = x_prompt.reshape(b * t, d)
    xs = jnp.pad(x_sample, ((0, 0), (0, SAMPLE_PAD - ts), (0, 0))).reshape(n_s, d)

    cache_kt = jnp.transpose(cache_k_win, (0, 1, 3, 4, 2))
    cache_vt = jnp.transpose(cache_v_win, (0, 1, 3, 4, 2))
    prev_s = jnp.pad(state_ffn_conv, ((0, 0), (0, 0), (SUBLANES - (conv_taps - 1), 0), (0, 0)))
    prev_s = prev_s.reshape(depth, n_s, width)
    s0_p = jnp.zeros((b, n_pair, LANES, DV_HEAD), F32)
    s0_s = state_gla.reshape(depth, bs, n_pair, LANES, DV_HEAD)

    outs_p = {"k": [], "v": [], "s": [], "c": []}
    outs_s = {"k": [], "v": [], "s": [], "c": []}
    for l in range(depth):
        *sorted_qkv, qb, kb, vb, rb, la, kt_win, vt_win = _inproj(
            xp, g_pre, w_in_bf, wg2_bf, bg, tabs_p, layer=l, tm=INPROJ_ROWS, gla_chunk=GLA_CHUNK,
            gla_valid=GLA_CHUNK, n_seq=b, win_rows=win_p)
        q_s, k_s, v_s = sorted_qkv[0:3], sorted_qkv[3:6], sorted_qkv[6:9]
        oa = _attn_prompt(q_s + k_s + v_s)
        ob, s_fin = _gla(qb, kb, vb, la, rb, gg, s0_p, layer=l, n_seq=b, seqs_per_step=1,
                         rows_per_seq_step=GLA_ROWS, chunk=GLA_CHUNK, sub=GLA_SUB, t_valid=GLA_CHUNK)
        xp, tail = _mix_ffn(oa.reshape(b * t, D_A), ob, xp, w_out_bf, g_post, g_f1, w_up_bf, conv_w, cb, w_dn_bf,
                            g_f2, layer=l, tm=FFN_ROWS, n_seq=b)
        to_rows = lambda a: jnp.transpose(a.reshape(b, N_HEADS_A, HEAD_DIM_A, win_p), (0, 3, 1, 2))
        outs_p["k"].append(to_rows(kt_win))
        outs_p["v"].append(to_rows(vt_win))
        outs_p["s"].append(s_fin.reshape(b, N_HEADS_B, DK_HEAD, DV_HEAD))
        outs_p["c"].append(tail[:, SUBLANES - (conv_taps - 1):])

        qa, ka, va, qb, kb, vb, rb, la = _inproj(xs, g_pre, w_in_bf, wg2_bf, bg, tabs_s, layer=l, tm=n_s,
                                                 gla_chunk=SAMPLE_PAD, gla_valid=ts)
        ka3 = ka.reshape(bs, SAMPLE_PAD, D_A)
        va3 = va.reshape(bs, SAMPLE_PAD, D_A)
        oa = _attn_sample(qa.reshape(bs, SAMPLE_PAD, D_A), cache_kt, cache_vt, ka3, va3, layer=l)
        ob, s_fin = _gla(qb, kb, vb, la, rb, gg, s0_s[l], layer=l, n_seq=bs, seqs_per_step=math.gcd(bs, GLA_SAMPLE_SEQS),
                         rows_per_seq_step=SAMPLE_PAD, chunk=SAMPLE_PAD, sub=SAMPLE_PAD, t_valid=ts)
        xs, u_all = _mix_ffn(oa.reshape(n_s, D_A), ob, xs, w_out_bf, g_post, g_f1, w_up_bf, conv_w, cb, w_dn_bf,
                             g_f2, layer=l, tm=n_s, prev=prev_s)
        outs_s["k"].append(ka3[:, :ts].reshape(bs, ts, N_HEADS_A, HEAD_DIM_A))
        outs_s["v"].append(va3[:, :ts].reshape(bs, ts, N_HEADS_A, HEAD_DIM_A))
        outs_s["s"].append(s_fin.reshape(bs, N_HEADS_B, DK_HEAD, DV_HEAD))
        outs_s["c"].append(u_all.reshape(bs, SAMPLE_PAD, width)[:, ts - (conv_taps - 1):ts])

    y_prompt = xp.reshape(b, t, d)
    y_sample = xs.reshape(bs, SAMPLE_PAD, d)[:, :ts]
    st = lambda xs_: jnp.stack(xs_)
    return (y_prompt, y_sample, st(outs_p["k"]), st(outs_p["v"]), st(outs_p["s"]), st(outs_p["c"]),
            st(outs_s["k"]), st(outs_s["v"]), st(outs_s["s"]), st(outs_s["c"]))
```

```python
import functools
import math

import jax
import jax.numpy as jnp
from jax import lax
from jax.experimental import pallas as pl
from jax.experimental.pallas import tpu as pltpu

F32 = jnp.float32
BF16 = jnp.bfloat16

PAST_LEN = 16384
N_HEADS_A = 8
HEAD_DIM_A = 64
D_A = N_HEADS_A * HEAD_DIM_A
ROT_HALF = HEAD_DIM_A // 8
ROPE_THETA = 500000.0
WIN_DENSE, WIN_MID, WIN_FAR = 128, 512, 2048
DILATIONS = (1, 4, 16)
BAND = 128
N_HEADS_B = 4
DK_HEAD = 64
DV_HEAD = 128
DK_B = N_HEADS_B * DK_HEAD
DV_B = N_HEADS_B * DV_HEAD
N_MAIN = 3 * D_A + 2 * DK_B + 2 * DV_B
GATE_RANK = 16
GATE_TAU = 16.0
GLA_CHUNK = 64
GLA_SUB = 16
EPS = 1e-6
NEG_BIG = -1e30

LANES = 128
SUBLANES = 8
VMEM_LIMIT = 56 * 1024 * 1024
SAMPLE_PAD = SUBLANES
FF_CHUNK = 2 * LANES
INPROJ_ROWS = 256
GLA_ROWS = 1024
GLA_SAMPLE_SEQS = 8
FFN_ROWS = 512
MERGE_ROWS = 256
ATT_UNROLL = 8

_NT = (((1,), (1,)), ((), ()))
_TN = (((0,), (0,)), ((), ()))


def _rms(x, g):
    ms = jnp.mean(x * x, axis=-1, keepdims=True)
    return x * lax.rsqrt(ms + EPS) * g


def _split3(x):
    hi = x.astype(BF16)
    r1 = x - hi.astype(F32)
    mid = r1.astype(BF16)
    lo = (r1 - mid.astype(F32)).astype(BF16)
    return hi, mid, lo


def _dilation_multiplicity(d):
    nonneg = d >= 0
    m0 = nonneg & (d <= WIN_DENSE)
    m1 = nonneg & (d <= WIN_MID) & ((d & 3) == 0)
    m2 = nonneg & (d <= WIN_FAR) & ((d & 15) == 0)
    return m0.astype(F32) + m1.astype(F32) + m2.astype(F32)


def _inproj_kernel(x_ref, g_ref, w_ref, wg2_ref, bg_ref, c_ref, su_ref, sd_ref, *refs,
                   prompt, layer, n_alias, gla_chunk, gla_valid):
    refs = refs[n_alias:]
    if prompt:
        sorted_refs, rest = refs[:9], refs[9:]
        qb_ref, kb_ref, vb_ref, rb_ref, la_ref, kt_ref, vt_ref, scr = rest
    else:
        qa_ref, ka_ref, va_ref, qb_ref, kb_ref, vb_ref, rb_ref, la_ref = refs
    h = _rms(x_ref[...], g_ref[...]).astype(BF16)
    tm = h.shape[0]

    def proj(lo, hi):
        return jnp.dot(h, w_ref[:, lo:hi], preferred_element_type=F32)

    cos = c_ref[...]
    s_up = su_ref[...]
    s_dn = sd_ref[...]

    def rope(y):
        outs = []
        for j in range(D_A // LANES):
            yj = y[:, LANES * j:LANES * (j + 1)]
            outs.append(yj * cos + pltpu.roll(yj, ROT_HALF, 1) * s_up
                        + pltpu.roll(yj, LANES - ROT_HALF, 1) * s_dn)
        return jnp.concatenate(outs, axis=1)

    gate_lr = proj(N_MAIN, N_MAIN + LANES)
    z = jnp.dot(gate_lr.astype(BF16), wg2_ref[...], preferred_element_type=F32) + bg_ref[...]
    log_sig = jnp.minimum(z, 0.0) - jnp.log1p(jnp.exp(-jnp.abs(z)))
    log_a = log_sig * (1.0 / GATE_TAU)
    r = lax.broadcasted_iota(jnp.int32, (tm, tm), 0)
    c = lax.broadcasted_iota(jnp.int32, (tm, tm), 1)
    tri = ((c <= r) & (c // gla_chunk == r // gla_chunk) & (c % gla_chunk < gla_valid)).astype(BF16)
    la_ref[...] = sum(jnp.dot(tri, piece, preferred_element_type=F32) for piece in _split3(log_a))

    qkv = (rope(proj(0, D_A)) * (HEAD_DIM_A ** -0.5), rope(proj(D_A, 2 * D_A)), proj(2 * D_A, 3 * D_A))
    if prompt:
        for a, val in enumerate(qkv):
            sorted_refs[3 * a][0] = val.astype(BF16)
            for c in range(D_A // LANES):
                cl = slice(LANES * c, LANES * (c + 1))
                scr[a, c] = val[:, cl]
                for o_ref, s in zip(sorted_refs[3 * a + 1:3 * a + 3], DILATIONS[1:]):
                    for r in range(s):
                        o_ref[r, :, cl] = scr.at[a, c][pl.ds(r, tm // s, stride=s), :].astype(BF16)
    else:
        qa_ref[...], ka_ref[...], va_ref[...] = qkv
    o = 3 * D_A
    qb_ref[...] = proj(o, o + DK_B) * (DK_HEAD ** -0.5); o += DK_B
    kb_ref[...] = proj(o, o + DK_B); o += DK_B
    vb_ref[...] = proj(o, o + DV_B); o += DV_B
    rb_ref[...] = proj(o, o + DV_B)
    if prompt:
        for c in range(D_A // LANES):
            cr = slice(LANES * c, LANES * (c + 1))
            for ref, a in ((kt_ref, 1), (vt_ref, 2)):
                if n_alias:
                    ref[cr, :] = scr[a, c].T
                else:
                    for dl in range(ref.shape[0]):
                        ref[dl, cr, :] = scr[a, c].T if dl == layer else jnp.zeros((LANES, tm), F32)


def _inproj(x, g, w, wg2, bg, tabs, *, layer, tm, gla_chunk, gla_valid, n_seq=1, win_rows=0, win_prev=None):
    n, d = x.shape
    prompt = win_rows > 0
    t = n // n_seq
    tiles_per_seq = t // tm
    first_win_tile = tiles_per_seq - win_rows // tm
    assert win_rows % tm == 0 and tm % DILATIONS[-1] == 0
    n_tab = tabs[0].shape[0] // tm
    row = lambda i: (i, 0)
    tab = lambda i: (i % n_tab, 0)
    lay3 = lambda i: (layer, 0, 0)
    f32_rows = lambda w_: (jax.ShapeDtypeStruct((n, w_), F32), pl.BlockSpec((tm, w_), row))
    outs = []
    if prompt:
        tile4 = lambda i: (i // tiles_per_seq, 0, i % tiles_per_seq, 0)
        for _ in range(3):
            for s in DILATIONS:
                outs.append((jax.ShapeDtypeStruct((n_seq, s, t // s, D_A), BF16),
                             pl.BlockSpec((None, s, tm // s, D_A), tile4)))
    else:
        outs += [f32_rows(D_A)] * 3
    outs += [f32_rows(DK_B), f32_rows(DK_B), f32_rows(DV_B), f32_rows(DV_B), f32_rows(DK_B)]
    scratch = []
    if prompt:
        depth = w.shape[0]
        win_col = lambda i: jnp.maximum(i % tiles_per_seq - first_win_tile, 0)
        if win_prev is None:
            win_spec = pl.BlockSpec((depth, None, D_A, tm), lambda i: (0, i // tiles_per_seq, 0, win_col(i)))
        else:
            win_spec = pl.BlockSpec((None, None, D_A, tm), lambda i: (layer, i // tiles_per_seq, 0, win_col(i)))
        outs += [(jax.ShapeDtypeStruct((depth, n_seq, D_A, win_rows), F32), win_spec)] * 2
        scratch = [pltpu.VMEM((3, D_A // LANES, tm, LANES), F32)]
    alias_in = list(win_prev) if win_prev is not None else []
    n_in = 8
    aliases = {n_in + a: len(outs) - len(alias_in) + a for a in range(len(alias_in))}
    kern = functools.partial(_inproj_kernel, prompt=prompt, layer=layer, n_alias=len(alias_in),
                             gla_chunk=gla_chunk, gla_valid=gla_valid)
    return pl.pallas_call(
        kern,
        grid=(n // tm,),
        in_specs=[pl.BlockSpec((tm, d), row), pl.BlockSpec((None, 1, d), lay3),
                  pl.BlockSpec((None,) + w.shape[1:], lay3), pl.BlockSpec((None,) + wg2.shape[1:], lay3),
                  pl.BlockSpec((None, 1, DK_B), lay3),
                  pl.BlockSpec((tm, LANES), tab), pl.BlockSpec((tm, LANES), tab), pl.BlockSpec((tm, LANES), tab)]
        + [pl.BlockSpec(memory_space=pl.ANY)] * len(alias_in),
        out_specs=tuple(spec for _, spec in outs),
        out_shape=tuple(shape for shape, _ in outs),
        input_output_aliases=aliases,
        scratch_shapes=scratch,
        compiler_params=pltpu.CompilerParams(dimension_semantics=("arbitrary",),
                                             vmem_limit_bytes=VMEM_LIMIT),
        name="inproj",
    )(x, g, w, wg2, bg, *tabs, *alias_in)


def _attn_prompt_kernel(*refs, t):
    sorted_refs, (bias_ref, o_ref, on, ln) = refs[:9], refs[9:]
    blk = BAND
    lane = lax.broadcasted_iota(jnp.int32, (1, LANES), 1)
    head0 = lane < HEAD_DIM_A
    head_sel = (head0, jnp.logical_not(head0))

    def band_blocks(pi, s, first_idx):
        q_ref, k_ref, v_ref = sorted_refs[pi], sorted_refs[3 + pi], sorted_refs[6 + pi]
        n_blk = t // s // blk
        blocks = []
        for u in range(ATT_UNROLL):
            idx = first_idx + u
            r = idx // n_blk
            j = idx % n_blk
            q0 = pl.multiple_of(j * blk, blk)
            w0 = pl.multiple_of(q0 - jnp.minimum(j, 1) * blk, blk)
            rows = pl.ds(q0, blk) if s == 1 else pl.ds(r + s * q0, blk, stride=s)
            blocks.append((q_ref[r, pl.ds(q0, blk), :], k_ref[r, pl.ds(w0, 2 * blk), :],
                           v_ref[r, pl.ds(w0, 2 * blk), :], bias_ref[jnp.minimum(j, 1)], rows))
        heads = [(u, sel) for u in range(ATT_UNROLL) for sel in head_sel]
        sc = [lax.dot_general(jnp.where(sel, blocks[u][0], jnp.zeros_like(blocks[u][0])), blocks[u][1], _NT,
                              preferred_element_type=F32) + blocks[u][3] for u, sel in heads]
        m = [jnp.max(x, axis=1, keepdims=True) for x in sc]
        p = [jnp.exp(x - mx) for x, mx in zip(sc, m)]
        l = [jnp.sum(x, axis=1, keepdims=True) for x in p]
        o = [jnp.dot(x.astype(BF16), blocks[u][2], preferred_element_type=F32) for x, (u, _) in zip(p, heads)]
        for u in range(ATT_UNROLL):
            rows = blocks[u][4]
            l_pair = jnp.where(head0, l[2 * u], l[2 * u + 1])
            on.at[pi][rows, :] = jnp.where(head0, o[2 * u], o[2 * u + 1]) / l_pair
            ln.at[pi][rows, :] = jnp.where(head0, m[2 * u], m[2 * u + 1]) + jnp.log(l_pair)

    for pi, s in enumerate(DILATIONS):
        def one_group(gi, c, pi=pi, s=s):
            band_blocks(pi, s, gi * ATT_UNROLL)
            return c

        lax.fori_loop(0, t // blk // ATT_UNROLL, one_group, 0)

    def merge(i, c):
        rows = pl.ds(pl.multiple_of(i * MERGE_ROWS, MERGE_ROWS), MERGE_ROWS)
        lse = [ln[pi, rows, :] for pi in range(len(DILATIONS))]
        top = jnp.maximum(jnp.maximum(lse[0], lse[1]), lse[2])
        w = [jnp.exp(x - top) for x in lse]
        num = w[0] * on[0, rows, :] + w[1] * on[1, rows, :] + w[2] * on[2, rows, :]
        o_ref[rows, :] = num / (w[0] + w[1] + w[2])
        return c

    lax.fori_loop(0, t // MERGE_ROWS, merge, 0)


def _band_bias():
    iq = jnp.arange(BAND, dtype=jnp.int32)[:, None]
    c = jnp.arange(2 * BAND, dtype=jnp.int32)[None, :]
    d = jnp.stack([iq - c, iq + BAND - c])
    return jnp.where((d >= 0) & (d <= BAND), 0.0, NEG_BIG).astype(F32)


def _attn_prompt(sorted_qkv):
    b, _, t, _ = sorted_qkv[0].shape
    n_pair = D_A // LANES
    n_pat = len(DILATIONS)
    bias = _band_bias()
    pair4 = lambda bi, p: (bi, 0, 0, p)
    in_specs = [pl.BlockSpec((None,) + a.shape[1:3] + (LANES,), pair4) for a in sorted_qkv]
    in_specs.append(pl.BlockSpec(bias.shape, lambda bi, p: (0, 0, 0)))
    return pl.pallas_call(
        functools.partial(_attn_prompt_kernel, t=t),
        grid=(b, n_pair),
        in_specs=in_specs,
        out_specs=pl.BlockSpec((None, t, LANES), lambda bi, p: (bi, 0, p)),
        out_shape=jax.ShapeDtypeStruct((b, t, D_A), F32),
        scratch_shapes=[pltpu.VMEM((n_pat, t, LANES), F32)] * 2,
        compiler_params=pltpu.CompilerParams(dimension_semantics=("arbitrary", "arbitrary"),
                                             vmem_limit_bytes=VMEM_LIMIT),
        name="attn_prompt",
    )(*sorted_qkv, bias)


def _attn_sample_kernel(q_ref, kt_ref, vt_ref, kn_ref, vn_ref, o_ref, *, win):
    q = q_ref[...].astype(BF16)
    pad = jnp.zeros((LANES - SAMPLE_PAD, D_A), F32)
    kn_t = jnp.concatenate([kn_ref[...], pad], axis=0).T.astype(BF16)
    vn_t = jnp.concatenate([vn_ref[...], pad], axis=0).T.astype(BF16)
    n_keys = win + LANES
    tok = lax.broadcasted_iota(jnp.int32, (SAMPLE_PAD, n_keys), 0)
    key = lax.broadcasted_iota(jnp.int32, (SAMPLE_PAD, n_keys), 1)
    mult = _dilation_multiplicity(win + tok - key)
    live = mult > 0.0
    hs = [slice(HEAD_DIM_A * h, HEAD_DIM_A * (h + 1)) for h in range(N_HEADS_A)]
    kts = [jnp.concatenate([kt_ref[h].astype(BF16), kn_t[hs[h], :]], axis=1) for h in range(N_HEADS_A)]
    ss = [jnp.where(live, jnp.dot(q[:, hs[h]], kts[h], preferred_element_type=F32), NEG_BIG)
          for h in range(N_HEADS_A)]
    ps = []
    for s in ss:
        m = jnp.max(s, axis=1, keepdims=True)
        p = jnp.exp(s - m) * mult
        ps.append((p / jnp.sum(p, axis=1, keepdims=True)).astype(BF16))
    vts = [jnp.concatenate([vt_ref[h].astype(BF16), vn_t[hs[h], :]], axis=1) for h in range(N_HEADS_A)]
    head_cols = [lax.dot_general(vts[h], ps[h], _NT, preferred_element_type=F32) for h in range(N_HEADS_A)]
    o_ref[...] = jnp.concatenate(head_cols, axis=0).T


def _attn_sample(q, kt, vt, kn, vn, *, layer):
    bs = q.shape[0]
    win = kt.shape[-1]
    seq = lambda i: (i, 0, 0)
    slab = lambda i: (layer, i, 0, 0, 0)
    return pl.pallas_call(
        functools.partial(_attn_sample_kernel, win=win),
        grid=(bs,),
        in_specs=[pl.BlockSpec((None, SAMPLE_PAD, D_A), seq),
                  pl.BlockSpec((None, None, N_HEADS_A, HEAD_DIM_A, win), slab),
                  pl.BlockSpec((None, None, N_HEADS_A, HEAD_DIM_A, win), slab),
                  pl.BlockSpec((None, SAMPLE_PAD, D_A), seq), pl.BlockSpec((None, SAMPLE_PAD, D_A), seq)],
        out_specs=pl.BlockSpec((None, SAMPLE_PAD, D_A), seq),
        out_shape=jax.ShapeDtypeStruct((bs, SAMPLE_PAD, D_A), F32),
        compiler_params=pltpu.CompilerParams(dimension_semantics=("arbitrary",),
                                             vmem_limit_bytes=VMEM_LIMIT),
        name="attn_sample",
    )(q, kt, vt, kn, vn)


def _gla_kernel(q_ref, k_ref, v_ref, b_ref, r_ref, g_ref, s0_ref, o_ref, sf_ref, s_scr,
                *, chunk, sub, n_seqs, n_chunks, t_valid):
    n_pair = N_HEADS_B // 2
    pair_v = 2 * DV_HEAD
    chains = [(sq, p) for sq in range(n_seqs) for p in range(n_pair)]

    @pl.when(pl.program_id(1) == 0)
    def _():
        zero = jnp.zeros((DK_HEAD, DV_HEAD), F32)
        for sq, p in chains:
            s0 = s0_ref[sq, p]
            s_scr[sq, p] = jnp.concatenate([jnp.concatenate([s0[:DK_HEAD], zero], axis=1),
                                            jnp.concatenate([zero, s0[DK_HEAD:]], axis=1)], axis=0)

    n_sub = chunk // sub
    row = lax.broadcasted_iota(jnp.int32, (chunk, 1), 0)
    lane = lax.broadcasted_iota(jnp.int32, (1, LANES), 1)
    head0 = lane < DK_HEAD
    causal = (lax.broadcasted_iota(jnp.int32, (chunk, 2 * chunk), 1) % chunk
              <= lax.broadcasted_iota(jnp.int32, (chunk, 2 * chunk), 0))
    own_block = (lax.broadcasted_iota(jnp.int32, (LANES, pair_v), 0) // DK_HEAD
                 == lax.broadcasted_iota(jnp.int32, (LANES, pair_v), 1) // DV_HEAD)
    g = g_ref[...]
    keep = row < t_valid
    zero_v = jnp.zeros((chunk, DV_HEAD), BF16)

    items = [(sq, c, p) for c in range(n_chunks) for sq, p in chains]
    rows_of = lambda sq, c: slice((sq * n_chunks + c) * chunk, (sq * n_chunks + c + 1) * chunk)
    b_, q_, k_, v_ = {}, {}, {}, {}
    for sq, c, p in items:
        ls = slice(LANES * p, LANES * (p + 1))
        b_[sq, c, p] = b_ref[rows_of(sq, c), ls]
        q_[sq, c, p] = q_ref[rows_of(sq, c), ls]
        kp = k_ref[rows_of(sq, c), ls]
        k_[sq, c, p] = jnp.where(keep, kp, 0.0) if t_valid < chunk else kp
        v_[sq, c, p] = v_ref[rows_of(sq, c), pair_v * p:pair_v * (p + 1)].astype(BF16)
    b_last = {it: b_[it][chunk - 1:chunk, :] for it in items}
    q_dec = {it: (q_[it] * jnp.exp(b_[it])).astype(BF16) for it in items}
    k_hat = {it: (k_[it] * jnp.exp(b_last[it] - b_[it])).astype(BF16) for it in items}
    q_sub, k_sub = {}, {}
    for it in items:
        b = b_[it]
        for i in range(n_sub):
            r0, r1 = sub * i, sub * (i + 1)
            beta = b[r0:r0 + 1, :]
            q_sub[it, i] = (q_[it][r0:r1] * jnp.exp(b[r0:r1] - beta)).astype(BF16)
            k_i = (k_[it] * jnp.exp(jnp.where(row < r1, beta - b, 0.0))).astype(BF16)
            k_sub[it, i] = jnp.concatenate([jnp.where(head0, k_i, jnp.zeros_like(k_i)),
                                            jnp.where(head0, jnp.zeros_like(k_i), k_i)], axis=0)
    parts = {key: lax.dot_general(q_sub[key], k_sub[key], _NT, preferred_element_type=F32) for key in q_sub}
    attn = {}
    for it in items:
        a = parts[it, 0] if n_sub == 1 else jnp.concatenate([parts[it, i] for i in range(n_sub)], axis=0)
        attn[it] = jnp.where(causal, a, 0.0).astype(BF16)
    v_bd = {it: jnp.concatenate([jnp.concatenate([v_[it][:, :DV_HEAD], zero_v], axis=1),
                                 jnp.concatenate([zero_v, v_[it][:, DV_HEAD:]], axis=1)], axis=0) for it in items}
    o_intra = {it: jnp.dot(attn[it], v_bd[it], preferred_element_type=F32) for it in items}
    ds_ = {it: lax.dot_general(k_hat[it], v_[it], _TN, preferred_element_type=F32) for it in items}
    decay_col = {it: jnp.exp(jnp.broadcast_to(b_last[it], (SUBLANES, LANES)).T[:, 0:1]) for it in items}
    state = {ch: s_scr[ch] for ch in chains}
    for it in items:
        sq, c, p = it
        st = state[sq, p]
        o_pair = o_intra[it] + jnp.dot(q_dec[it], st.astype(BF16), preferred_element_type=F32)
        state[sq, p] = jnp.where(own_block, decay_col[it] * st + ds_[it], 0.0)
        for hh in range(2):
            vs = slice(DV_HEAD * (2 * p + hh), DV_HEAD * (2 * p + hh + 1))
            r = r_ref[rows_of(sq, c), vs]
            o = o_pair[:, DV_HEAD * hh:DV_HEAD * (hh + 1)]
            o_ref[rows_of(sq, c), vs] = _rms(o, g) * (r * (1.0 / (1.0 + jnp.exp(-r))))
    for ch in chains:
        s_scr[ch] = state[ch]
        sf_ref[ch] = jnp.concatenate([state[ch][:DK_HEAD, :DV_HEAD], state[ch][DK_HEAD:, DV_HEAD:]], axis=0)


def _gla(q, k, v, la, r, g, s0, *, layer, n_seq, seqs_per_step, rows_per_seq_step, chunk, sub, t_valid):
    n = q.shape[0]
    t = n // n_seq
    steps = t // rows_per_seq_step
    assert seqs_per_step == 1 or steps == 1
    rows_per_step = seqs_per_step * rows_per_seq_step
    row = lambda b, i: (b * steps + i, 0)
    st = lambda b, i: (b, 0, 0, 0)
    n_pair = N_HEADS_B // 2
    kern = functools.partial(_gla_kernel, chunk=chunk, sub=sub, n_seqs=seqs_per_step,
                             n_chunks=rows_per_seq_step // chunk, t_valid=t_valid)
    return pl.pallas_call(
        kern,
        grid=(n_seq // seqs_per_step, steps),
        in_specs=[pl.BlockSpec((rows_per_step, DK_B), row), pl.BlockSpec((rows_per_step, DK_B), row),
                  pl.BlockSpec((rows_per_step, DV_B), row), pl.BlockSpec((rows_per_step, DK_B), row),
                  pl.BlockSpec((rows_per_step, DV_B), row),
                  pl.BlockSpec((None, 1, DV_HEAD), lambda b, i: (layer, 0, 0)),
                  pl.BlockSpec((seqs_per_step, n_pair, LANES, DV_HEAD), st)],
        out_specs=(pl.BlockSpec((rows_per_step, DV_B), row),
                   pl.BlockSpec((seqs_per_step, n_pair, LANES, DV_HEAD), st)),
        out_shape=(jax.ShapeDtypeStruct((n, DV_B), F32),
                   jax.ShapeDtypeStruct((n_seq, n_pair, LANES, DV_HEAD), F32)),
        scratch_shapes=[pltpu.VMEM((seqs_per_step, n_pair, LANES, 2 * DV_HEAD), F32)],
        compiler_params=pltpu.CompilerParams(dimension_semantics=("arbitrary", "arbitrary"),
                                             vmem_limit_bytes=VMEM_LIMIT),
        name="gla",
    )(q, k, v, la, r, g, s0)


def _gelu_tanh(x):
    c = math.sqrt(2.0 / math.pi)
    return 0.5 * x * (1.0 + jnp.tanh(c * (x + 0.044715 * (x * x * x))))


def _causal_conv(u, prev, cw, cb):
    rows, width = u.shape
    tiles = rows // SUBLANES
    u3 = u.reshape(tiles, SUBLANES, width)
    p3 = prev.reshape(tiles, SUBLANES, width)
    r = lax.broadcasted_iota(jnp.int32, (tiles, SUBLANES, width), 1)
    back1 = pltpu.roll(jnp.where(r >= SUBLANES - 1, p3, u3), 1, 1).reshape(rows, width)
    back2 = pltpu.roll(jnp.where(r >= SUBLANES - 2, p3, u3), 2, 1).reshape(rows, width)
    return cb + cw[0:1, :] * back2 + cw[1:2, :] * back1 + cw[2:3, :] * u


def _mix_ffn_core(oa_ref, ob_ref, x_ref, wo_ref, gpost_ref, g1_ref, wup_ref, cw_ref, cb_ref, wdn_ref, g2_ref,
                  y_ref, prev_cols, emit_u):
    d_ff = wdn_ref.shape[0]
    m = (jnp.dot(oa_ref[...].astype(BF16), wo_ref[0:D_A, :], preferred_element_type=F32)
         + jnp.dot(ob_ref[...].astype(BF16), wo_ref[D_A:D_A + DV_B, :], preferred_element_type=F32))
    x1 = x_ref[...] + _rms(m, gpost_ref[...])
    h = _rms(x1, g1_ref[...]).astype(BF16)
    ys = []
    for j in range(d_ff // FF_CHUNK):
        halves = []
        for base in (0, d_ff):
            cols = slice(base + j * FF_CHUNK, base + (j + 1) * FF_CHUNK)
            u = jnp.dot(h, wup_ref[:, cols], preferred_element_type=F32)
            halves.append(_causal_conv(u, prev_cols(cols, u), cw_ref[:, cols], cb_ref[:, cols]))
            emit_u(cols, u)
        ys.append((_gelu_tanh(halves[0]) * halves[1]).astype(BF16))
    f = jnp.dot(jnp.concatenate(ys, axis=1), wdn_ref[...], preferred_element_type=F32)
    y_ref[...] = x1 + _rms(f, g2_ref[...])


def _mix_ffn_prompt_kernel(*refs, tiles_per_seq):
    in_refs, (y_ref, tail_ref, carry) = refs[:11], refs[11:]
    tm = y_ref.shape[0]

    @pl.when(pl.program_id(0) % tiles_per_seq == 0)
    def _():
        carry[...] = jnp.zeros_like(carry)

    def prev_cols(cols, u):
        return jnp.concatenate([carry[:, cols], u[:tm - SUBLANES]], axis=0)

    def emit_u(cols, u):
        tail = u[tm - SUBLANES:]
        carry[:, cols] = tail
        tail_ref[:, cols] = tail

    _mix_ffn_core(*in_refs, y_ref, prev_cols, emit_u)


def _mix_ffn_sample_kernel(*refs):
    in_refs, (prev_ref, y_ref, u_ref) = refs[:11], refs[11:]

    def emit_u(cols, u):
        u_ref[:, cols] = u

    _mix_ffn_core(*in_refs, y_ref, lambda cols, u: prev_ref[:, cols], emit_u)


def _mix_ffn(oa, ob, x, wo, gpost, g1, wup, cw, cb, wdn, g2, *, layer, tm, n_seq=1, prev=None):
    n, d = x.shape
    width = wup.shape[-1]
    row = lambda i: (i, 0)
    lay3 = lambda i: (layer, 0, 0)
    single = pl.Buffered(1)
    weight = lambda a: pl.BlockSpec((None,) + a.shape[1:], lay3, pipeline_mode=single)
    small = lambda a: pl.BlockSpec((None,) + a.shape[1:], lay3)
    in_specs = [pl.BlockSpec((tm, D_A), row), pl.BlockSpec((tm, DV_B), row), pl.BlockSpec((tm, d), row),
                weight(wo), small(gpost), small(g1), weight(wup), small(cw), small(cb), weight(wdn), small(g2)]
    args = [oa, ob, x, wo, gpost, g1, wup, cw, cb, wdn, g2]
    if prev is None:
        tiles_per_seq = n // n_seq // tm
        kern = functools.partial(_mix_ffn_prompt_kernel, tiles_per_seq=tiles_per_seq)
        out_specs = (pl.BlockSpec((tm, d), row),
                     pl.BlockSpec((None, SUBLANES, width), lambda i: (i // tiles_per_seq, 0, 0)))
        out_shape = (jax.ShapeDtypeStruct((n, d), F32), jax.ShapeDtypeStruct((n_seq, SUBLANES, width), F32))
        scratch = [pltpu.VMEM((SUBLANES, width), F32)]
    else:
        kern = _mix_ffn_sample_kernel
        in_specs.append(pl.BlockSpec((None, tm, width), lambda i: (layer, i, 0)))
        args.append(prev)
        out_specs = (pl.BlockSpec((tm, d), row), pl.BlockSpec((tm, width), row))
        out_shape = (jax.ShapeDtypeStruct((n, d), F32), jax.ShapeDtypeStruct((n, width), F32))
        scratch = []
    return pl.pallas_call(
        kern,
        grid=(n // tm,),
        in_specs=in_specs,
        out_specs=out_specs,
        out_shape=out_shape,
        scratch_shapes=scratch,
        compiler_params=pltpu.CompilerParams(dimension_semantics=("arbitrary",),
                                             vmem_limit_bytes=VMEM_LIMIT),
        name="mix_ffn",
    )(*args)


def _rope_tables(pos):
    inv_freq = ROPE_THETA ** (-jnp.arange(ROT_HALF, dtype=F32) / ROT_HALF)
    ang = pos.astype(F32)[:, None] * inv_freq[None, :]
    cos, sin = jnp.cos(ang), jnp.sin(ang)
    n = pos.shape[0]
    rest = HEAD_DIM_A - 2 * ROT_HALF
    reps = LANES // HEAD_DIM_A
    c = jnp.tile(jnp.concatenate([cos, cos, jnp.ones((n, rest), F32)], axis=1), (1, reps))
    s_up = jnp.tile(jnp.concatenate([jnp.zeros((n, ROT_HALF), F32), sin, jnp.zeros((n, rest), F32)], axis=1),
                    (1, reps))
    s_dn = jnp.tile(jnp.concatenate([-sin, jnp.zeros((n, ROT_HALF + rest), F32)], axis=1), (1, reps))
    return c, s_up, s_dn


def kernel(x_prompt, x_sample, cache_k_win, cache_v_win, state_gla, state_ffn_conv, g_mix_pre, g_mix_post,
           g_ffn_pre, g_ffn_post, w_in, w_gate2, b_gate, g_gla, w_out, w_up, conv_w, conv_b, w_down):
    depth = w_in.shape[0]
    b, t, d = x_prompt.shape
    bs, ts, _ = x_sample.shape
    win = cache_k_win.shape[2]
    width = w_up.shape[-1]
    conv_taps = conv_w.shape[1]
    n_pair = N_HEADS_B // 2
    win_p = min(win, t)
    n_s = bs * SAMPLE_PAD

    w_in_bf = jnp.pad(w_in, ((0, 0), (0, 0), (0, N_MAIN + LANES - w_in.shape[-1]))).astype(BF16)
    wg2_bf = jnp.pad(w_gate2, ((0, 0), (0, LANES - GATE_RANK), (0, 0))).astype(BF16)
    w_out_bf, w_up_bf, w_dn_bf = w_out.astype(BF16), w_up.astype(BF16), w_down.astype(BF16)
    per_layer_row = lambda a: a[:, None, :]
    bg, gg, cb = per_layer_row(b_gate), per_layer_row(g_gla), per_layer_row(conv_b)
    g_pre, g_post = per_layer_row(g_mix_pre), per_layer_row(g_mix_post)
    g_f1, g_f2 = per_layer_row(g_ffn_pre), per_layer_row(g_ffn_post)

    tabs_p = _rope_tables(jnp.arange(t, dtype=jnp.int32))
    pos_s = PAST_LEN + jnp.arange(SAMPLE_PAD, dtype=jnp.int32)
    tabs_s = tuple(jnp.tile(a, (bs, 1)) for a in _rope_tables(pos_s))

    xp = x_prompt.reshape(b * t, d)
    xs = jnp.pad(x_sample, ((0, 0), (0, SAMPLE_PAD - ts), (0, 0))).reshape(n_s, d)

    cache_kt = jnp.transpose(cache_k_win, (0, 1, 3, 4, 2))
    cache_vt = jnp.transpose(cache_v_win, (0, 1, 3, 4, 2))
    prev_s = jnp.pad(state_ffn_conv, ((0, 0), (0, 0), (SUBLANES - (conv_taps - 1), 0), (0, 0)))
    prev_s = prev_s.reshape(depth, n_s, width)
    s0_p = jnp.zeros((b, n_pair, LANES, DV_HEAD), F32)
    s0_s = state_gla.reshape(depth, bs, n_pair, LANES, DV_HEAD)

    outs_p = {"s": [], "c": []}
    outs_s = {"k": [], "v": [], "s": [], "c": []}
    win_bufs = None
    for l in range(depth):
        *sorted_qkv, qb, kb, vb, rb, la, kt_win, vt_win = _inproj(
            xp, g_pre, w_in_bf, wg2_bf, bg, tabs_p, layer=l, tm=INPROJ_ROWS, gla_chunk=GLA_CHUNK,
            gla_valid=GLA_CHUNK, n_seq=b, win_rows=win_p, win_prev=win_bufs)
        win_bufs = (kt_win, vt_win)
        q_s, k_s, v_s = sorted_qkv[0:3], sorted_qkv[3:6], sorted_qkv[6:9]
        oa = _attn_prompt(q_s + k_s + v_s)
        ob, s_fin = _gla(qb, kb, vb, la, rb, gg, s0_p, layer=l, n_seq=b, seqs_per_step=1,
                         rows_per_seq_step=GLA_ROWS, chunk=GLA_CHUNK, sub=GLA_SUB, t_valid=GLA_CHUNK)
        xp, tail = _mix_ffn(oa.reshape(b * t, D_A), ob, xp, w_out_bf, g_post, g_f1, w_up_bf, conv_w, cb, w_dn_bf,
                            g_f2, layer=l, tm=FFN_ROWS, n_seq=b)
        outs_p["s"].append(s_fin.reshape(b, N_HEADS_B, DK_HEAD, DV_HEAD))
        outs_p["c"].append(tail[:, SUBLANES - (conv_taps - 1):])

        qa, ka, va, qb, kb, vb, rb, la = _inproj(xs, g_pre, w_in_bf, wg2_bf, bg, tabs_s, layer=l, tm=n_s,
                                                 gla_chunk=SAMPLE_PAD, gla_valid=ts)
        ka3 = ka.reshape(bs, SAMPLE_PAD, D_A)
        va3 = va.reshape(bs, SAMPLE_PAD, D_A)
        oa = _attn_sample(qa.reshape(bs, SAMPLE_PAD, D_A), cache_kt, cache_vt, ka3, va3, layer=l)
        ob, s_fin = _gla(qb, kb, vb, la, rb, gg, s0_s[l], layer=l, n_seq=bs, seqs_per_step=math.gcd(bs, GLA_SAMPLE_SEQS),
                         rows_per_seq_step=SAMPLE_PAD, chunk=SAMPLE_PAD, sub=SAMPLE_PAD, t_valid=ts)
        xs, u_all = _mix_ffn(oa.reshape(n_s, D_A), ob, xs, w_out_bf, g_post, g_f1, w_up_bf, conv_w, cb, w_dn_bf,
                             g_f2, layer=l, tm=n_s, prev=prev_s)
        outs_s["k"].append(ka3[:, :ts].reshape(bs, ts, N_HEADS_A, HEAD_DIM_A))
        outs_s["v"].append(va3[:, :ts].reshape(bs, ts, N_HEADS_A, HEAD_DIM_A))
        outs_s["s"].append(s_fin.reshape(bs, N_HEADS_B, DK_HEAD, DV_HEAD))
        outs_s["c"].append(u_all.reshape(bs, SAMPLE_PAD, width)[:, ts - (conv_taps - 1):ts])

    y_prompt = xp.reshape(b, t, d)
    y_sample = xs.reshape(bs, SAMPLE_PAD, d)[:, :ts]
    st = lambda xs_: jnp.stack(xs_)
    to_rows = lambda a: jnp.transpose(a.reshape(depth, b, N_HEADS_A, HEAD_DIM_A, win_p), (0, 1, 4, 2, 3))
    return (y_prompt, y_sample, to_rows(win_bufs[0]), to_rows(win_bufs[1]), st(outs_p["s"]), st(outs_p["c"]),
            st(outs_s["k"]), st(outs_s["v"]), st(outs_s["s"]), st(outs_s["c"]))
```

```python
import functools
import math

import jax
import jax.numpy as jnp
from jax import lax
from jax.experimental import pallas as pl
from jax.experimental.pallas import tpu as pltpu

F32 = jnp.float32
BF16 = jnp.bfloat16

PAST_LEN = 16384
N_HEADS_A = 8
HEAD_DIM_A = 64
D_A = N_HEADS_A * HEAD_DIM_A
ROT_HALF = HEAD_DIM_A // 8
ROPE_THETA = 500000.0
WIN_DENSE, WIN_MID, WIN_FAR = 128, 512, 2048
DILATIONS = (1, 4, 16)
BAND = 128
N_HEADS_B = 4
DK_HEAD = 64
DV_HEAD = 128
DK_B = N_HEADS_B * DK_HEAD
DV_B = N_HEADS_B * DV_HEAD
N_MAIN = 3 * D_A + 2 * DK_B + 2 * DV_B
GATE_RANK = 16
GATE_TAU = 16.0
GLA_CHUNK = 64
GLA_SUB = 16
EPS = 1e-6
NEG_BIG = -1e30

LANES = 128
SUBLANES = 8
VMEM_LIMIT = 56 * 1024 * 1024
SAMPLE_PAD = SUBLANES
FF_CHUNK = 2 * LANES
INPROJ_ROWS = 256
GLA_ROWS = 1024
GLA_SAMPLE_SEQS = 8
FFN_ROWS = 512
MERGE_ROWS = 256
ATT_UNROLL = 8

_NT = (((1,), (1,)), ((), ()))
_TN = (((0,), (0,)), ((), ()))


def _rms(x, g):
    ms = jnp.mean(x * x, axis=-1, keepdims=True)
    return x * lax.rsqrt(ms + EPS) * g


def _split3(x):
    hi = x.astype(BF16)
    r1 = x - hi.astype(F32)
    mid = r1.astype(BF16)
    lo = (r1 - mid.astype(F32)).astype(BF16)
    return hi, mid, lo


def _dilation_multiplicity(d):
    nonneg = d >= 0
    m0 = nonneg & (d <= WIN_DENSE)
    m1 = nonneg & (d <= WIN_MID) & ((d & 3) == 0)
    m2 = nonneg & (d <= WIN_FAR) & ((d & 15) == 0)
    return m0.astype(F32) + m1.astype(F32) + m2.astype(F32)


def _inproj_kernel(x_ref, g_ref, w_ref, wg1_ref, wg2_ref, bg_ref, c_ref, su_ref, sd_ref, *refs,
                   prompt, layer, n_alias, gla_chunk, gla_valid):
    refs = refs[n_alias:]
    if prompt:
        sorted_refs, rest = refs[:9], refs[9:]
        qb_ref, kb_ref, vb_ref, rb_ref, la_ref, kt_ref, vt_ref, scr = rest
    else:
        qa_ref, ka_ref, va_ref, qb_ref, kb_ref, vb_ref, rb_ref, la_ref = refs
    h = _rms(x_ref[...], g_ref[...]).astype(BF16)
    tm = h.shape[0]

    def proj(lo, hi):
        return jnp.dot(h, w_ref[:, lo:hi], preferred_element_type=F32)

    cos = c_ref[...]
    s_up = su_ref[...]
    s_dn = sd_ref[...]

    def rope(y):
        outs = []
        for j in range(D_A // LANES):
            yj = y[:, LANES * j:LANES * (j + 1)]
            outs.append(yj * cos + pltpu.roll(yj, ROT_HALF, 1) * s_up
                        + pltpu.roll(yj, LANES - ROT_HALF, 1) * s_dn)
        return jnp.concatenate(outs, axis=1)

    gate_lr = jnp.dot(h, wg1_ref[...], preferred_element_type=F32)
    z = jnp.dot(gate_lr.astype(BF16), wg2_ref[...], preferred_element_type=F32) + bg_ref[...]
    log_sig = jnp.minimum(z, 0.0) - jnp.log1p(jnp.exp(-jnp.abs(z)))
    log_a = log_sig * (1.0 / GATE_TAU)
    r = lax.broadcasted_iota(jnp.int32, (tm, tm), 0)
    c = lax.broadcasted_iota(jnp.int32, (tm, tm), 1)
    tri = ((c <= r) & (c // gla_chunk == r // gla_chunk) & (c % gla_chunk < gla_valid)).astype(BF16)
    la_ref[...] = sum(jnp.dot(tri, piece, preferred_element_type=F32) for piece in _split3(log_a))

    qkv = (rope(proj(0, D_A)) * (HEAD_DIM_A ** -0.5), rope(proj(D_A, 2 * D_A)), proj(2 * D_A, 3 * D_A))
    if prompt:
        for a, val in enumerate(qkv):
            sorted_refs[3 * a][0] = val.astype(BF16)
            for c in range(D_A // LANES):
                cl = slice(LANES * c, LANES * (c + 1))
                scr[a, c] = val[:, cl]
                for o_ref, s in zip(sorted_refs[3 * a + 1:3 * a + 3], DILATIONS[1:]):
                    for r in range(s):
                        o_ref[r, :, cl] = scr.at[a, c][pl.ds(r, tm // s, stride=s), :].astype(BF16)
    else:
        qa_ref[...], ka_ref[...], va_ref[...] = qkv
    o = 3 * D_A
    qb_ref[...] = proj(o, o + DK_B) * (DK_HEAD ** -0.5); o += DK_B
    kb_ref[...] = proj(o, o + DK_B); o += DK_B
    vb_ref[...] = proj(o, o + DV_B); o += DV_B
    rb_ref[...] = proj(o, o + DV_B)
    if prompt:
        for c in range(D_A // LANES):
            cr = slice(LANES * c, LANES * (c + 1))
            for ref, a in ((kt_ref, 1), (vt_ref, 2)):
                if n_alias:
                    ref[cr, :] = scr[a, c].T
                else:
                    for dl in range(ref.shape[0]):
                        ref[dl, cr, :] = scr[a, c].T if dl == layer else jnp.zeros((LANES, tm), F32)


def _inproj(x, g, w, wg1, wg2, bg, tabs, *, layer, tm, gla_chunk, gla_valid, n_seq=1, win_rows=0, win_prev=None):
    n, d = x.shape
    prompt = win_rows > 0
    t = n // n_seq
    tiles_per_seq = t // tm
    first_win_tile = tiles_per_seq - win_rows // tm
    assert win_rows % tm == 0 and tm % DILATIONS[-1] == 0
    n_tab = tabs[0].shape[0] // tm
    row = lambda i: (i, 0)
    tab = lambda i: (i % n_tab, 0)
    lay3 = lambda i: (layer, 0, 0)
    f32_rows = lambda w_: (jax.ShapeDtypeStruct((n, w_), F32), pl.BlockSpec((tm, w_), row))
    outs = []
    if prompt:
        tile4 = lambda i: (i // tiles_per_seq, 0, i % tiles_per_seq, 0)
        for _ in range(3):
            for s in DILATIONS:
                outs.append((jax.ShapeDtypeStruct((n_seq, s, t // s, D_A), BF16),
                             pl.BlockSpec((None, s, tm // s, D_A), tile4)))
    else:
        outs += [f32_rows(D_A)] * 3
    outs += [f32_rows(DK_B), f32_rows(DK_B), f32_rows(DV_B), f32_rows(DV_B), f32_rows(DK_B)]
    scratch = []
    if prompt:
        depth = w.shape[0]
        win_col = lambda i: jnp.maximum(i % tiles_per_seq - first_win_tile, 0)
        if win_prev is None:
            win_spec = pl.BlockSpec((depth, None, D_A, tm), lambda i: (0, i // tiles_per_seq, 0, win_col(i)))
        else:
            win_spec = pl.BlockSpec((None, None, D_A, tm), lambda i: (layer, i // tiles_per_seq, 0, win_col(i)))
        outs += [(jax.ShapeDtypeStruct((depth, n_seq, D_A, win_rows), F32), win_spec)] * 2
        scratch = [pltpu.VMEM((3, D_A // LANES, tm, LANES), F32)]
    alias_in = list(win_prev) if win_prev is not None else []
    n_in = 9
    aliases = {n_in + a: len(outs) - len(alias_in) + a for a in range(len(alias_in))}
    kern = functools.partial(_inproj_kernel, prompt=prompt, layer=layer, n_alias=len(alias_in),
                             gla_chunk=gla_chunk, gla_valid=gla_valid)
    return pl.pallas_call(
        kern,
        grid=(n // tm,),
        in_specs=[pl.BlockSpec((tm, d), row), pl.BlockSpec((None, 1, d), lay3),
                  pl.BlockSpec((None,) + w.shape[1:], lay3), pl.BlockSpec((None,) + wg1.shape[1:], lay3),
                  pl.BlockSpec((None,) + wg2.shape[1:], lay3),
                  pl.BlockSpec((None, 1, DK_B), lay3),
                  pl.BlockSpec((tm, LANES), tab), pl.BlockSpec((tm, LANES), tab), pl.BlockSpec((tm, LANES), tab)]
        + [pl.BlockSpec(memory_space=pl.ANY)] * len(alias_in),
        out_specs=tuple(spec for _, spec in outs),
        out_shape=tuple(shape for shape, _ in outs),
        input_output_aliases=aliases,
        scratch_shapes=scratch,
        compiler_params=pltpu.CompilerParams(dimension_semantics=("arbitrary",),
                                             vmem_limit_bytes=VMEM_LIMIT),
        name="inproj",
    )(x, g, w, wg1, wg2, bg, *tabs, *alias_in)


def _attn_prompt_kernel(*refs, t):
    sorted_refs, (bias_ref, o_ref, on, ln) = refs[:9], refs[9:]
    blk = BAND
    lane = lax.broadcasted_iota(jnp.int32, (1, LANES), 1)
    head0 = lane < HEAD_DIM_A
    head_sel = (head0, jnp.logical_not(head0))

    def band_blocks(pi, s, first_idx):
        q_ref, k_ref, v_ref = sorted_refs[pi], sorted_refs[3 + pi], sorted_refs[6 + pi]
        n_blk = t // s // blk
        blocks = []
        for u in range(ATT_UNROLL):
            idx = first_idx + u
            r = idx // n_blk
            j = idx % n_blk
            q0 = pl.multiple_of(j * blk, blk)
            w0 = pl.multiple_of(q0 - jnp.minimum(j, 1) * blk, blk)
            rows = pl.ds(q0, blk) if s == 1 else pl.ds(r + s * q0, blk, stride=s)
            blocks.append((q_ref[r, pl.ds(q0, blk), :], k_ref[r, pl.ds(w0, 2 * blk), :],
                           v_ref[r, pl.ds(w0, 2 * blk), :], bias_ref[jnp.minimum(j, 1)], rows))
        heads = [(u, sel) for u in range(ATT_UNROLL) for sel in head_sel]
        sc = [lax.dot_general(jnp.where(sel, blocks[u][0], jnp.zeros_like(blocks[u][0])), blocks[u][1], _NT,
                              preferred_element_type=F32) + blocks[u][3] for u, sel in heads]
        m = [jnp.max(x, axis=1, keepdims=True) for x in sc]
        p = [jnp.exp(x - mx) for x, mx in zip(sc, m)]
        l = [jnp.sum(x, axis=1, keepdims=True) for x in p]
        o = [jnp.dot(x.astype(BF16), blocks[u][2], preferred_element_type=F32) for x, (u, _) in zip(p, heads)]
        for u in range(ATT_UNROLL):
            rows = blocks[u][4]
            l_pair = jnp.where(head0, l[2 * u], l[2 * u + 1])
            on.at[pi][rows, :] = jnp.where(head0, o[2 * u], o[2 * u + 1]) / l_pair
            ln.at[pi][rows, :] = jnp.where(head0, m[2 * u], m[2 * u + 1]) + jnp.log(l_pair)

    for pi, s in enumerate(DILATIONS):
        def one_group(gi, c, pi=pi, s=s):
            band_blocks(pi, s, gi * ATT_UNROLL)
            return c

        lax.fori_loop(0, t // blk // ATT_UNROLL, one_group, 0)

    def merge(i, c):
        rows = pl.ds(pl.multiple_of(i * MERGE_ROWS, MERGE_ROWS), MERGE_ROWS)
        lse = [ln[pi, rows, :] for pi in range(len(DILATIONS))]
        top = jnp.maximum(jnp.maximum(lse[0], lse[1]), lse[2])
        w = [jnp.exp(x - top) for x in lse]
        num = w[0] * on[0, rows, :] + w[1] * on[1, rows, :] + w[2] * on[2, rows, :]
        o_ref[rows, :] = num / (w[0] + w[1] + w[2])
        return c

    lax.fori_loop(0, t // MERGE_ROWS, merge, 0)


def _band_bias():
    iq = jnp.arange(BAND, dtype=jnp.int32)[:, None]
    c = jnp.arange(2 * BAND, dtype=jnp.int32)[None, :]
    d = jnp.stack([iq - c, iq + BAND - c])
    return jnp.where((d >= 0) & (d <= BAND), 0.0, NEG_BIG).astype(F32)


def _attn_prompt(sorted_qkv):
    b, _, t, _ = sorted_qkv[0].shape
    n_pair = D_A // LANES
    n_pat = len(DILATIONS)
    bias = _band_bias()
    pair4 = lambda bi, p: (bi, 0, 0, p)
    in_specs = [pl.BlockSpec((None,) + a.shape[1:3] + (LANES,), pair4) for a in sorted_qkv]
    in_specs.append(pl.BlockSpec(bias.shape, lambda bi, p: (0, 0, 0)))
    return pl.pallas_call(
        functools.partial(_attn_prompt_kernel, t=t),
        grid=(b, n_pair),
        in_specs=in_specs,
        out_specs=pl.BlockSpec((None, t, LANES), lambda bi, p: (bi, 0, p)),
        out_shape=jax.ShapeDtypeStruct((b, t, D_A), F32),
        scratch_shapes=[pltpu.VMEM((n_pat, t, LANES), F32)] * 2,
        compiler_params=pltpu.CompilerParams(dimension_semantics=("arbitrary", "arbitrary"),
                                             vmem_limit_bytes=VMEM_LIMIT),
        name="attn_prompt",
    )(*sorted_qkv, bias)


def _attn_sample_kernel(q_ref, kt_ref, vt_ref, kn_ref, vn_ref, o_ref, *, win):
    q = q_ref[...].astype(BF16)
    pad = jnp.zeros((LANES - SAMPLE_PAD, D_A), F32)
    kn_t = jnp.concatenate([kn_ref[...], pad], axis=0).T.astype(BF16)
    vn_t = jnp.concatenate([vn_ref[...], pad], axis=0).T.astype(BF16)
    n_keys = win + LANES
    tok = lax.broadcasted_iota(jnp.int32, (SAMPLE_PAD, n_keys), 0)
    key = lax.broadcasted_iota(jnp.int32, (SAMPLE_PAD, n_keys), 1)
    mult = _dilation_multiplicity(win + tok - key)
    live = mult > 0.0
    hs = [slice(HEAD_DIM_A * h, HEAD_DIM_A * (h + 1)) for h in range(N_HEADS_A)]
    kts = [jnp.concatenate([kt_ref[h].astype(BF16), kn_t[hs[h], :]], axis=1) for h in range(N_HEADS_A)]
    ss = [jnp.where(live, jnp.dot(q[:, hs[h]], kts[h], preferred_element_type=F32), NEG_BIG)
          for h in range(N_HEADS_A)]
    ps = []
    for s in ss:
        m = jnp.max(s, axis=1, keepdims=True)
        p = jnp.exp(s - m) * mult
        ps.append((p / jnp.sum(p, axis=1, keepdims=True)).astype(BF16))
    vts = [jnp.concatenate([vt_ref[h].astype(BF16), vn_t[hs[h], :]], axis=1) for h in range(N_HEADS_A)]
    head_cols = [lax.dot_general(vts[h], ps[h], _NT, preferred_element_type=F32) for h in range(N_HEADS_A)]
    o_ref[...] = jnp.concatenate(head_cols, axis=0).T


def _attn_sample(q, kt, vt, kn, vn, *, layer):
    bs = q.shape[0]
    win = kt.shape[-1]
    seq = lambda i: (i, 0, 0)
    slab = lambda i: (layer, i, 0, 0, 0)
    return pl.pallas_call(
        functools.partial(_attn_sample_kernel, win=win),
        grid=(bs,),
        in_specs=[pl.BlockSpec((None, SAMPLE_PAD, D_A), seq),
                  pl.BlockSpec((None, None, N_HEADS_A, HEAD_DIM_A, win), slab),
                  pl.BlockSpec((None, None, N_HEADS_A, HEAD_DIM_A, win), slab),
                  pl.BlockSpec((None, SAMPLE_PAD, D_A), seq), pl.BlockSpec((None, SAMPLE_PAD, D_A), seq)],
        out_specs=pl.BlockSpec((None, SAMPLE_PAD, D_A), seq),
        out_shape=jax.ShapeDtypeStruct((bs, SAMPLE_PAD, D_A), F32),
        compiler_params=pltpu.CompilerParams(dimension_semantics=("arbitrary",),
                                             vmem_limit_bytes=VMEM_LIMIT),
        name="attn_sample",
    )(q, kt, vt, kn, vn)


def _gla_kernel(q_ref, k_ref, v_ref, b_ref, r_ref, g_ref, s0_ref, o_ref, sf_ref, s_scr,
                *, chunk, sub, n_seqs, n_chunks, t_valid):
    n_pair = N_HEADS_B // 2
    pair_v = 2 * DV_HEAD
    chains = [(sq, p) for sq in range(n_seqs) for p in range(n_pair)]

    @pl.when(pl.program_id(1) == 0)
    def _():
        zero = jnp.zeros((DK_HEAD, DV_HEAD), F32)
        for sq, p in chains:
            s0 = s0_ref[sq, p]
            s_scr[sq, p] = jnp.concatenate([jnp.concatenate([s0[:DK_HEAD], zero], axis=1),
                                            jnp.concatenate([zero, s0[DK_HEAD:]], axis=1)], axis=0)

    n_sub = chunk // sub
    row = lax.broadcasted_iota(jnp.int32, (chunk, 1), 0)
    lane = lax.broadcasted_iota(jnp.int32, (1, LANES), 1)
    head0 = lane < DK_HEAD
    causal = (lax.broadcasted_iota(jnp.int32, (chunk, 2 * chunk), 1) % chunk
              <= lax.broadcasted_iota(jnp.int32, (chunk, 2 * chunk), 0))
    own_block = (lax.broadcasted_iota(jnp.int32, (LANES, pair_v), 0) // DK_HEAD
                 == lax.broadcasted_iota(jnp.int32, (LANES, pair_v), 1) // DV_HEAD)
    g = g_ref[...]
    keep = row < t_valid
    zero_v = jnp.zeros((chunk, DV_HEAD), BF16)

    items = [(sq, c, p) for c in range(n_chunks) for sq, p in chains]
    rows_of = lambda sq, c: slice((sq * n_chunks + c) * chunk, (sq * n_chunks + c + 1) * chunk)
    b_, q_, k_, v_ = {}, {}, {}, {}
    for sq, c, p in items:
        ls = slice(LANES * p, LANES * (p + 1))
        b_[sq, c, p] = b_ref[rows_of(sq, c), ls]
        q_[sq, c, p] = q_ref[rows_of(sq, c), ls]
        kp = k_ref[rows_of(sq, c), ls]
        k_[sq, c, p] = jnp.where(keep, kp, 0.0) if t_valid < chunk else kp
        v_[sq, c, p] = v_ref[rows_of(sq, c), pair_v * p:pair_v * (p + 1)].astype(BF16)
    b_last = {it: b_[it][chunk - 1:chunk, :] for it in items}
    q_dec = {it: (q_[it] * jnp.exp(b_[it])).astype(BF16) for it in items}
    k_hat = {it: (k_[it] * jnp.exp(b_last[it] - b_[it])).astype(BF16) for it in items}
    q_sub, k_sub = {}, {}
    for it in items:
        b = b_[it]
        for i in range(n_sub):
            r0, r1 = sub * i, sub * (i + 1)
            beta = b[r0:r0 + 1, :]
            q_sub[it, i] = (q_[it][r0:r1] * jnp.exp(b[r0:r1] - beta)).astype(BF16)
            k_i = (k_[it] * jnp.exp(jnp.where(row < r1, beta - b, 0.0))).astype(BF16)
            k_sub[it, i] = jnp.concatenate([jnp.where(head0, k_i, jnp.zeros_like(k_i)),
                                            jnp.where(head0, jnp.zeros_like(k_i), k_i)], axis=0)
    parts = {key: lax.dot_general(q_sub[key], k_sub[key], _NT, preferred_element_type=F32) for key in q_sub}
    attn = {}
    for it in items:
        a = parts[it, 0] if n_sub == 1 else jnp.concatenate([parts[it, i] for i in range(n_sub)], axis=0)
        attn[it] = jnp.where(causal, a, 0.0).astype(BF16)
    v_bd = {it: jnp.concatenate([jnp.concatenate([v_[it][:, :DV_HEAD], zero_v], axis=1),
                                 jnp.concatenate([zero_v, v_[it][:, DV_HEAD:]], axis=1)], axis=0) for it in items}
    o_intra = {it: jnp.dot(attn[it], v_bd[it], preferred_element_type=F32) for it in items}
    ds_ = {it: lax.dot_general(k_hat[it], v_[it], _TN, preferred_element_type=F32) for it in items}
    decay_col = {it: jnp.exp(jnp.broadcast_to(b_last[it], (SUBLANES, LANES)).T[:, 0:1]) for it in items}
    state = {ch: s_scr[ch] for ch in chains}
    for it in items:
        sq, c, p = it
        st = state[sq, p]
        o_pair = o_intra[it] + jnp.dot(q_dec[it], st.astype(BF16), preferred_element_type=F32)
        state[sq, p] = jnp.where(own_block, decay_col[it] * st + ds_[it], 0.0)
        for hh in range(2):
            vs = slice(DV_HEAD * (2 * p + hh), DV_HEAD * (2 * p + hh + 1))
            r = r_ref[rows_of(sq, c), vs]
            o = o_pair[:, DV_HEAD * hh:DV_HEAD * (hh + 1)]
            o_ref[rows_of(sq, c), vs] = _rms(o, g) * (r * (1.0 / (1.0 + jnp.exp(-r))))
    for ch in chains:
        s_scr[ch] = state[ch]
        sf_ref[ch] = jnp.concatenate([state[ch][:DK_HEAD, :DV_HEAD], state[ch][DK_HEAD:, DV_HEAD:]], axis=0)


def _gla(q, k, v, la, r, g, s0, *, layer, n_seq, seqs_per_step, rows_per_seq_step, chunk, sub, t_valid):
    n = q.shape[0]
    t = n // n_seq
    steps = t // rows_per_seq_step
    assert seqs_per_step == 1 or steps == 1
    rows_per_step = seqs_per_step * rows_per_seq_step
    row = lambda b, i: (b * steps + i, 0)
    st = lambda b, i: (b, 0, 0, 0)
    n_pair = N_HEADS_B // 2
    kern = functools.partial(_gla_kernel, chunk=chunk, sub=sub, n_seqs=seqs_per_step,
                             n_chunks=rows_per_seq_step // chunk, t_valid=t_valid)
    return pl.pallas_call(
        kern,
        grid=(n_seq // seqs_per_step, steps),
        in_specs=[pl.BlockSpec((rows_per_step, DK_B), row), pl.BlockSpec((rows_per_step, DK_B), row),
                  pl.BlockSpec((rows_per_step, DV_B), row), pl.BlockSpec((rows_per_step, DK_B), row),
                  pl.BlockSpec((rows_per_step, DV_B), row),
                  pl.BlockSpec((None, 1, DV_HEAD), lambda b, i: (layer, 0, 0)),
                  pl.BlockSpec((seqs_per_step, n_pair, LANES, DV_HEAD), st)],
        out_specs=(pl.BlockSpec((rows_per_step, DV_B), row),
                   pl.BlockSpec((seqs_per_step, n_pair, LANES, DV_HEAD), st)),
        out_shape=(jax.ShapeDtypeStruct((n, DV_B), F32),
                   jax.ShapeDtypeStruct((n_seq, n_pair, LANES, DV_HEAD), F32)),
        scratch_shapes=[pltpu.VMEM((seqs_per_step, n_pair, LANES, 2 * DV_HEAD), F32)],
        compiler_params=pltpu.CompilerParams(dimension_semantics=("arbitrary", "arbitrary"),
                                             vmem_limit_bytes=VMEM_LIMIT),
        name="gla",
    )(q, k, v, la, r, g, s0)


def _gelu_tanh(x):
    c = math.sqrt(2.0 / math.pi)
    return 0.5 * x * (1.0 + jnp.tanh(c * (x + 0.044715 * (x * x * x))))


def _causal_conv(u, prev, cw, cb):
    rows, width = u.shape
    tiles = rows // SUBLANES
    u3 = u.reshape(tiles, SUBLANES, width)
    p3 = prev.reshape(tiles, SUBLANES, width)
    r = lax.broadcasted_iota(jnp.int32, (tiles, SUBLANES, width), 1)
    back1 = pltpu.roll(jnp.where(r >= SUBLANES - 1, p3, u3), 1, 1).reshape(rows, width)
    back2 = pltpu.roll(jnp.where(r >= SUBLANES - 2, p3, u3), 2, 1).reshape(rows, width)
    return cb + cw[0:1, :] * back2 + cw[1:2, :] * back1 + cw[2:3, :] * u


def _mix_ffn_core(oa_ref, ob_ref, x_ref, wo_ref, gpost_ref, g1_ref, wup_ref, cw_ref, cb_ref, wdn_ref, g2_ref,
                  y_ref, prev_cols, emit_u):
    d_ff = wdn_ref.shape[0]
    m = (jnp.dot(oa_ref[...].astype(BF16), wo_ref[0:D_A, :], preferred_element_type=F32)
         + jnp.dot(ob_ref[...].astype(BF16), wo_ref[D_A:D_A + DV_B, :], preferred_element_type=F32))
    x1 = x_ref[...] + _rms(m, gpost_ref[...])
    h = _rms(x1, g1_ref[...]).astype(BF16)
    ys = []
    for j in range(d_ff // FF_CHUNK):
        halves = []
        for base in (0, d_ff):
            cols = slice(base + j * FF_CHUNK, base + (j + 1) * FF_CHUNK)
            u = jnp.dot(h, wup_ref[:, cols], preferred_element_type=F32)
            halves.append(_causal_conv(u, prev_cols(cols, u), cw_ref[:, cols], cb_ref[:, cols]))
            emit_u(cols, u)
        ys.append((_gelu_tanh(halves[0]) * halves[1]).astype(BF16))
    f = jnp.dot(jnp.concatenate(ys, axis=1), wdn_ref[...], preferred_element_type=F32)
    y_ref[...] = x1 + _rms(f, g2_ref[...])


def _mix_ffn_prompt_kernel(*refs, tiles_per_seq):
    in_refs, (y_ref, tail_ref, carry) = refs[:11], refs[11:]
    tm = y_ref.shape[0]

    @pl.when(pl.program_id(0) % tiles_per_seq == 0)
    def _():
        carry[...] = jnp.zeros_like(carry)

    def prev_cols(cols, u):
        return jnp.concatenate([carry[:, cols], u[:tm - SUBLANES]], axis=0)

    def emit_u(cols, u):
        tail = u[tm - SUBLANES:]
        carry[:, cols] = tail
        tail_ref[:, cols] = tail

    _mix_ffn_core(*in_refs, y_ref, prev_cols, emit_u)


def _mix_ffn_sample_kernel(*refs):
    in_refs, (prev_ref, y_ref, u_ref) = refs[:11], refs[11:]

    def emit_u(cols, u):
        u_ref[:, cols] = u

    _mix_ffn_core(*in_refs, y_ref, lambda cols, u: prev_ref[:, cols], emit_u)


def _mix_ffn(oa, ob, x, wo, gpost, g1, wup, cw, cb, wdn, g2, *, layer, tm, n_seq=1, prev=None):
    n, d = x.shape
    width = wup.shape[-1]
    row = lambda i: (i, 0)
    lay3 = lambda i: (layer, 0, 0)
    single = pl.Buffered(1)
    weight = lambda a: pl.BlockSpec((None,) + a.shape[1:], lay3, pipeline_mode=single)
    small = lambda a: pl.BlockSpec((None,) + a.shape[1:], lay3)
    in_specs = [pl.BlockSpec((tm, D_A), row), pl.BlockSpec((tm, DV_B), row), pl.BlockSpec((tm, d), row),
                weight(wo), small(gpost), small(g1), weight(wup), small(cw), small(cb), weight(wdn), small(g2)]
    args = [oa, ob, x, wo, gpost, g1, wup, cw, cb, wdn, g2]
    if prev is None:
        tiles_per_seq = n // n_seq // tm
        kern = functools.partial(_mix_ffn_prompt_kernel, tiles_per_seq=tiles_per_seq)
        out_specs = (pl.BlockSpec((tm, d), row),
                     pl.BlockSpec((None, SUBLANES, width), lambda i: (i // tiles_per_seq, 0, 0)))
        out_shape = (jax.ShapeDtypeStruct((n, d), F32), jax.ShapeDtypeStruct((n_seq, SUBLANES, width), F32))
        scratch = [pltpu.VMEM((SUBLANES, width), F32)]
    else:
        kern = _mix_ffn_sample_kernel
        in_specs.append(pl.BlockSpec((None, tm, width), lambda i: (layer, i, 0)))
        args.append(prev)
        out_specs = (pl.BlockSpec((tm, d), row), pl.BlockSpec((tm, width), row))
        out_shape = (jax.ShapeDtypeStruct((n, d), F32), jax.ShapeDtypeStruct((n, width), F32))
        scratch = []
    return pl.pallas_call(
        kern,
        grid=(n // tm,),
        in_specs=in_specs,
        out_specs=out_specs,
        out_shape=out_shape,
        scratch_shapes=scratch,
        compiler_params=pltpu.CompilerParams(dimension_semantics=("arbitrary",),
                                             vmem_limit_bytes=VMEM_LIMIT),
        name="mix_ffn",
    )(*args)


def _rope_tables(pos):
    inv_freq = ROPE_THETA ** (-jnp.arange(ROT_HALF, dtype=F32) / ROT_HALF)
    ang = pos.astype(F32)[:, None] * inv_freq[None, :]
    cos, sin = jnp.cos(ang), jnp.sin(ang)
    n = pos.shape[0]
    rest = HEAD_DIM_A - 2 * ROT_HALF
    reps = LANES // HEAD_DIM_A
    c = jnp.tile(jnp.concatenate([cos, cos, jnp.ones((n, rest), F32)], axis=1), (1, reps))
    s_up = jnp.tile(jnp.concatenate([jnp.zeros((n, ROT_HALF), F32), sin, jnp.zeros((n, rest), F32)], axis=1),
                    (1, reps))
    s_dn = jnp.tile(jnp.concatenate([-sin, jnp.zeros((n, ROT_HALF + rest), F32)], axis=1), (1, reps))
    return c, s_up, s_dn


def kernel(x_prompt, x_sample, cache_k_win, cache_v_win, state_gla, state_ffn_conv, g_mix_pre, g_mix_post,
           g_ffn_pre, g_ffn_post, w_in, w_gate2, b_gate, g_gla, w_out, w_up, conv_w, conv_b, w_down):
    depth = w_in.shape[0]
    b, t, d = x_prompt.shape
    bs, ts, _ = x_sample.shape
    win = cache_k_win.shape[2]
    width = w_up.shape[-1]
    conv_taps = conv_w.shape[1]
    n_pair = N_HEADS_B // 2
    win_p = min(win, t)
    n_s = bs * SAMPLE_PAD

    w_in_bf = w_in.astype(BF16)
    wg1_bf = jnp.pad(w_in[:, :, N_MAIN:], ((0, 0), (0, 0), (0, LANES - GATE_RANK))).astype(BF16)
    wg2_bf = jnp.pad(w_gate2, ((0, 0), (0, LANES - GATE_RANK), (0, 0))).astype(BF16)
    w_out_bf, w_up_bf, w_dn_bf = w_out.astype(BF16), w_up.astype(BF16), w_down.astype(BF16)
    per_layer_row = lambda a: a[:, None, :]
    bg, gg, cb = per_layer_row(b_gate), per_layer_row(g_gla), per_layer_row(conv_b)
    g_pre, g_post = per_layer_row(g_mix_pre), per_layer_row(g_mix_post)
    g_f1, g_f2 = per_layer_row(g_ffn_pre), per_layer_row(g_ffn_post)

    tabs_p = _rope_tables(jnp.arange(t, dtype=jnp.int32))
    pos_s = PAST_LEN + jnp.arange(SAMPLE_PAD, dtype=jnp.int32)
    tabs_s = tuple(jnp.tile(a, (bs, 1)) for a in _rope_tables(pos_s))

    xp = x_prompt.reshape(b * t, d)
    xs = jnp.pad(x_sample, ((0, 0), (0, SAMPLE_PAD - ts), (0, 0))).reshape(n_s, d)

    cache_kt = jnp.transpose(cache_k_win, (0, 1, 3, 4, 2))
    cache_vt = jnp.transpose(cache_v_win, (0, 1, 3, 4, 2))
    prev_s = jnp.pad(state_ffn_conv, ((0, 0), (0, 0), (SUBLANES - (conv_taps - 1), 0), (0, 0)))
    prev_s = prev_s.reshape(depth, n_s, width)
    s0_p = jnp.zeros((b, n_pair, LANES, DV_HEAD), F32)
    s0_s = state_gla.reshape(depth, bs, n_pair, LANES, DV_HEAD)

    outs_p = {"s": [], "c": []}
    outs_s = {"k": [], "v": [], "s": [], "c": []}
    win_bufs = None
    for l in range(depth):
        *sorted_qkv, qb, kb, vb, rb, la, kt_win, vt_win = _inproj(
            xp, g_pre, w_in_bf, wg1_bf, wg2_bf, bg, tabs_p, layer=l, tm=INPROJ_ROWS, gla_chunk=GLA_CHUNK,
            gla_valid=GLA_CHUNK, n_seq=b, win_rows=win_p, win_prev=win_bufs)
        win_bufs = (kt_win, vt_win)
        q_s, k_s, v_s = sorted_qkv[0:3], sorted_qkv[3:6], sorted_qkv[6:9]
        oa = _attn_prompt(q_s + k_s + v_s)
        ob, s_fin = _gla(qb, kb, vb, la, rb, gg, s0_p, layer=l, n_seq=b, seqs_per_step=1,
                         rows_per_seq_step=GLA_ROWS, chunk=GLA_CHUNK, sub=GLA_SUB, t_valid=GLA_CHUNK)
        xp, tail = _mix_ffn(oa.reshape(b * t, D_A), ob, xp, w_out_bf, g_post, g_f1, w_up_bf, conv_w, cb, w_dn_bf,
                            g_f2, layer=l, tm=FFN_ROWS, n_seq=b)
        outs_p["s"].append(s_fin.reshape(b, N_HEADS_B, DK_HEAD, DV_HEAD))
        outs_p["c"].append(tail[:, SUBLANES - (conv_taps - 1):])

        qa, ka, va, qb, kb, vb, rb, la = _inproj(xs, g_pre, w_in_bf, wg1_bf, wg2_bf, bg, tabs_s, layer=l, tm=n_s,
                                                 gla_chunk=SAMPLE_PAD, gla_valid=ts)
        ka3 = ka.reshape(bs, SAMPLE_PAD, D_A)
        va3 = va.reshape(bs, SAMPLE_PAD, D_A)
        oa = _attn_sample(qa.reshape(bs, SAMPLE_PAD, D_A), cache_kt, cache_vt, ka3, va3, layer=l)
        ob, s_fin = _gla(qb, kb, vb, la, rb, gg, s0_s[l], layer=l, n_seq=bs, seqs_per_step=math.gcd(bs, GLA_SAMPLE_SEQS),
                         rows_per_seq_step=SAMPLE_PAD, chunk=SAMPLE_PAD, sub=SAMPLE_PAD, t_valid=ts)
        xs, u_all = _mix_ffn(oa.reshape(n_s, D_A), ob, xs, w_out_bf, g_post, g_f1, w_up_bf, conv_w, cb, w_dn_bf,
                             g_f2, layer=l, tm=n_s, prev=prev_s)
        outs_s["k"].append(ka3[:, :ts].reshape(bs, ts, N_HEADS_A, HEAD_DIM_A))
        outs_s["v"].append(va3[:, :ts].reshape(bs, ts, N_HEADS_A, HEAD_DIM_A))
        outs_s["s"].append(s_fin.reshape(bs, N_HEADS_B, DK_HEAD, DV_HEAD))
        outs_s["c"].append(u_all.reshape(bs, SAMPLE_PAD, width)[:, ts - (conv_taps - 1):ts])

    y_prompt = xp.reshape(b, t, d)
    y_sample = xs.reshape(bs, SAMPLE_PAD, d)[:, :ts]
    st = lambda xs_: jnp.stack(xs_)
    to_rows = lambda a: jnp.transpose(a.reshape(depth, b, N_HEADS_A, HEAD_DIM_A, win_p), (0, 1, 4, 2, 3))
    return (y_prompt, y_sample, to_rows(win_bufs[0]), to_rows(win_bufs[1]), st(outs_p["s"]), st(outs_p["c"]),
            st(outs_s["k"]), st(outs_s["v"]), st(outs_s["s"]), st(outs_s["c"]))
```

```python
import functools
import math

import jax
import jax.numpy as jnp
from jax import lax
from jax.experimental import pallas as pl
from jax.experimental.pallas import tpu as pltpu

F32 = jnp.float32
BF16 = jnp.bfloat16

PAST_LEN = 16384
N_HEADS_A = 8
HEAD_DIM_A = 64
D_A = N_HEADS_A * HEAD_DIM_A
ROT_HALF = HEAD_DIM_A // 8
ROPE_THETA = 500000.0
WIN_DENSE, WIN_MID, WIN_FAR = 128, 512, 2048
DILATIONS = (1, 4, 16)
BAND = 128
N_HEADS_B = 4
DK_HEAD = 64
DV_HEAD = 128
DK_B = N_HEADS_B * DK_HEAD
DV_B = N_HEADS_B * DV_HEAD
N_MAIN = 3 * D_A + 2 * DK_B + 2 * DV_B
GATE_RANK = 16
GATE_TAU = 16.0
GLA_CHUNK = 64
GLA_SUB = 16
EPS = 1e-6
NEG_BIG = -1e30

LANES = 128
SUBLANES = 8
VMEM_LIMIT = 56 * 1024 * 1024
SAMPLE_PAD = SUBLANES
FF_CHUNK = 2 * LANES
INPROJ_ROWS = 256
GLA_ROWS = 1024
GLA_SAMPLE_SEQS = 8
FFN_ROWS = 512
MERGE_ROWS = 256
ATT_UNROLL = 8

_NT = (((1,), (1,)), ((), ()))
_TN = (((0,), (0,)), ((), ()))


def _rms(x, g):
    ms = jnp.mean(x * x, axis=-1, keepdims=True)
    return x * lax.rsqrt(ms + EPS) * g


def _split3(x):
    hi = x.astype(BF16)
    r1 = x - hi.astype(F32)
    mid = r1.astype(BF16)
    lo = (r1 - mid.astype(F32)).astype(BF16)
    return hi, mid, lo


def _dilation_multiplicity(d):
    nonneg = d >= 0
    m0 = nonneg & (d <= WIN_DENSE)
    m1 = nonneg & (d <= WIN_MID) & ((d & 3) == 0)
    m2 = nonneg & (d <= WIN_FAR) & ((d & 15) == 0)
    return m0.astype(F32) + m1.astype(F32) + m2.astype(F32)


def _inproj_kernel(x_ref, g_ref, w_ref, wg1_ref, wg2_ref, bg_ref, c_ref, su_ref, sd_ref, *refs,
                   prompt, layer, n_alias, gla_chunk, gla_valid):
    refs = refs[n_alias:]
    if prompt:
        sorted_refs, rest = refs[:9], refs[9:]
        qb_ref, kb_ref, vb_ref, rb_ref, la_ref, kt_ref, vt_ref, scr = rest
    else:
        qa_ref, ka_ref, va_ref, qb_ref, kb_ref, vb_ref, rb_ref, la_ref = refs
    h = _rms(x_ref[...], g_ref[...]).astype(BF16)
    tm = h.shape[0]

    def proj(lo, hi):
        return jnp.dot(h, w_ref[:, lo:hi], preferred_element_type=F32)

    cos = c_ref[...]
    s_up = su_ref[...]
    s_dn = sd_ref[...]

    def rope(y):
        outs = []
        for j in range(D_A // LANES):
            yj = y[:, LANES * j:LANES * (j + 1)]
            outs.append(yj * cos + pltpu.roll(yj, ROT_HALF, 1) * s_up
                        + pltpu.roll(yj, LANES - ROT_HALF, 1) * s_dn)
        return jnp.concatenate(outs, axis=1)

    gate_lr = jnp.dot(h, wg1_ref[...], preferred_element_type=F32)
    z = jnp.dot(gate_lr.astype(BF16), wg2_ref[...], preferred_element_type=F32) + bg_ref[...]
    log_sig = jnp.minimum(z, 0.0) - jnp.log1p(jnp.exp(-jnp.abs(z)))
    log_a = log_sig * (1.0 / GATE_TAU)
    r = lax.broadcasted_iota(jnp.int32, (tm, tm), 0)
    c = lax.broadcasted_iota(jnp.int32, (tm, tm), 1)
    tri = ((c <= r) & (c // gla_chunk == r // gla_chunk) & (c % gla_chunk < gla_valid)).astype(BF16)
    la_ref[...] = sum(jnp.dot(tri, piece, preferred_element_type=F32) for piece in _split3(log_a))

    qkv = (rope(proj(0, D_A)) * (HEAD_DIM_A ** -0.5), rope(proj(D_A, 2 * D_A)), proj(2 * D_A, 3 * D_A))
    if prompt:
        for a, val in enumerate(qkv):
            sorted_refs[3 * a][0] = val.astype(BF16)
            for c in range(D_A // LANES):
                cl = slice(LANES * c, LANES * (c + 1))
                scr[a, c] = val[:, cl]
                for o_ref, s in zip(sorted_refs[3 * a + 1:3 * a + 3], DILATIONS[1:]):
                    for r in range(s):
                        o_ref[r, :, cl] = scr.at[a, c][pl.ds(r, tm // s, stride=s), :].astype(BF16)
    else:
        qa_ref[...], ka_ref[...], va_ref[...] = qkv
    o = 3 * D_A
    qb_ref[...] = proj(o, o + DK_B) * (DK_HEAD ** -0.5); o += DK_B
    kb_ref[...] = proj(o, o + DK_B); o += DK_B
    vb_ref[...] = proj(o, o + DV_B); o += DV_B
    rb_ref[...] = proj(o, o + DV_B)
    if prompt:
        for c in range(D_A // LANES):
            cr = slice(LANES * c, LANES * (c + 1))
            for ref, a in ((kt_ref, 1), (vt_ref, 2)):
                if n_alias:
                    ref[cr, :] = scr[a, c].T
                else:
                    for dl in range(ref.shape[0]):
                        ref[dl, cr, :] = scr[a, c].T if dl == layer else jnp.zeros((LANES, tm), F32)


def _inproj(x, g, w, wg1, wg2, bg, tabs, *, layer, tm, gla_chunk, gla_valid, n_seq=1, win_rows=0, win_prev=None):
    n, d = x.shape
    prompt = win_rows > 0
    t = n // n_seq
    tiles_per_seq = t // tm
    first_win_tile = tiles_per_seq - win_rows // tm
    assert win_rows % tm == 0 and tm % DILATIONS[-1] == 0
    n_tab = tabs[0].shape[0] // tm
    row = lambda i: (i, 0)
    tab = lambda i: (i % n_tab, 0)
    lay3 = lambda i: (layer, 0, 0)
    f32_rows = lambda w_: (jax.ShapeDtypeStruct((n, w_), F32), pl.BlockSpec((tm, w_), row))
    outs = []
    if prompt:
        tile4 = lambda i: (i // tiles_per_seq, 0, i % tiles_per_seq, 0)
        for _ in range(3):
            for s in DILATIONS:
                outs.append((jax.ShapeDtypeStruct((n_seq, s, t // s, D_A), BF16),
                             pl.BlockSpec((None, s, tm // s, D_A), tile4)))
    else:
        outs += [f32_rows(D_A)] * 3
    outs += [f32_rows(DK_B), f32_rows(DK_B), f32_rows(DV_B), f32_rows(DV_B), f32_rows(DK_B)]
    scratch = []
    if prompt:
        depth = w.shape[0]
        win_col = lambda i: jnp.maximum(i % tiles_per_seq - first_win_tile, 0)
        if win_prev is None:
            win_spec = pl.BlockSpec((depth, None, D_A, tm), lambda i: (0, i // tiles_per_seq, 0, win_col(i)))
        else:
            win_spec = pl.BlockSpec((None, None, D_A, tm), lambda i: (layer, i // tiles_per_seq, 0, win_col(i)))
        outs += [(jax.ShapeDtypeStruct((depth, n_seq, D_A, win_rows), F32), win_spec)] * 2
        scratch = [pltpu.VMEM((3, D_A // LANES, tm, LANES), F32)]
    alias_in = list(win_prev) if win_prev is not None else []
    n_in = 9
    aliases = {n_in + a: len(outs) - len(alias_in) + a for a in range(len(alias_in))}
    kern = functools.partial(_inproj_kernel, prompt=prompt, layer=layer, n_alias=len(alias_in),
                             gla_chunk=gla_chunk, gla_valid=gla_valid)
    return pl.pallas_call(
        kern,
        grid=(n // tm,),
        in_specs=[pl.BlockSpec((tm, d), row), pl.BlockSpec((None, 1, d), lay3),
                  pl.BlockSpec((None,) + w.shape[1:], lay3), pl.BlockSpec((None,) + wg1.shape[1:], lay3),
                  pl.BlockSpec((None,) + wg2.shape[1:], lay3),
                  pl.BlockSpec((None, 1, DK_B), lay3),
                  pl.BlockSpec((tm, LANES), tab), pl.BlockSpec((tm, LANES), tab), pl.BlockSpec((tm, LANES), tab)]
        + [pl.BlockSpec(memory_space=pl.ANY)] * len(alias_in),
        out_specs=tuple(spec for _, spec in outs),
        out_shape=tuple(shape for shape, _ in outs),
        input_output_aliases=aliases,
        scratch_shapes=scratch,
        compiler_params=pltpu.CompilerParams(dimension_semantics=("arbitrary",),
                                             vmem_limit_bytes=VMEM_LIMIT),
        name="inproj",
    )(x, g, w, wg1, wg2, bg, *tabs, *alias_in)


def _attn_prompt_kernel(*refs, t):
    sorted_refs, (bias_ref, o_ref, on, ln) = refs[:9], refs[9:]
    blk = BAND
    lane = lax.broadcasted_iota(jnp.int32, (1, LANES), 1)
    head0 = lane < HEAD_DIM_A
    head_sel = (head0, jnp.logical_not(head0))

    def band_blocks(pi, s, first_idx):
        q_ref, k_ref, v_ref = sorted_refs[pi], sorted_refs[3 + pi], sorted_refs[6 + pi]
        n_blk = t // s // blk
        blocks = []
        for u in range(ATT_UNROLL):
            idx = first_idx + u
            r = idx // n_blk
            j = idx % n_blk
            q0 = pl.multiple_of(j * blk, blk)
            w0 = pl.multiple_of(q0 - jnp.minimum(j, 1) * blk, blk)
            rows = pl.ds(q0, blk) if s == 1 else pl.ds(r + s * q0, blk, stride=s)
            blocks.append((q_ref[r, pl.ds(q0, blk), :], k_ref[r, pl.ds(w0, 2 * blk), :],
                           v_ref[r, pl.ds(w0, 2 * blk), :], bias_ref[jnp.minimum(j, 1)], rows))
        heads = [(u, sel) for u in range(ATT_UNROLL) for sel in head_sel]
        sc = [lax.dot_general(jnp.where(sel, blocks[u][0], jnp.zeros_like(blocks[u][0])), blocks[u][1], _NT,
                              preferred_element_type=F32) + blocks[u][3] for u, sel in heads]
        m = [jnp.max(x, axis=1, keepdims=True) for x in sc]
        p = [jnp.exp(x - mx) for x, mx in zip(sc, m)]
        l = [jnp.sum(x, axis=1, keepdims=True) for x in p]
        o = [jnp.dot(x.astype(BF16), blocks[u][2], preferred_element_type=F32) for x, (u, _) in zip(p, heads)]
        for u in range(ATT_UNROLL):
            rows = blocks[u][4]
            l_pair = jnp.where(head0, l[2 * u], l[2 * u + 1])
            on.at[pi][rows, :] = jnp.where(head0, o[2 * u], o[2 * u + 1]) / l_pair
            ln.at[pi][rows, :] = jnp.where(head0, m[2 * u], m[2 * u + 1]) + jnp.log(l_pair)

    for pi, s in enumerate(DILATIONS):
        def one_group(gi, c, pi=pi, s=s):
            band_blocks(pi, s, gi * ATT_UNROLL)
            return c

        lax.fori_loop(0, t // blk // ATT_UNROLL, one_group, 0)

    def merge(i, c):
        rows = pl.ds(pl.multiple_of(i * MERGE_ROWS, MERGE_ROWS), MERGE_ROWS)
        lse = [ln[pi, rows, :] for pi in range(len(DILATIONS))]
        top = jnp.maximum(jnp.maximum(lse[0], lse[1]), lse[2])
        w = [jnp.exp(x - top) for x in lse]
        num = w[0] * on[0, rows, :] + w[1] * on[1, rows, :] + w[2] * on[2, rows, :]
        o_ref[rows, :] = (num / (w[0] + w[1] + w[2])).astype(o_ref.dtype)
        return c

    lax.fori_loop(0, t // MERGE_ROWS, merge, 0)


def _band_bias():
    iq = jnp.arange(BAND, dtype=jnp.int32)[:, None]
    c = jnp.arange(2 * BAND, dtype=jnp.int32)[None, :]
    d = jnp.stack([iq - c, iq + BAND - c])
    return jnp.where((d >= 0) & (d <= BAND), 0.0, NEG_BIG).astype(F32)


def _attn_prompt(sorted_qkv):
    b, _, t, _ = sorted_qkv[0].shape
    n_pair = D_A // LANES
    n_pat = len(DILATIONS)
    bias = _band_bias()
    pair4 = lambda bi, p: (bi, 0, 0, p)
    in_specs = [pl.BlockSpec((None,) + a.shape[1:3] + (LANES,), pair4) for a in sorted_qkv]
    in_specs.append(pl.BlockSpec(bias.shape, lambda bi, p: (0, 0, 0)))
    return pl.pallas_call(
        functools.partial(_attn_prompt_kernel, t=t),
        grid=(b, n_pair),
        in_specs=in_specs,
        out_specs=pl.BlockSpec((None, t, LANES), lambda bi, p: (bi, 0, p)),
        out_shape=jax.ShapeDtypeStruct((b, t, D_A), BF16),
        scratch_shapes=[pltpu.VMEM((n_pat, t, LANES), F32)] * 2,
        compiler_params=pltpu.CompilerParams(dimension_semantics=("arbitrary", "arbitrary"),
                                             vmem_limit_bytes=VMEM_LIMIT),
        name="attn_prompt",
    )(*sorted_qkv, bias)


def _attn_sample_kernel(q_ref, kt_ref, vt_ref, kn_ref, vn_ref, o_ref, *, win):
    q = q_ref[...].astype(BF16)
    pad = jnp.zeros((LANES - SAMPLE_PAD, D_A), F32)
    kn_t = jnp.concatenate([kn_ref[...], pad], axis=0).T.astype(BF16)
    vn_t = jnp.concatenate([vn_ref[...], pad], axis=0).T.astype(BF16)
    n_keys = win + LANES
    tok = lax.broadcasted_iota(jnp.int32, (SAMPLE_PAD, n_keys), 0)
    key = lax.broadcasted_iota(jnp.int32, (SAMPLE_PAD, n_keys), 1)
    mult = _dilation_multiplicity(win + tok - key)
    live = mult > 0.0
    hs = [slice(HEAD_DIM_A * h, HEAD_DIM_A * (h + 1)) for h in range(N_HEADS_A)]
    kts = [jnp.concatenate([kt_ref[h].astype(BF16), kn_t[hs[h], :]], axis=1) for h in range(N_HEADS_A)]
    ss = [jnp.where(live, jnp.dot(q[:, hs[h]], kts[h], preferred_element_type=F32), NEG_BIG)
          for h in range(N_HEADS_A)]
    ps = []
    for s in ss:
        m = jnp.max(s, axis=1, keepdims=True)
        p = jnp.exp(s - m) * mult
        ps.append((p / jnp.sum(p, axis=1, keepdims=True)).astype(BF16))
    vts = [jnp.concatenate([vt_ref[h].astype(BF16), vn_t[hs[h], :]], axis=1) for h in range(N_HEADS_A)]
    head_cols = [lax.dot_general(vts[h], ps[h], _NT, preferred_element_type=F32) for h in range(N_HEADS_A)]
    o_ref[...] = jnp.concatenate(head_cols, axis=0).T


def _attn_sample(q, kt, vt, kn, vn, *, layer):
    bs = q.shape[0]
    win = kt.shape[-1]
    seq = lambda i: (i, 0, 0)
    slab = lambda i: (layer, i, 0, 0, 0)
    return pl.pallas_call(
        functools.partial(_attn_sample_kernel, win=win),
        grid=(bs,),
        in_specs=[pl.BlockSpec((None, SAMPLE_PAD, D_A), seq),
                  pl.BlockSpec((None, None, N_HEADS_A, HEAD_DIM_A, win), slab),
                  pl.BlockSpec((None, None, N_HEADS_A, HEAD_DIM_A, win), slab),
                  pl.BlockSpec((None, SAMPLE_PAD, D_A), seq), pl.BlockSpec((None, SAMPLE_PAD, D_A), seq)],
        out_specs=pl.BlockSpec((None, SAMPLE_PAD, D_A), seq),
        out_shape=jax.ShapeDtypeStruct((bs, SAMPLE_PAD, D_A), F32),
        compiler_params=pltpu.CompilerParams(dimension_semantics=("arbitrary",),
                                             vmem_limit_bytes=VMEM_LIMIT),
        name="attn_sample",
    )(q, kt, vt, kn, vn)


def _gla_kernel(q_ref, k_ref, v_ref, b_ref, r_ref, g_ref, s0_ref, o_ref, sf_ref, s_scr,
                *, chunk, sub, n_seqs, n_chunks, t_valid):
    n_pair = N_HEADS_B // 2
    pair_v = 2 * DV_HEAD
    chains = [(sq, p) for sq in range(n_seqs) for p in range(n_pair)]

    @pl.when(pl.program_id(1) == 0)
    def _():
        zero = jnp.zeros((DK_HEAD, DV_HEAD), F32)
        for sq, p in chains:
            s0 = s0_ref[sq, p]
            s_scr[sq, p] = jnp.concatenate([jnp.concatenate([s0[:DK_HEAD], zero], axis=1),
                                            jnp.concatenate([zero, s0[DK_HEAD:]], axis=1)], axis=0)

    n_sub = chunk // sub
    row = lax.broadcasted_iota(jnp.int32, (chunk, 1), 0)
    lane = lax.broadcasted_iota(jnp.int32, (1, LANES), 1)
    head0 = lane < DK_HEAD
    causal = (lax.broadcasted_iota(jnp.int32, (chunk, 2 * chunk), 1) % chunk
              <= lax.broadcasted_iota(jnp.int32, (chunk, 2 * chunk), 0))
    own_block = (lax.broadcasted_iota(jnp.int32, (LANES, pair_v), 0) // DK_HEAD
                 == lax.broadcasted_iota(jnp.int32, (LANES, pair_v), 1) // DV_HEAD)
    g = g_ref[...]
    keep = row < t_valid
    zero_v = jnp.zeros((chunk, DV_HEAD), BF16)

    items = [(sq, c, p) for c in range(n_chunks) for sq, p in chains]
    rows_of = lambda sq, c: slice((sq * n_chunks + c) * chunk, (sq * n_chunks + c + 1) * chunk)
    b_, q_, k_, v_ = {}, {}, {}, {}
    for sq, c, p in items:
        ls = slice(LANES * p, LANES * (p + 1))
        b_[sq, c, p] = b_ref[rows_of(sq, c), ls]
        q_[sq, c, p] = q_ref[rows_of(sq, c), ls]
        kp = k_ref[rows_of(sq, c), ls]
        k_[sq, c, p] = jnp.where(keep, kp, 0.0) if t_valid < chunk else kp
        v_[sq, c, p] = v_ref[rows_of(sq, c), pair_v * p:pair_v * (p + 1)].astype(BF16)
    b_last = {it: b_[it][chunk - 1:chunk, :] for it in items}
    q_dec = {it: (q_[it] * jnp.exp(b_[it])).astype(BF16) for it in items}
    k_hat = {it: (k_[it] * jnp.exp(b_last[it] - b_[it])).astype(BF16) for it in items}
    q_sub, k_sub = {}, {}
    for it in items:
        b = b_[it]
        for i in range(n_sub):
            r0, r1 = sub * i, sub * (i + 1)
            beta = b[r0:r0 + 1, :]
            q_sub[it, i] = (q_[it][r0:r1] * jnp.exp(b[r0:r1] - beta)).astype(BF16)
            k_i = (k_[it] * jnp.exp(jnp.where(row < r1, beta - b, 0.0))).astype(BF16)
            k_sub[it, i] = jnp.concatenate([jnp.where(head0, k_i, jnp.zeros_like(k_i)),
                                            jnp.where(head0, jnp.zeros_like(k_i), k_i)], axis=0)
    parts = {key: lax.dot_general(q_sub[key], k_sub[key], _NT, preferred_element_type=F32) for key in q_sub}
    attn = {}
    for it in items:
        a = parts[it, 0] if n_sub == 1 else jnp.concatenate([parts[it, i] for i in range(n_sub)], axis=0)
        attn[it] = jnp.where(causal, a, 0.0).astype(BF16)
    v_bd = {it: jnp.concatenate([jnp.concatenate([v_[it][:, :DV_HEAD], zero_v], axis=1),
                                 jnp.concatenate([zero_v, v_[it][:, DV_HEAD:]], axis=1)], axis=0) for it in items}
    o_intra = {it: jnp.dot(attn[it], v_bd[it], preferred_element_type=F32) for it in items}
    ds_ = {it: lax.dot_general(k_hat[it], v_[it], _TN, preferred_element_type=F32) for it in items}
    decay_col = {it: jnp.exp(jnp.broadcast_to(b_last[it], (SUBLANES, LANES)).T[:, 0:1]) for it in items}
    state = {ch: s_scr[ch] for ch in chains}
    for it in items:
        sq, c, p = it
        st = state[sq, p]
        o_pair = o_intra[it] + jnp.dot(q_dec[it], st.astype(BF16), preferred_element_type=F32)
        state[sq, p] = jnp.where(own_block, decay_col[it] * st + ds_[it], 0.0)
        for hh in range(2):
            vs = slice(DV_HEAD * (2 * p + hh), DV_HEAD * (2 * p + hh + 1))
            r = r_ref[rows_of(sq, c), vs]
            o = o_pair[:, DV_HEAD * hh:DV_HEAD * (hh + 1)]
            o_ref[rows_of(sq, c), vs] = (_rms(o, g) * (r * (1.0 / (1.0 + jnp.exp(-r))))).astype(o_ref.dtype)
    for ch in chains:
        s_scr[ch] = state[ch]
        sf_ref[ch] = jnp.concatenate([state[ch][:DK_HEAD, :DV_HEAD], state[ch][DK_HEAD:, DV_HEAD:]], axis=0)


def _gla(q, k, v, la, r, g, s0, *, layer, n_seq, seqs_per_step, rows_per_seq_step, chunk, sub, t_valid):
    n = q.shape[0]
    t = n // n_seq
    steps = t // rows_per_seq_step
    assert seqs_per_step == 1 or steps == 1
    rows_per_step = seqs_per_step * rows_per_seq_step
    row = lambda b, i: (b * steps + i, 0)
    st = lambda b, i: (b, 0, 0, 0)
    n_pair = N_HEADS_B // 2
    out_dtype = BF16 if chunk % (2 * SUBLANES) == 0 else F32
    kern = functools.partial(_gla_kernel, chunk=chunk, sub=sub, n_seqs=seqs_per_step,
                             n_chunks=rows_per_seq_step // chunk, t_valid=t_valid)
    return pl.pallas_call(
        kern,
        grid=(n_seq // seqs_per_step, steps),
        in_specs=[pl.BlockSpec((rows_per_step, DK_B), row), pl.BlockSpec((rows_per_step, DK_B), row),
                  pl.BlockSpec((rows_per_step, DV_B), row), pl.BlockSpec((rows_per_step, DK_B), row),
                  pl.BlockSpec((rows_per_step, DV_B), row),
                  pl.BlockSpec((None, 1, DV_HEAD), lambda b, i: (layer, 0, 0)),
                  pl.BlockSpec((seqs_per_step, n_pair, LANES, DV_HEAD), st)],
        out_specs=(pl.BlockSpec((rows_per_step, DV_B), row),
                   pl.BlockSpec((seqs_per_step, n_pair, LANES, DV_HEAD), st)),
        out_shape=(jax.ShapeDtypeStruct((n, DV_B), out_dtype),
                   jax.ShapeDtypeStruct((n_seq, n_pair, LANES, DV_HEAD), F32)),
        scratch_shapes=[pltpu.VMEM((seqs_per_step, n_pair, LANES, 2 * DV_HEAD), F32)],
        compiler_params=pltpu.CompilerParams(dimension_semantics=("arbitrary", "arbitrary"),
                                             vmem_limit_bytes=VMEM_LIMIT),
        name="gla",
    )(q, k, v, la, r, g, s0)


def _gelu_tanh(x):
    c = math.sqrt(2.0 / math.pi)
    return 0.5 * x * (1.0 + jnp.tanh(c * (x + 0.044715 * (x * x * x))))


def _causal_conv(u, prev, cw, cb):
    rows, width = u.shape
    tiles = rows // SUBLANES
    u3 = u.reshape(tiles, SUBLANES, width)
    p3 = prev.reshape(tiles, SUBLANES, width)
    r = lax.broadcasted_iota(jnp.int32, (tiles, SUBLANES, width), 1)
    back1 = pltpu.roll(jnp.where(r >= SUBLANES - 1, p3, u3), 1, 1).reshape(rows, width)
    back2 = pltpu.roll(jnp.where(r >= SUBLANES - 2, p3, u3), 2, 1).reshape(rows, width)
    return cb + cw[0:1, :] * back2 + cw[1:2, :] * back1 + cw[2:3, :] * u


def _mix_ffn_core(oa_ref, ob_ref, x_ref, wo_ref, gpost_ref, g1_ref, wup_ref, cw_ref, cb_ref, wdn_ref, g2_ref,
                  y_ref, prev_cols, emit_u):
    d_ff = wdn_ref.shape[0]
    m = (jnp.dot(oa_ref[...].astype(BF16), wo_ref[0:D_A, :], preferred_element_type=F32)
         + jnp.dot(ob_ref[...].astype(BF16), wo_ref[D_A:D_A + DV_B, :], preferred_element_type=F32))
    x1 = x_ref[...] + _rms(m, gpost_ref[...])
    h = _rms(x1, g1_ref[...]).astype(BF16)
    ys = []
    for j in range(d_ff // FF_CHUNK):
        halves = []
        for base in (0, d_ff):
            cols = slice(base + j * FF_CHUNK, base + (j + 1) * FF_CHUNK)
            u = jnp.dot(h, wup_ref[:, cols], preferred_element_type=F32)
            halves.append(_causal_conv(u, prev_cols(cols, u), cw_ref[:, cols], cb_ref[:, cols]))
            emit_u(cols, u)
        ys.append((_gelu_tanh(halves[0]) * halves[1]).astype(BF16))
    f = jnp.dot(jnp.concatenate(ys, axis=1), wdn_ref[...], preferred_element_type=F32)
    y_ref[...] = x1 + _rms(f, g2_ref[...])


def _mix_ffn_prompt_kernel(*refs, tiles_per_seq):
    in_refs, (y_ref, tail_ref, carry) = refs[:11], refs[11:]
    tm = y_ref.shape[0]

    @pl.when(pl.program_id(0) % tiles_per_seq == 0)
    def _():
        carry[...] = jnp.zeros_like(carry)

    def prev_cols(cols, u):
        return jnp.concatenate([carry[:, cols], u[:tm - SUBLANES]], axis=0)

    def emit_u(cols, u):
        tail = u[tm - SUBLANES:]
        carry[:, cols] = tail
        tail_ref[:, cols] = tail

    _mix_ffn_core(*in_refs, y_ref, prev_cols, emit_u)


def _mix_ffn_sample_kernel(*refs):
    in_refs, (prev_ref, y_ref, u_ref) = refs[:11], refs[11:]

    def emit_u(cols, u):
        u_ref[:, cols] = u

    _mix_ffn_core(*in_refs, y_ref, lambda cols, u: prev_ref[:, cols], emit_u)


def _mix_ffn(oa, ob, x, wo, gpost, g1, wup, cw, cb, wdn, g2, *, layer, tm, n_seq=1, prev=None):
    n, d = x.shape
    width = wup.shape[-1]
    row = lambda i: (i, 0)
    lay3 = lambda i: (layer, 0, 0)
    single = pl.Buffered(1)
    weight = lambda a: pl.BlockSpec((None,) + a.shape[1:], lay3, pipeline_mode=single)
    small = lambda a: pl.BlockSpec((None,) + a.shape[1:], lay3)
    in_specs = [pl.BlockSpec((tm, D_A), row), pl.BlockSpec((tm, DV_B), row), pl.BlockSpec((tm, d), row),
                weight(wo), small(gpost), small(g1), weight(wup), small(cw), small(cb), weight(wdn), small(g2)]
    args = [oa, ob, x, wo, gpost, g1, wup, cw, cb, wdn, g2]
    if prev is None:
        tiles_per_seq = n // n_seq // tm
        kern = functools.partial(_mix_ffn_prompt_kernel, tiles_per_seq=tiles_per_seq)
        out_specs = (pl.BlockSpec((tm, d), row),
                     pl.BlockSpec((None, SUBLANES, width), lambda i: (i // tiles_per_seq, 0, 0)))
        out_shape = (jax.ShapeDtypeStruct((n, d), F32), jax.ShapeDtypeStruct((n_seq, SUBLANES, width), F32))
        scratch = [pltpu.VMEM((SUBLANES, width), F32)]
    else:
        kern = _mix_ffn_sample_kernel
        in_specs.append(pl.BlockSpec((None, tm, width), lambda i: (layer, i, 0)))
        args.append(prev)
        out_specs = (pl.BlockSpec((tm, d), row), pl.BlockSpec((tm, width), row))
        out_shape = (jax.ShapeDtypeStruct((n, d), F32), jax.ShapeDtypeStruct((n, width), F32))
        scratch = []
    return pl.pallas_call(
        kern,
        grid=(n // tm,),
        in_specs=in_specs,
        out_specs=out_specs,
        out_shape=out_shape,
        scratch_shapes=scratch,
        compiler_params=pltpu.CompilerParams(dimension_semantics=("arbitrary",),
                                             vmem_limit_bytes=VMEM_LIMIT),
        name="mix_ffn",
    )(*args)


def _rope_tables(pos):
    inv_freq = ROPE_THETA ** (-jnp.arange(ROT_HALF, dtype=F32) / ROT_HALF)
    ang = pos.astype(F32)[:, None] * inv_freq[None, :]
    cos, sin = jnp.cos(ang), jnp.sin(ang)
    n = pos.shape[0]
    rest = HEAD_DIM_A - 2 * ROT_HALF
    reps = LANES // HEAD_DIM_A
    c = jnp.tile(jnp.concatenate([cos, cos, jnp.ones((n, rest), F32)], axis=1), (1, reps))
    s_up = jnp.tile(jnp.concatenate([jnp.zeros((n, ROT_HALF), F32), sin, jnp.zeros((n, rest), F32)], axis=1),
                    (1, reps))
    s_dn = jnp.tile(jnp.concatenate([-sin, jnp.zeros((n, ROT_HALF + rest), F32)], axis=1), (1, reps))
    return c, s_up, s_dn


def kernel(x_prompt, x_sample, cache_k_win, cache_v_win, state_gla, state_ffn_conv, g_mix_pre, g_mix_post,
           g_ffn_pre, g_ffn_post, w_in, w_gate2, b_gate, g_gla, w_out, w_up, conv_w, conv_b, w_down):
    depth = w_in.shape[0]
    b, t, d = x_prompt.shape
    bs, ts, _ = x_sample.shape
    win = cache_k_win.shape[2]
    width = w_up.shape[-1]
    conv_taps = conv_w.shape[1]
    n_pair = N_HEADS_B // 2
    win_p = min(win, t)
    n_s = bs * SAMPLE_PAD

    w_in_bf = w_in.astype(BF16)
    wg1_bf = jnp.pad(w_in[:, :, N_MAIN:], ((0, 0), (0, 0), (0, LANES - GATE_RANK))).astype(BF16)
    wg2_bf = jnp.pad(w_gate2, ((0, 0), (0, LANES - GATE_RANK), (0, 0))).astype(BF16)
    w_out_bf, w_up_bf, w_dn_bf = w_out.astype(BF16), w_up.astype(BF16), w_down.astype(BF16)
    per_layer_row = lambda a: a[:, None, :]
    bg, gg, cb = per_layer_row(b_gate), per_layer_row(g_gla), per_layer_row(conv_b)
    g_pre, g_post = per_layer_row(g_mix_pre), per_layer_row(g_mix_post)
    g_f1, g_f2 = per_layer_row(g_ffn_pre), per_layer_row(g_ffn_post)

    tabs_p = _rope_tables(jnp.arange(t, dtype=jnp.int32))
    pos_s = PAST_LEN + jnp.arange(SAMPLE_PAD, dtype=jnp.int32)
    tabs_s = tuple(jnp.tile(a, (bs, 1)) for a in _rope_tables(pos_s))

    xp = x_prompt.reshape(b * t, d)
    xs = jnp.pad(x_sample, ((0, 0), (0, SAMPLE_PAD - ts), (0, 0))).reshape(n_s, d)

    cache_kt = jnp.transpose(cache_k_win, (0, 1, 3, 4, 2))
    cache_vt = jnp.transpose(cache_v_win, (0, 1, 3, 4, 2))
    prev_s = jnp.pad(state_ffn_conv, ((0, 0), (0, 0), (SUBLANES - (conv_taps - 1), 0), (0, 0)))
    prev_s = prev_s.reshape(depth, n_s, width)
    s0_p = jnp.zeros((b, n_pair, LANES, DV_HEAD), F32)
    s0_s = state_gla.reshape(depth, bs, n_pair, LANES, DV_HEAD)

    outs_p = {"s": [], "c": []}
    outs_s = {"k": [], "v": [], "s": [], "c": []}
    win_bufs = None
    for l in range(depth):
        *sorted_qkv, qb, kb, vb, rb, la, kt_win, vt_win = _inproj(
            xp, g_pre, w_in_bf, wg1_bf, wg2_bf, bg, tabs_p, layer=l, tm=INPROJ_ROWS, gla_chunk=GLA_CHUNK,
            gla_valid=GLA_CHUNK, n_seq=b, win_rows=win_p, win_prev=win_bufs)
        win_bufs = (kt_win, vt_win)
        q_s, k_s, v_s = sorted_qkv[0:3], sorted_qkv[3:6], sorted_qkv[6:9]
        oa = _attn_prompt(q_s + k_s + v_s)
        ob, s_fin = _gla(qb, kb, vb, la, rb, gg, s0_p, layer=l, n_seq=b, seqs_per_step=1,
                         rows_per_seq_step=GLA_ROWS, chunk=GLA_CHUNK, sub=GLA_SUB, t_valid=GLA_CHUNK)
        xp, tail = _mix_ffn(oa.reshape(b * t, D_A), ob, xp, w_out_bf, g_post, g_f1, w_up_bf, conv_w, cb, w_dn_bf,
                            g_f2, layer=l, tm=FFN_ROWS, n_seq=b)
        outs_p["s"].append(s_fin.reshape(b, N_HEADS_B, DK_HEAD, DV_HEAD))
        outs_p["c"].append(tail[:, SUBLANES - (conv_taps - 1):])

        qa, ka, va, qb, kb, vb, rb, la = _inproj(xs, g_pre, w_in_bf, wg1_bf, wg2_bf, bg, tabs_s, layer=l, tm=n_s,
                                                 gla_chunk=SAMPLE_PAD, gla_valid=ts)
        ka3 = ka.reshape(bs, SAMPLE_PAD, D_A)
        va3 = va.reshape(bs, SAMPLE_PAD, D_A)
        oa = _attn_sample(qa.reshape(bs, SAMPLE_PAD, D_A), cache_kt, cache_vt, ka3, va3, layer=l)
        ob, s_fin = _gla(qb, kb, vb, la, rb, gg, s0_s[l], layer=l, n_seq=bs, seqs_per_step=math.gcd(bs, GLA_SAMPLE_SEQS),
                         rows_per_seq_step=SAMPLE_PAD, chunk=SAMPLE_PAD, sub=SAMPLE_PAD, t_valid=ts)
        xs, u_all = _mix_ffn(oa.reshape(n_s, D_A), ob, xs, w_out_bf, g_post, g_f1, w_up_bf, conv_w, cb, w_dn_bf,
                             g_f2, layer=l, tm=n_s, prev=prev_s)
        outs_s["k"].append(ka3[:, :ts].reshape(bs, ts, N_HEADS_A, HEAD_DIM_A))
        outs_s["v"].append(va3[:, :ts].reshape(bs, ts, N_HEADS_A, HEAD_DIM_A))
        outs_s["s"].append(s_fin.reshape(bs, N_HEADS_B, DK_HEAD, DV_HEAD))
        outs_s["c"].append(u_all.reshape(bs, SAMPLE_PAD, width)[:, ts - (conv_taps - 1):ts])

    y_prompt = xp.reshape(b, t, d)
    y_sample = xs.reshape(bs, SAMPLE_PAD, d)[:, :ts]
    st = lambda xs_: jnp.stack(xs_)
    to_rows = lambda a: jnp.transpose(a.reshape(depth, b, N_HEADS_A, HEAD_DIM_A, win_p), (0, 1, 4, 2, 3))
    return (y_prompt, y_sample, to_rows(win_bufs[0]), to_rows(win_bufs[1]), st(outs_p["s"]), st(outs_p["c"]),
            st(outs_s["k"]), st(outs_s["v"]), st(outs_s["s"]), st(outs_s["c"]))
```

```python
import functools
import math

import jax
import jax.numpy as jnp
from jax import lax
from jax.experimental import pallas as pl
from jax.experimental.pallas import tpu as pltpu

F32 = jnp.float32
BF16 = jnp.bfloat16

PAST_LEN = 16384
N_HEADS_A = 8
HEAD_DIM_A = 64
D_A = N_HEADS_A * HEAD_DIM_A
ROT_HALF = HEAD_DIM_A // 8
ROPE_THETA = 500000.0
WIN_DENSE, WIN_MID, WIN_FAR = 128, 512, 2048
DILATIONS = (1, 4, 16)
BAND = 128
N_HEADS_B = 4
DK_HEAD = 64
DV_HEAD = 128
DK_B = N_HEADS_B * DK_HEAD
DV_B = N_HEADS_B * DV_HEAD
N_MAIN = 3 * D_A + 2 * DK_B + 2 * DV_B
GATE_RANK = 16
GATE_TAU = 16.0
GLA_CHUNK = 64
GLA_SUB = 16
EPS = 1e-6
NEG_BIG = -1e30

LANES = 128
SUBLANES = 8
VMEM_LIMIT = 56 * 1024 * 1024
SAMPLE_PAD = SUBLANES
FF_CHUNK = 2 * LANES
INPROJ_ROWS = 256
GLA_ROWS = 1024
GLA_SAMPLE_SEQS = 8
FFN_ROWS = 512
MERGE_ROWS = 256
ATT_UNROLL = 8

_NT = (((1,), (1,)), ((), ()))
_TN = (((0,), (0,)), ((), ()))


def _rms(x, g):
    ms = jnp.mean(x * x, axis=-1, keepdims=True)
    return x * lax.rsqrt(ms + EPS) * g


def _split3(x):
    hi = x.astype(BF16)
    r1 = x - hi.astype(F32)
    mid = r1.astype(BF16)
    lo = (r1 - mid.astype(F32)).astype(BF16)
    return hi, mid, lo


def _dilation_multiplicity(d):
    nonneg = d >= 0
    m0 = nonneg & (d <= WIN_DENSE)
    m1 = nonneg & (d <= WIN_MID) & ((d & 3) == 0)
    m2 = nonneg & (d <= WIN_FAR) & ((d & 15) == 0)
    return m0.astype(F32) + m1.astype(F32) + m2.astype(F32)


def _inproj_kernel(x_ref, g_ref, w_ref, wg1_ref, wg2_ref, bg_ref, c_ref, su_ref, sd_ref, *refs,
                   prompt, layer, n_alias, gla_chunk, gla_valid):
    refs = refs[n_alias:]
    if prompt:
        sorted_refs, rest = refs[:9], refs[9:]
        qb_ref, kb_ref, vb_ref, rb_ref, la_ref, kt_ref, vt_ref, scr = rest
    else:
        qa_ref, ka_ref, va_ref, qb_ref, kb_ref, vb_ref, rb_ref, la_ref = refs
    h = _rms(x_ref[...], g_ref[...]).astype(BF16)
    tm = h.shape[0]

    def proj(lo, hi):
        return jnp.dot(h, w_ref[:, lo:hi], preferred_element_type=F32)

    cos = c_ref[...]
    s_up = su_ref[...]
    s_dn = sd_ref[...]

    def rope(y):
        outs = []
        for j in range(D_A // LANES):
            yj = y[:, LANES * j:LANES * (j + 1)]
            outs.append(yj * cos + pltpu.roll(yj, ROT_HALF, 1) * s_up
                        + pltpu.roll(yj, LANES - ROT_HALF, 1) * s_dn)
        return jnp.concatenate(outs, axis=1)

    gate_lr = jnp.dot(h, wg1_ref[...], preferred_element_type=F32)
    z = jnp.dot(gate_lr.astype(BF16), wg2_ref[...], preferred_element_type=F32) + bg_ref[...]
    log_sig = jnp.minimum(z, 0.0) - jnp.log1p(jnp.exp(-jnp.abs(z)))
    log_a = log_sig * (1.0 / GATE_TAU)
    r = lax.broadcasted_iota(jnp.int32, (tm, tm), 0)
    c = lax.broadcasted_iota(jnp.int32, (tm, tm), 1)
    tri = ((c <= r) & (c // gla_chunk == r // gla_chunk) & (c % gla_chunk < gla_valid)).astype(BF16)
    la_ref[...] = sum(jnp.dot(tri, piece, preferred_element_type=F32) for piece in _split3(log_a))

    qkv = (rope(proj(0, D_A)) * (HEAD_DIM_A ** -0.5), rope(proj(D_A, 2 * D_A)), proj(2 * D_A, 3 * D_A))
    if prompt:
        for a, val in enumerate(qkv):
            sorted_refs[3 * a][0] = val.astype(BF16)
            for c in range(D_A // LANES):
                cl = slice(LANES * c, LANES * (c + 1))
                scr[a, c] = val[:, cl]
                for o_ref, s in zip(sorted_refs[3 * a + 1:3 * a + 3], DILATIONS[1:]):
                    for r in range(s):
                        o_ref[r, :, cl] = scr.at[a, c][pl.ds(r, tm // s, stride=s), :].astype(BF16)
    else:
        qa_ref[...], ka_ref[...], va_ref[...] = qkv
    o = 3 * D_A
    qb_ref[...] = proj(o, o + DK_B) * (DK_HEAD ** -0.5); o += DK_B
    kb_ref[...] = proj(o, o + DK_B); o += DK_B
    vb_ref[...] = proj(o, o + DV_B); o += DV_B
    rb_ref[...] = proj(o, o + DV_B)
    if prompt:
        for c in range(D_A // LANES):
            cr = slice(LANES * c, LANES * (c + 1))
            for ref, a in ((kt_ref, 1), (vt_ref, 2)):
                if n_alias:
                    ref[cr, :] = scr[a, c].T
                else:
                    for dl in range(ref.shape[0]):
                        ref[dl, cr, :] = scr[a, c].T if dl == layer else jnp.zeros((LANES, tm), F32)


def _inproj(x, g, w, wg1, wg2, bg, tabs, *, layer, tm, gla_chunk, gla_valid, n_seq=1, win_rows=0, win_prev=None):
    n, d = x.shape
    prompt = win_rows > 0
    t = n // n_seq
    tiles_per_seq = t // tm
    first_win_tile = tiles_per_seq - win_rows // tm
    assert win_rows % tm == 0 and tm % DILATIONS[-1] == 0
    n_tab = tabs[0].shape[0] // tm
    row = lambda i: (i, 0)
    tab = lambda i: (i % n_tab, 0)
    lay3 = lambda i: (layer, 0, 0)
    f32_rows = lambda w_: (jax.ShapeDtypeStruct((n, w_), F32), pl.BlockSpec((tm, w_), row))
    outs = []
    if prompt:
        tile4 = lambda i: (i // tiles_per_seq, 0, i % tiles_per_seq, 0)
        for _ in range(3):
            for s in DILATIONS:
                outs.append((jax.ShapeDtypeStruct((n_seq, s, t // s, D_A), BF16),
                             pl.BlockSpec((None, s, tm // s, D_A), tile4)))
    else:
        outs += [f32_rows(D_A)] * 3
    outs += [f32_rows(DK_B), f32_rows(DK_B), f32_rows(DV_B), f32_rows(DV_B), f32_rows(DK_B)]
    scratch = []
    if prompt:
        depth = w.shape[0]
        win_col = lambda i: jnp.maximum(i % tiles_per_seq - first_win_tile, 0)
        if win_prev is None:
            win_spec = pl.BlockSpec((depth, None, D_A, tm), lambda i: (0, i // tiles_per_seq, 0, win_col(i)))
        else:
            win_spec = pl.BlockSpec((None, None, D_A, tm), lambda i: (layer, i // tiles_per_seq, 0, win_col(i)))
        outs += [(jax.ShapeDtypeStruct((depth, n_seq, D_A, win_rows), F32), win_spec)] * 2
        scratch = [pltpu.VMEM((3, D_A // LANES, tm, LANES), F32)]
    alias_in = list(win_prev) if win_prev is not None else []
    n_in = 9
    aliases = {n_in + a: len(outs) - len(alias_in) + a for a in range(len(alias_in))}
    kern = functools.partial(_inproj_kernel, prompt=prompt, layer=layer, n_alias=len(alias_in),
                             gla_chunk=gla_chunk, gla_valid=gla_valid)
    return pl.pallas_call(
        kern,
        grid=(n // tm,),
        in_specs=[pl.BlockSpec((tm, d), row), pl.BlockSpec((None, 1, d), lay3),
                  pl.BlockSpec((None,) + w.shape[1:], lay3), pl.BlockSpec((None,) + wg1.shape[1:], lay3),
                  pl.BlockSpec((None,) + wg2.shape[1:], lay3),
                  pl.BlockSpec((None, 1, DK_B), lay3),
                  pl.BlockSpec((tm, LANES), tab), pl.BlockSpec((tm, LANES), tab), pl.BlockSpec((tm, LANES), tab)]
        + [pl.BlockSpec(memory_space=pl.ANY)] * len(alias_in),
        out_specs=tuple(spec for _, spec in outs),
        out_shape=tuple(shape for shape, _ in outs),
        input_output_aliases=aliases,
        scratch_shapes=scratch,
        compiler_params=pltpu.CompilerParams(dimension_semantics=("arbitrary",),
                                             vmem_limit_bytes=VMEM_LIMIT),
        name="inproj",
    )(x, g, w, wg1, wg2, bg, *tabs, *alias_in)


def _attn_prompt_kernel(*refs, t):
    sorted_refs, (bias_ref, o_ref, on, ln) = refs[:9], refs[9:]
    blk = BAND
    lane = lax.broadcasted_iota(jnp.int32, (1, LANES), 1)
    head0 = lane < HEAD_DIM_A
    head_sel = (head0, jnp.logical_not(head0))

    def band_blocks(pi, s, first_idx):
        q_ref, k_ref, v_ref = sorted_refs[pi], sorted_refs[3 + pi], sorted_refs[6 + pi]
        n_blk = t // s // blk
        blocks = []
        for u in range(ATT_UNROLL):
            idx = first_idx + u
            r = idx // n_blk
            j = idx % n_blk
            q0 = pl.multiple_of(j * blk, blk)
            w0 = pl.multiple_of(q0 - jnp.minimum(j, 1) * blk, blk)
            rows = pl.ds(q0, blk) if s == 1 else pl.ds(r + s * q0, blk, stride=s)
            blocks.append((q_ref[r, pl.ds(q0, blk), :], k_ref[r, pl.ds(w0, 2 * blk), :],
                           v_ref[r, pl.ds(w0, 2 * blk), :], bias_ref[jnp.minimum(j, 1)], rows))
        heads = [(u, sel) for u in range(ATT_UNROLL) for sel in head_sel]
        sc = [lax.dot_general(jnp.where(sel, blocks[u][0], jnp.zeros_like(blocks[u][0])), blocks[u][1], _NT,
                              preferred_element_type=F32) + blocks[u][3] for u, sel in heads]
        m = [jnp.max(x, axis=1, keepdims=True) for x in sc]
        p = [jnp.exp(x - mx) for x, mx in zip(sc, m)]
        l = [jnp.sum(x, axis=1, keepdims=True) for x in p]
        o = [jnp.dot(x.astype(BF16), blocks[u][2], preferred_element_type=F32) for x, (u, _) in zip(p, heads)]
        for u in range(ATT_UNROLL):
            rows = blocks[u][4]
            l_pair = jnp.where(head0, l[2 * u], l[2 * u + 1])
            on.at[pi][rows, :] = jnp.where(head0, o[2 * u], o[2 * u + 1]) / l_pair
            ln.at[pi][rows, :] = jnp.where(head0, m[2 * u], m[2 * u + 1]) + jnp.log(l_pair)

    for pi, s in enumerate(DILATIONS):
        def one_group(gi, c, pi=pi, s=s):
            band_blocks(pi, s, gi * ATT_UNROLL)
            return c

        lax.fori_loop(0, t // blk // ATT_UNROLL, one_group, 0)

    def merge(i, c):
        rows = pl.ds(pl.multiple_of(i * MERGE_ROWS, MERGE_ROWS), MERGE_ROWS)
        lse = [ln[pi, rows, :] for pi in range(len(DILATIONS))]
        top = jnp.maximum(jnp.maximum(lse[0], lse[1]), lse[2])
        w = [jnp.exp(x - top) for x in lse]
        num = w[0] * on[0, rows, :] + w[1] * on[1, rows, :] + w[2] * on[2, rows, :]
        o_ref[rows, :] = (num / (w[0] + w[1] + w[2])).astype(o_ref.dtype)
        return c

    lax.fori_loop(0, t // MERGE_ROWS, merge, 0)


def _band_bias():
    iq = jnp.arange(BAND, dtype=jnp.int32)[:, None]
    c = jnp.arange(2 * BAND, dtype=jnp.int32)[None, :]
    d = jnp.stack([iq - c, iq + BAND - c])
    return jnp.where((d >= 0) & (d <= BAND), 0.0, NEG_BIG).astype(F32)


def _attn_prompt(sorted_qkv):
    b, _, t, _ = sorted_qkv[0].shape
    n_pair = D_A // LANES
    n_pat = len(DILATIONS)
    bias = _band_bias()
    pair4 = lambda bi, p: (bi, 0, 0, p)
    in_specs = [pl.BlockSpec((None,) + a.shape[1:3] + (LANES,), pair4) for a in sorted_qkv]
    in_specs.append(pl.BlockSpec(bias.shape, lambda bi, p: (0, 0, 0)))
    return pl.pallas_call(
        functools.partial(_attn_prompt_kernel, t=t),
        grid=(b, n_pair),
        in_specs=in_specs,
        out_specs=pl.BlockSpec((None, t, LANES), lambda bi, p: (bi, 0, p)),
        out_shape=jax.ShapeDtypeStruct((b, t, D_A), BF16),
        scratch_shapes=[pltpu.VMEM((n_pat, t, LANES), F32)] * 2,
        compiler_params=pltpu.CompilerParams(dimension_semantics=("arbitrary", "arbitrary"),
                                             vmem_limit_bytes=VMEM_LIMIT),
        name="attn_prompt",
    )(*sorted_qkv, bias)


def _attn_sample_kernel(q_ref, kt_ref, vt_ref, kn_ref, vn_ref, o_ref, *, win):
    q = q_ref[...].astype(BF16)
    pad = jnp.zeros((LANES - SAMPLE_PAD, D_A), F32)
    kn_t = jnp.concatenate([kn_ref[...], pad], axis=0).T.astype(BF16)
    vn_t = jnp.concatenate([vn_ref[...], pad], axis=0).T.astype(BF16)
    n_keys = win + LANES
    tok = lax.broadcasted_iota(jnp.int32, (SAMPLE_PAD, n_keys), 0)
    key = lax.broadcasted_iota(jnp.int32, (SAMPLE_PAD, n_keys), 1)
    mult = _dilation_multiplicity(win + tok - key)
    live = mult > 0.0
    hs = [slice(HEAD_DIM_A * h, HEAD_DIM_A * (h + 1)) for h in range(N_HEADS_A)]
    kts = [jnp.concatenate([kt_ref[h].astype(BF16), kn_t[hs[h], :]], axis=1) for h in range(N_HEADS_A)]
    ss = [jnp.where(live, jnp.dot(q[:, hs[h]], kts[h], preferred_element_type=F32), NEG_BIG)
          for h in range(N_HEADS_A)]
    ps = []
    for s in ss:
        m = jnp.max(s, axis=1, keepdims=True)
        p = jnp.exp(s - m) * mult
        ps.append((p / jnp.sum(p, axis=1, keepdims=True)).astype(BF16))
    vts = [jnp.concatenate([vt_ref[h].astype(BF16), vn_t[hs[h], :]], axis=1) for h in range(N_HEADS_A)]
    head_cols = [lax.dot_general(vts[h], ps[h], _NT, preferred_element_type=F32) for h in range(N_HEADS_A)]
    o_ref[...] = jnp.concatenate(head_cols, axis=0).T


def _attn_sample(q, kt, vt, kn, vn, *, layer):
    bs = q.shape[0]
    win = kt.shape[-1]
    seq = lambda i: (i, 0, 0)
    slab = lambda i: (layer, i, 0, 0, 0)
    return pl.pallas_call(
        functools.partial(_attn_sample_kernel, win=win),
        grid=(bs,),
        in_specs=[pl.BlockSpec((None, SAMPLE_PAD, D_A), seq),
                  pl.BlockSpec((None, None, N_HEADS_A, HEAD_DIM_A, win), slab),
                  pl.BlockSpec((None, None, N_HEADS_A, HEAD_DIM_A, win), slab),
                  pl.BlockSpec((None, SAMPLE_PAD, D_A), seq), pl.BlockSpec((None, SAMPLE_PAD, D_A), seq)],
        out_specs=pl.BlockSpec((None, SAMPLE_PAD, D_A), seq),
        out_shape=jax.ShapeDtypeStruct((bs, SAMPLE_PAD, D_A), F32),
        compiler_params=pltpu.CompilerParams(dimension_semantics=("arbitrary",),
                                             vmem_limit_bytes=VMEM_LIMIT),
        name="attn_sample",
    )(q, kt, vt, kn, vn)


def _gla_kernel(q_ref, k_ref, v_ref, b_ref, r_ref, g_ref, s0_ref, o_ref, sf_ref, s_scr,
                *, chunk, sub, n_seqs, n_chunks, t_valid):
    n_pair = N_HEADS_B // 2
    pair_v = 2 * DV_HEAD
    chains = [(sq, p) for sq in range(n_seqs) for p in range(n_pair)]

    @pl.when(pl.program_id(1) == 0)
    def _():
        zero = jnp.zeros((DK_HEAD, DV_HEAD), F32)
        for sq, p in chains:
            s0 = s0_ref[sq, p]
            s_scr[sq, p] = jnp.concatenate([jnp.concatenate([s0[:DK_HEAD], zero], axis=1),
                                            jnp.concatenate([zero, s0[DK_HEAD:]], axis=1)], axis=0)

    n_sub = chunk // sub
    row = lax.broadcasted_iota(jnp.int32, (chunk, 1), 0)
    lane = lax.broadcasted_iota(jnp.int32, (1, LANES), 1)
    head0 = lane < DK_HEAD
    causal = (lax.broadcasted_iota(jnp.int32, (chunk, 2 * chunk), 1) % chunk
              <= lax.broadcasted_iota(jnp.int32, (chunk, 2 * chunk), 0))
    own_block = (lax.broadcasted_iota(jnp.int32, (LANES, pair_v), 0) // DK_HEAD
                 == lax.broadcasted_iota(jnp.int32, (LANES, pair_v), 1) // DV_HEAD)
    g = g_ref[...]
    keep = row < t_valid
    zero_v = jnp.zeros((chunk, DV_HEAD), BF16)

    items = [(sq, c, p) for c in range(n_chunks) for sq, p in chains]
    rows_of = lambda sq, c: slice((sq * n_chunks + c) * chunk, (sq * n_chunks + c + 1) * chunk)
    b_, q_, k_, v_ = {}, {}, {}, {}
    for sq, c, p in items:
        ls = slice(LANES * p, LANES * (p + 1))
        b_[sq, c, p] = b_ref[rows_of(sq, c), ls]
        q_[sq, c, p] = q_ref[rows_of(sq, c), ls]
        kp = k_ref[rows_of(sq, c), ls]
        k_[sq, c, p] = jnp.where(keep, kp, 0.0) if t_valid < chunk else kp
        v_[sq, c, p] = v_ref[rows_of(sq, c), pair_v * p:pair_v * (p + 1)].astype(BF16)
    b_last = {it: b_[it][chunk - 1:chunk, :] for it in items}
    q_dec = {it: (q_[it] * jnp.exp(b_[it])).astype(BF16) for it in items}
    k_hat = {it: (k_[it] * jnp.exp(b_last[it] - b_[it])).astype(BF16) for it in items}
    q_sub, k_sub = {}, {}
    for it in items:
        b = b_[it]
        for i in range(n_sub):
            r0, r1 = sub * i, sub * (i + 1)
            beta = b[r0:r0 + 1, :]
            q_sub[it, i] = (q_[it][r0:r1] * jnp.exp(b[r0:r1] - beta)).astype(BF16)
            k_i = (k_[it][:r1] * jnp.exp(beta - b[:r1])).astype(BF16)
            if r1 < chunk:
                k_i = jnp.concatenate([k_i, jnp.zeros((chunk - r1, LANES), BF16)], axis=0)
            k_sub[it, i] = jnp.concatenate([jnp.where(head0, k_i, jnp.zeros_like(k_i)),
                                            jnp.where(head0, jnp.zeros_like(k_i), k_i)], axis=0)
    parts = {key: lax.dot_general(q_sub[key], k_sub[key], _NT, preferred_element_type=F32) for key in q_sub}
    attn = {}
    for it in items:
        a = parts[it, 0] if n_sub == 1 else jnp.concatenate([parts[it, i] for i in range(n_sub)], axis=0)
        attn[it] = jnp.where(causal, a, 0.0).astype(BF16)
    v_bd = {it: jnp.concatenate([jnp.concatenate([v_[it][:, :DV_HEAD], zero_v], axis=1),
                                 jnp.concatenate([zero_v, v_[it][:, DV_HEAD:]], axis=1)], axis=0) for it in items}
    o_intra = {it: jnp.dot(attn[it], v_bd[it], preferred_element_type=F32) for it in items}
    ds_ = {it: lax.dot_general(k_hat[it], v_[it], _TN, preferred_element_type=F32) for it in items}
    decay_col = {it: jnp.exp(jnp.broadcast_to(b_last[it], (SUBLANES, LANES)).T[:, 0:1]) for it in items}
    state = {ch: s_scr[ch] for ch in chains}
    for it in items:
        sq, c, p = it
        st = state[sq, p]
        o_pair = o_intra[it] + jnp.dot(q_dec[it], st.astype(BF16), preferred_element_type=F32)
        state[sq, p] = jnp.where(own_block, decay_col[it] * st + ds_[it], 0.0)
        for hh in range(2):
            vs = slice(DV_HEAD * (2 * p + hh), DV_HEAD * (2 * p + hh + 1))
            r = r_ref[rows_of(sq, c), vs]
            o = o_pair[:, DV_HEAD * hh:DV_HEAD * (hh + 1)]
            o_ref[rows_of(sq, c), vs] = (_rms(o, g) * (r * (1.0 / (1.0 + jnp.exp(-r))))).astype(o_ref.dtype)
    for ch in chains:
        s_scr[ch] = state[ch]
        sf_ref[ch] = jnp.concatenate([state[ch][:DK_HEAD, :DV_HEAD], state[ch][DK_HEAD:, DV_HEAD:]], axis=0)


def _gla(q, k, v, la, r, g, s0, *, layer, n_seq, seqs_per_step, rows_per_seq_step, chunk, sub, t_valid):
    n = q.shape[0]
    t = n // n_seq
    steps = t // rows_per_seq_step
    assert seqs_per_step == 1 or steps == 1
    rows_per_step = seqs_per_step * rows_per_seq_step
    row = lambda b, i: (b * steps + i, 0)
    st = lambda b, i: (b, 0, 0, 0)
    n_pair = N_HEADS_B // 2
    out_dtype = BF16 if chunk % (2 * SUBLANES) == 0 else F32
    kern = functools.partial(_gla_kernel, chunk=chunk, sub=sub, n_seqs=seqs_per_step,
                             n_chunks=rows_per_seq_step // chunk, t_valid=t_valid)
    return pl.pallas_call(
        kern,
        grid=(n_seq // seqs_per_step, steps),
        in_specs=[pl.BlockSpec((rows_per_step, DK_B), row), pl.BlockSpec((rows_per_step, DK_B), row),
                  pl.BlockSpec((rows_per_step, DV_B), row), pl.BlockSpec((rows_per_step, DK_B), row),
                  pl.BlockSpec((rows_per_step, DV_B), row),
                  pl.BlockSpec((None, 1, DV_HEAD), lambda b, i: (layer, 0, 0)),
                  pl.BlockSpec((seqs_per_step, n_pair, LANES, DV_HEAD), st)],
        out_specs=(pl.BlockSpec((rows_per_step, DV_B), row),
                   pl.BlockSpec((seqs_per_step, n_pair, LANES, DV_HEAD), st)),
        out_shape=(jax.ShapeDtypeStruct((n, DV_B), out_dtype),
                   jax.ShapeDtypeStruct((n_seq, n_pair, LANES, DV_HEAD), F32)),
        scratch_shapes=[pltpu.VMEM((seqs_per_step, n_pair, LANES, 2 * DV_HEAD), F32)],
        compiler_params=pltpu.CompilerParams(dimension_semantics=("arbitrary", "arbitrary"),
                                             vmem_limit_bytes=VMEM_LIMIT),
        name="gla",
    )(q, k, v, la, r, g, s0)


def _gelu_tanh(x):
    c = math.sqrt(2.0 / math.pi)
    return 0.5 * x * (1.0 + jnp.tanh(c * (x + 0.044715 * (x * x * x))))


def _causal_conv(u, prev, cw, cb):
    rows, width = u.shape
    tiles = rows // SUBLANES
    u3 = u.reshape(tiles, SUBLANES, width)
    p3 = prev.reshape(tiles, SUBLANES, width)
    r = lax.broadcasted_iota(jnp.int32, (tiles, SUBLANES, width), 1)
    back1 = pltpu.roll(jnp.where(r >= SUBLANES - 1, p3, u3), 1, 1).reshape(rows, width)
    back2 = pltpu.roll(jnp.where(r >= SUBLANES - 2, p3, u3), 2, 1).reshape(rows, width)
    return cb + cw[0:1, :] * back2 + cw[1:2, :] * back1 + cw[2:3, :] * u


def _mix_ffn_core(oa_ref, ob_ref, x_ref, wo_ref, gpost_ref, g1_ref, wup_ref, cw_ref, cb_ref, wdn_ref, g2_ref,
                  y_ref, prev_cols, emit_u):
    d_ff = wdn_ref.shape[0]
    m = (jnp.dot(oa_ref[...].astype(BF16), wo_ref[0:D_A, :], preferred_element_type=F32)
         + jnp.dot(ob_ref[...].astype(BF16), wo_ref[D_A:D_A + DV_B, :], preferred_element_type=F32))
    x1 = x_ref[...] + _rms(m, gpost_ref[...])
    h = _rms(x1, g1_ref[...]).astype(BF16)
    ys = []
    for j in range(d_ff // FF_CHUNK):
        halves = []
        for base in (0, d_ff):
            cols = slice(base + j * FF_CHUNK, base + (j + 1) * FF_CHUNK)
            u = jnp.dot(h, wup_ref[:, cols], preferred_element_type=F32)
            halves.append(_causal_conv(u, prev_cols(cols, u), cw_ref[:, cols], cb_ref[:, cols]))
            emit_u(cols, u)
        ys.append((_gelu_tanh(halves[0]) * halves[1]).astype(BF16))
    f = jnp.dot(jnp.concatenate(ys, axis=1), wdn_ref[...], preferred_element_type=F32)
    y_ref[...] = x1 + _rms(f, g2_ref[...])


def _mix_ffn_prompt_kernel(*refs, tiles_per_seq):
    in_refs, (y_ref, tail_ref, carry) = refs[:11], refs[11:]
    tm = y_ref.shape[0]

    @pl.when(pl.program_id(0) % tiles_per_seq == 0)
    def _():
        carry[...] = jnp.zeros_like(carry)

    def prev_cols(cols, u):
        return jnp.concatenate([carry[:, cols], u[:tm - SUBLANES]], axis=0)

    def emit_u(cols, u):
        tail = u[tm - SUBLANES:]
        carry[:, cols] = tail
        tail_ref[:, cols] = tail

    _mix_ffn_core(*in_refs, y_ref, prev_cols, emit_u)


def _mix_ffn_sample_kernel(*refs):
    in_refs, (prev_ref, y_ref, u_ref) = refs[:11], refs[11:]

    def emit_u(cols, u):
        u_ref[:, cols] = u

    _mix_ffn_core(*in_refs, y_ref, lambda cols, u: prev_ref[:, cols], emit_u)


def _mix_ffn(oa, ob, x, wo, gpost, g1, wup, cw, cb, wdn, g2, *, layer, tm, n_seq=1, prev=None):
    n, d = x.shape
    width = wup.shape[-1]
    row = lambda i: (i, 0)
    lay3 = lambda i: (layer, 0, 0)
    single = pl.Buffered(1)
    weight = lambda a: pl.BlockSpec((None,) + a.shape[1:], lay3, pipeline_mode=single)
    small = lambda a: pl.BlockSpec((None,) + a.shape[1:], lay3)
    in_specs = [pl.BlockSpec((tm, D_A), row), pl.BlockSpec((tm, DV_B), row), pl.BlockSpec((tm, d), row),
                weight(wo), small(gpost), small(g1), weight(wup), small(cw), small(cb), weight(wdn), small(g2)]
    args = [oa, ob, x, wo, gpost, g1, wup, cw, cb, wdn, g2]
    if prev is None:
        tiles_per_seq = n // n_seq // tm
        kern = functools.partial(_mix_ffn_prompt_kernel, tiles_per_seq=tiles_per_seq)
        out_specs = (pl.BlockSpec((tm, d), row),
                     pl.BlockSpec((None, SUBLANES, width), lambda i: (i // tiles_per_seq, 0, 0)))
        out_shape = (jax.ShapeDtypeStruct((n, d), F32), jax.ShapeDtypeStruct((n_seq, SUBLANES, width), F32))
        scratch = [pltpu.VMEM((SUBLANES, width), F32)]
    else:
        kern = _mix_ffn_sample_kernel
        in_specs.append(pl.BlockSpec((None, tm, width), lambda i: (layer, i, 0)))
        args.append(prev)
        out_specs = (pl.BlockSpec((tm, d), row), pl.BlockSpec((tm, width), row))
        out_shape = (jax.ShapeDtypeStruct((n, d), F32), jax.ShapeDtypeStruct((n, width), F32))
        scratch = []
    return pl.pallas_call(
        kern,
        grid=(n // tm,),
        in_specs=in_specs,
        out_specs=out_specs,
        out_shape=out_shape,
        scratch_shapes=scratch,
        compiler_params=pltpu.CompilerParams(dimension_semantics=("arbitrary",),
                                             vmem_limit_bytes=VMEM_LIMIT),
        name="mix_ffn",
    )(*args)


def _rope_tables(pos):
    inv_freq = ROPE_THETA ** (-jnp.arange(ROT_HALF, dtype=F32) / ROT_HALF)
    ang = pos.astype(F32)[:, None] * inv_freq[None, :]
    cos, sin = jnp.cos(ang), jnp.sin(ang)
    n = pos.shape[0]
    rest = HEAD_DIM_A - 2 * ROT_HALF
    reps = LANES // HEAD_DIM_A
    c = jnp.tile(jnp.concatenate([cos, cos, jnp.ones((n, rest), F32)], axis=1), (1, reps))
    s_up = jnp.tile(jnp.concatenate([jnp.zeros((n, ROT_HALF), F32), sin, jnp.zeros((n, rest), F32)], axis=1),
                    (1, reps))
    s_dn = jnp.tile(jnp.concatenate([-sin, jnp.zeros((n, ROT_HALF + rest), F32)], axis=1), (1, reps))
    return c, s_up, s_dn


def kernel(x_prompt, x_sample, cache_k_win, cache_v_win, state_gla, state_ffn_conv, g_mix_pre, g_mix_post,
           g_ffn_pre, g_ffn_post, w_in, w_gate2, b_gate, g_gla, w_out, w_up, conv_w, conv_b, w_down):
    depth = w_in.shape[0]
    b, t, d = x_prompt.shape
    bs, ts, _ = x_sample.shape
    win = cache_k_win.shape[2]
    width = w_up.shape[-1]
    conv_taps = conv_w.shape[1]
    n_pair = N_HEADS_B // 2
    win_p = min(win, t)
    n_s = bs * SAMPLE_PAD

    w_in_bf = w_in.astype(BF16)
    wg1_bf = jnp.pad(w_in[:, :, N_MAIN:], ((0, 0), (0, 0), (0, LANES - GATE_RANK))).astype(BF16)
    wg2_bf = jnp.pad(w_gate2, ((0, 0), (0, LANES - GATE_RANK), (0, 0))).astype(BF16)
    w_out_bf, w_up_bf, w_dn_bf = w_out.astype(BF16), w_up.astype(BF16), w_down.astype(BF16)
    per_layer_row = lambda a: a[:, None, :]
    bg, gg, cb = per_layer_row(b_gate), per_layer_row(g_gla), per_layer_row(conv_b)
    g_pre, g_post = per_layer_row(g_mix_pre), per_layer_row(g_mix_post)
    g_f1, g_f2 = per_layer_row(g_ffn_pre), per_layer_row(g_ffn_post)

    tabs_p = _rope_tables(jnp.arange(t, dtype=jnp.int32))
    pos_s = PAST_LEN + jnp.arange(SAMPLE_PAD, dtype=jnp.int32)
    tabs_s = tuple(jnp.tile(a, (bs, 1)) for a in _rope_tables(pos_s))

    xp = x_prompt.reshape(b * t, d)
    xs = jnp.pad(x_sample, ((0, 0), (0, SAMPLE_PAD - ts), (0, 0))).reshape(n_s, d)

    cache_kt = jnp.transpose(cache_k_win, (0, 1, 3, 4, 2))
    cache_vt = jnp.transpose(cache_v_win, (0, 1, 3, 4, 2))
    prev_s = jnp.pad(state_ffn_conv, ((0, 0), (0, 0), (SUBLANES - (conv_taps - 1), 0), (0, 0)))
    prev_s = prev_s.reshape(depth, n_s, width)
    s0_p = jnp.zeros((b, n_pair, LANES, DV_HEAD), F32)
    s0_s = state_gla.reshape(depth, bs, n_pair, LANES, DV_HEAD)

    outs_p = {"s": [], "c": []}
    outs_s = {"k": [], "v": [], "s": [], "c": []}
    win_bufs = None
    for l in range(depth):
        *sorted_qkv, qb, kb, vb, rb, la, kt_win, vt_win = _inproj(
            xp, g_pre, w_in_bf, wg1_bf, wg2_bf, bg, tabs_p, layer=l, tm=INPROJ_ROWS, gla_chunk=GLA_CHUNK,
            gla_valid=GLA_CHUNK, n_seq=b, win_rows=win_p, win_prev=win_bufs)
        win_bufs = (kt_win, vt_win)
        q_s, k_s, v_s = sorted_qkv[0:3], sorted_qkv[3:6], sorted_qkv[6:9]
        oa = _attn_prompt(q_s + k_s + v_s)
        ob, s_fin = _gla(qb, kb, vb, la, rb, gg, s0_p, layer=l, n_seq=b, seqs_per_step=1,
                         rows_per_seq_step=GLA_ROWS, chunk=GLA_CHUNK, sub=GLA_SUB, t_valid=GLA_CHUNK)
        xp, tail = _mix_ffn(oa.reshape(b * t, D_A), ob, xp, w_out_bf, g_post, g_f1, w_up_bf, conv_w, cb, w_dn_bf,
                            g_f2, layer=l, tm=FFN_ROWS, n_seq=b)
        outs_p["s"].append(s_fin.reshape(b, N_HEADS_B, DK_HEAD, DV_HEAD))
        outs_p["c"].append(tail[:, SUBLANES - (conv_taps - 1):])

        qa, ka, va, qb, kb, vb, rb, la = _inproj(xs, g_pre, w_in_bf, wg1_bf, wg2_bf, bg, tabs_s, layer=l, tm=n_s,
                                                 gla_chunk=SAMPLE_PAD, gla_valid=ts)
        ka3 = ka.reshape(bs, SAMPLE_PAD, D_A)
        va3 = va.reshape(bs, SAMPLE_PAD, D_A)
        oa = _attn_sample(qa.reshape(bs, SAMPLE_PAD, D_A), cache_kt, cache_vt, ka3, va3, layer=l)
        ob, s_fin = _gla(qb, kb, vb, la, rb, gg, s0_s[l], layer=l, n_seq=bs, seqs_per_step=math.gcd(bs, GLA_SAMPLE_SEQS),
                         rows_per_seq_step=SAMPLE_PAD, chunk=SAMPLE_PAD, sub=SAMPLE_PAD, t_valid=ts)
        xs, u_all = _mix_ffn(oa.reshape(n_s, D_A), ob, xs, w_out_bf, g_post, g_f1, w_up_bf, conv_w, cb, w_dn_bf,
                             g_f2, layer=l, tm=n_s, prev=prev_s)
        outs_s["k"].append(ka3[:, :ts].reshape(bs, ts, N_HEADS_A, HEAD_DIM_A))
        outs_s["v"].append(va3[:, :ts].reshape(bs, ts, N_HEADS_A, HEAD_DIM_A))
        outs_s["s"].append(s_fin.reshape(bs, N_HEADS_B, DK_HEAD, DV_HEAD))
        outs_s["c"].append(u_all.reshape(bs, SAMPLE_PAD, width)[:, ts - (conv_taps - 1):ts])

    y_prompt = xp.reshape(b, t, d)
    y_sample = xs.reshape(bs, SAMPLE_PAD, d)[:, :ts]
    st = lambda xs_: jnp.stack(xs_)
    to_rows = lambda a: jnp.transpose(a.reshape(depth, b, N_HEADS_A, HEAD_DIM_A, win_p), (0, 1, 4, 2, 3))
    return (y_prompt, y_sample, to_rows(win_bufs[0]), to_rows(win_bufs[1]), st(outs_p["s"]), st(outs_p["c"]),
            st(outs_s["k"]), st(outs_s["v"]), st(outs_s["s"]), st(outs_s["c"]))
```
